```python
import jax, jax.numpy as jnp
from jax import lax
import numpy as np

D_MODEL = 1024
BATCH = 8
SEQ = 4096
DEPTH = 4

HEAD_DIM = 64
N_HEADS_GROUP = 4
GROUP_WIDTH = N_HEADS_GROUP * HEAD_DIM
N_MIXERS = 4
D_MIX = N_MIXERS * GROUP_WIDTH
D_PLE = 256

RWKV_DECAY_RANK = 32
RWKV_ICL_RANK = 32
RWKV_GATE_RANK = 64
RWKV_GN_EPS = 64e-5
RWKV_WIDTHS = (GROUP_WIDTH, GROUP_WIDTH, GROUP_WIDTH, RWKV_DECAY_RANK, RWKV_ICL_RANK, RWKV_GATE_RANK)
D_IN_A = 3 * GROUP_WIDTH + RWKV_DECAY_RANK + RWKV_ICL_RANK + RWKV_GATE_RANK

GLA_KEY_DIM = 32
GLA_QK_WIDTH = N_HEADS_GROUP * GLA_KEY_DIM
GLA_GATE_RANK = 16
GLA_TAU = 16.0
GLA_CHUNK = 64
GLA_WIDTHS = (GLA_QK_WIDTH, GLA_QK_WIDTH, GROUP_WIDTH, GLA_GATE_RANK, GROUP_WIDTH)
D_IN_B = 2 * GLA_QK_WIDTH + 2 * GROUP_WIDTH + GLA_GATE_RANK

MLSTM_CONV = 4
MLSTM_CHUNK = 64
MLSTM_WIDTHS = (GROUP_WIDTH, GROUP_WIDTH, GROUP_WIDTH, GROUP_WIDTH, N_HEADS_GROUP, N_HEADS_GROUP)
D_IN_C = 4 * GROUP_WIDTH + 2 * N_HEADS_GROUP

MLA_Q_RANK = 256
MLA_KV_RANK = 128
MLA_NOPE = 64
MLA_ROPE = 32
MLA_V = HEAD_DIM
MLA_WIDTHS = (MLA_Q_RANK, MLA_KV_RANK, MLA_ROPE)
D_IN_D = MLA_Q_RANK + MLA_KV_RANK + MLA_ROPE
ROPE_THETA = 10000.0
ATTN_BLOCK = 128

D_IN = D_IN_A + D_IN_B + D_IN_C + D_IN_D

N_EXPERT_GROUPS = 4
EXPERTS_PER_GROUP = 8
N_EXPERTS = N_EXPERT_GROUPS * EXPERTS_PER_GROUP
TOP_K = 2
D_EXPERT = 512
MOE_BLOCK = 256

DN_ALPHA = (2.0 * DEPTH) ** 0.25
DN_BETA = (8.0 * DEPTH) ** -0.25
LN_EPS = 1e-5
NORM_EPS = 1e-6

kernel_name = 'hybrid_parallel_groups_hmoe_deepnorm'


def split_last(t, widths):
    return jnp.split(t, np.cumsum(widths)[:-1].tolist(), axis=-1)


def heads(t, d):
    return t.reshape(t.shape[:-1] + (t.shape[-1] // d, d))


def layer_norm(x, g, b, eps):
    xf = x.astype(jnp.float32)
    xc = xf - jnp.mean(xf, axis=-1, keepdims=True)
    y = xc * lax.rsqrt(jnp.mean(xc * xc, axis=-1, keepdims=True) + eps) * g
    if b is not None:
        y = y + b
    return y.astype(x.dtype)


def rms_norm(x, g, eps):
    xf = x.astype(jnp.float32)
    return (xf * lax.rsqrt(jnp.mean(xf * xf, axis=-1, keepdims=True) + eps) * g).astype(x.dtype)


def token_shift(x):
    return jnp.pad(x, ((0, 0), (1, 0), (0, 0)))[:, :-1]


def causal_depthwise_conv(x, w, b):
    width, ch = w.shape
    y = lax.conv_general_dilated(x, w[:, None, :].astype(x.dtype), window_strides=(1,),
                                 padding=((width - 1, 0),), dimension_numbers=('NWC', 'WIO', 'NWC'),
                                 feature_group_count=ch)
    return y + b


def rope_cos_sin(positions):
    inv_freq = ROPE_THETA ** (-jnp.arange(0, MLA_ROPE, 2, dtype=jnp.float32) / MLA_ROPE)
    ang = positions.astype(jnp.float32)[..., None] * inv_freq
    return jnp.cos(ang), jnp.sin(ang)


def apply_rope(x, cos, sin):
    x1, x2 = jnp.split(x, 2, axis=-1)
    return jnp.concatenate([x1 * cos - x2 * sin, x1 * sin + x2 * cos], axis=-1).astype(x.dtype)


def rwkv7_scan(r, w, k, v, a, b):
    bsz, _, nh, n = r.shape

    def step(state, inp):
        r_t, w_t, k_t, v_t, a_t, b_t = inp
        sa = jnp.einsum('bhvk,bhk->bhv', state, a_t)
        state = state * w_t[:, :, None, :] + sa[..., None] * b_t[:, :, None, :] + v_t[..., None] * k_t[:, :, None, :]
        return state, jnp.einsum('bhvk,bhk->bhv', state, r_t)

    seq = tuple(jnp.moveaxis(t, 1, 0) for t in (r, w, k, v, a, b))
    _, y = lax.scan(step, jnp.zeros((bsz, nh, n, n), jnp.float32), seq)
    return jnp.moveaxis(y, 0, 1)


def rwkv7_group(u, mu, w0, w_up, a0, a_up, g_up, k_k, k_a, r_k, gn_g, gn_b):
    bsz, s, _ = u.shape
    u = u + (token_shift(u) - u) * mu
    r, k, v, wd, ad, gd = split_last(u, RWKV_WIDTHS)
    log_neg_logw = -jax.nn.softplus(-(w0 + jnp.tanh(wd) @ w_up)) - 0.5
    decay = jnp.exp(-jnp.exp(log_neg_logw.astype(jnp.float32)))
    a = jax.nn.sigmoid(a0 + ad @ a_up)
    g = jax.nn.sigmoid(gd) @ g_up
    kk = heads(k * k_k, HEAD_DIM).astype(jnp.float32)
    kk = kk / jnp.maximum(jnp.sqrt(jnp.sum(kk * kk, axis=-1, keepdims=True)), 1e-12)
    k = k * (1.0 + (a - 1.0) * k_a)
    rh, kh, vh, ah = (heads(t, HEAD_DIM).astype(jnp.float32) for t in (r, k, v, a))
    y = rwkv7_scan(rh, heads(decay, HEAD_DIM), kh, vh, -kk, kk * ah)
    y = layer_norm(y, gn_g.reshape(N_HEADS_GROUP, HEAD_DIM), gn_b.reshape(N_HEADS_GROUP, HEAD_DIM), RWKV_GN_EPS)
    y = y + jnp.sum(rh * kh * r_k, axis=-1, keepdims=True) * vh
    return (y.reshape(bsz, s, GROUP_WIDTH) * g).astype(u.dtype)


def gla_chunked(q, k, v, log_a):
    bsz, s, nh, dk = q.shape
    dv = v.shape[-1]
    n = s // GLA_CHUNK

    def chunks(t):
        return jnp.moveaxis(t.astype(jnp.float32).reshape(bsz, n, GLA_CHUNK, nh, t.shape[-1]), 1, 0)

    causal = jnp.tril(jnp.ones((GLA_CHUNK, GLA_CHUNK), dtype=bool))[None, :, :, None, None]

    def step(state, inp):
        qc, kc, vc, lac = inp
        b = jnp.cumsum(lac, axis=1)
        rel = jnp.exp(jnp.where(causal, b[:, :, None] - b[:, None, :], -jnp.inf))
        scores = jnp.einsum('bihd,bjhd,bijhd->bhij', qc, kc, rel)
        o = jnp.einsum('bhij,bjhv->bihv', scores, vc) + jnp.einsum('bihd,bhdv->bihv', qc * jnp.exp(b), state)
        b_end = b[:, -1]
        state = state * jnp.exp(b_end)[..., None] + jnp.einsum('bjhd,bjhv->bhdv', kc * jnp.exp(b_end[:, None] - b), vc)
        return state, o

    _, o = lax.scan(step, jnp.zeros((bsz, nh, dk, dv), jnp.float32), (chunks(q), chunks(k), chunks(v), chunks(log_a)))
    return jnp.moveaxis(o, 0, 1).reshape(bsz, s, nh, dv)


def gla_group(u, alpha_up, alpha_b, norm_g):
    bsz, s, _ = u.shape
    q, k, v, ad, gate = split_last(u, GLA_WIDTHS)
    log_a = jax.nn.log_sigmoid((ad @ alpha_up + alpha_b).astype(jnp.float32)) / GLA_TAU
    o = gla_chunked(heads(q, GLA_KEY_DIM) * GLA_KEY_DIM ** -0.5, heads(k, GLA_KEY_DIM),
                    heads(v, HEAD_DIM), heads(log_a, GLA_KEY_DIM))
    o = rms_norm(o, norm_g.reshape(N_HEADS_GROUP, HEAD_DIM), NORM_EPS)
    return (o.reshape(bsz, s, GROUP_WIDTH) * jax.nn.silu(gate)).astype(u.dtype)


def mlstm_chunked(q, k, v, log_i, log_f):
    bsz, s, nh, d = q.shape
    c = MLSTM_CHUNK
    n = s // c

    def chunks(t):
        return jnp.moveaxis(t.reshape((bsz, n, c) + t.shape[2:]), 1, 0)

    causal = jnp.tril(jnp.ones((c, c), dtype=bool))

    def step(carry, inp):
        mem, nrm, m = carry
        qc, kc, vc, li, lf = inp
        li = jnp.swapaxes(li, 1, 2)
        f_cum = jnp.cumsum(jnp.swapaxes(lf, 1, 2), axis=-1)
        log_w = jnp.where(causal, f_cum[..., :, None] - f_cum[..., None, :] + li[..., None, :], -jnp.inf)
        log_carry = f_cum + m[..., None]
        m_row = jnp.maximum(jnp.max(log_w, axis=-1), log_carry)
        sc = jnp.einsum('bihd,bjhd->bhij', qc, kc) * jnp.exp(log_w - m_row[..., None])
        w_c = jnp.exp(log_carry - m_row)
        num = jnp.einsum('bhij,bjhd->bhid', sc, vc) + w_c[..., None] * jnp.einsum('bihd,bhde->bhie', qc, mem)
        den = jnp.sum(sc, axis=-1) + w_c * jnp.einsum('bihd,bhd->bhi', qc, nrm)
        h = num / jnp.maximum(jnp.abs(den), jnp.exp(-m_row))[..., None]
        f_end = f_cum[..., -1]
        log_kv = f_end[..., None] - f_cum + li
        m_new = jnp.maximum(f_end + m, jnp.max(log_kv, axis=-1))
        kw = jnp.exp(log_kv - m_new[..., None])
        carry_decay = jnp.exp(f_end + m - m_new)
        mem = carry_decay[..., None, None] * mem + jnp.einsum('bhj,bjhd,bjhe->bhde', kw, kc, vc)
        nrm = carry_decay[..., None] * nrm + jnp.einsum('bhj,bjhd->bhd', kw, kc)
        return (mem, nrm, m_new), jnp.swapaxes(h, 1, 2)

    init = (jnp.zeros((bsz, nh, d, d), jnp.float32), jnp.zeros((bsz, nh, d), jnp.float32),
            jnp.zeros((bsz, nh), jnp.float32))
    _, h = lax.scan(step, init, tuple(chunks(t) for t in (q, k, v, log_i, log_f)))
    return jnp.moveaxis(h, 0, 1).reshape(bsz, s, nh, d)


def mlstm_group(u, conv_w, conv_b, i_b, f_b, norm_g):
    bsz, s, _ = u.shape
    q, k, v, o, ig, fg = split_last(u, MLSTM_WIDTHS)
    qk = jax.nn.silu(causal_depthwise_conv(jnp.concatenate([q, k], axis=-1), conv_w, conv_b))
    q, k = jnp.split(qk, 2, axis=-1)
    log_i = (ig + i_b).astype(jnp.float32)
    log_f = jax.nn.log_sigmoid((fg + f_b).astype(jnp.float32))
    h = mlstm_chunked(heads(q, HEAD_DIM).astype(jnp.float32),
                      heads(k, HEAD_DIM).astype(jnp.float32) * HEAD_DIM ** -0.5,
                      heads(v, HEAD_DIM).astype(jnp.float32), log_i, log_f)
    h = layer_norm(h, norm_g.reshape(N_HEADS_GROUP, HEAD_DIM), None, LN_EPS)
    return (h.reshape(bsz, s, GROUP_WIDTH) * jax.nn.sigmoid(o)).astype(u.dtype)


def mla_group(u, cos, sin, q_norm_g, w_uq, kv_norm_g, w_ukv):
    bsz, s, _ = u.shape
    cq, ckv, kr = split_last(u, MLA_WIDTHS)
    q = heads(rms_norm(cq, q_norm_g, NORM_EPS) @ w_uq, MLA_NOPE + MLA_ROPE)
    kv = heads(rms_norm(ckv, kv_norm_g, NORM_EPS) @ w_ukv, MLA_NOPE + MLA_V)
    q_nope, q_rope = jnp.split(q, [MLA_NOPE], axis=-1)
    k_nope, v = jnp.split(kv, [MLA_NOPE], axis=-1)
    q = jnp.concatenate([q_nope, apply_rope(q_rope, cos[:, :, None], sin[:, :, None])], axis=-1)
    k_rope = apply_rope(kr, cos, sin)[:, :, None]
    k = jnp.concatenate([k_nope, jnp.broadcast_to(k_rope, (bsz, s, N_HEADS_GROUP, MLA_ROPE))], axis=-1)
    scale = (MLA_NOPE + MLA_ROPE) ** -0.5
    outs = []
    for start in range(0, s, ATTN_BLOCK):
        end = start + ATTN_BLOCK
        sc = jnp.einsum('bqhd,bkhd->bhqk', q[:, start:end], k[:, :end]).astype(jnp.float32) * scale
        mask = jnp.arange(end)[None, :] <= jnp.arange(start, end)[:, None]
        probs = jax.nn.softmax(jnp.where(mask, sc, -jnp.inf), axis=-1).astype(v.dtype)
        outs.append(jnp.einsum('bhqk,bkhd->bqhd', probs, v[:, :end]))
    return jnp.concatenate(outs, axis=1).reshape(bsz, s, GROUP_WIDTH)


def hier_moe(x, w_rg, b_rg, w_re, b_re, w_gate, w_up, w_down):
    bsz, s, d = x.shape
    t_tok = bsz * s
    n_assign = t_tok * TOP_K
    xt = x.reshape(t_tok, d)
    group_prob = jax.nn.softmax((xt @ w_rg + b_rg).astype(jnp.float32), axis=-1)
    group_p, group_idx = lax.top_k(group_prob, 1)
    e_logits = (xt @ w_re + b_re).astype(jnp.float32).reshape(t_tok, N_EXPERT_GROUPS, EXPERTS_PER_GROUP)
    e_logits = jnp.take_along_axis(e_logits, group_idx[:, :, None], axis=1)[:, 0]
    expert_p, local_idx = lax.top_k(jax.nn.softmax(e_logits, axis=-1), TOP_K)
    gate = group_p * expert_p / jnp.sum(expert_p, axis=-1, keepdims=True)
    expert_idx = group_idx * EXPERTS_PER_GROUP + local_idx
    flat_e = expert_idx.reshape(n_assign)
    flat_t = jnp.repeat(jnp.arange(t_tok, dtype=jnp.int32), TOP_K)
    flat_g = gate.reshape(n_assign)
    order = jnp.argsort(flat_e)
    sorted_e = flat_e[order]
    counts = jnp.bincount(flat_e, length=N_EXPERTS)
    starts = jnp.cumsum(counts) - counts
    padded = (counts + MOE_BLOCK - 1) // MOE_BLOCK * MOE_BLOCK
    pad_ends = jnp.cumsum(padded)
    dest = pad_ends[sorted_e] - padded[sorted_e] + jnp.arange(n_assign) - starts[sorted_e]
    n_rows = -(-(n_assign + N_EXPERTS * (MOE_BLOCK - 1)) // MOE_BLOCK) * MOE_BLOCK
    n_blocks = n_rows // MOE_BLOCK
    row_tok = jnp.full((n_rows,), t_tok, jnp.int32).at[dest].set(flat_t[order])
    row_gate = jnp.zeros((n_rows,), x.dtype).at[dest].set(flat_g[order].astype(x.dtype))
    block_expert = jnp.minimum(jnp.searchsorted(pad_ends, jnp.arange(n_blocks) * MOE_BLOCK, side='right'), N_EXPERTS - 1)
    x_rows = jnp.concatenate([xt, jnp.zeros((1, d), xt.dtype)], axis=0)[row_tok].reshape(n_blocks, MOE_BLOCK, d)

    def expert_block(args):
        xb, e = args
        return (jax.nn.silu(xb @ w_gate[e]) * (xb @ w_up[e])) @ w_down[e]

    y_rows = lax.map(expert_block, (x_rows, block_expert)).reshape(n_rows, d)
    y = jax.ops.segment_sum(y_rows * row_gate[:, None], row_tok, num_segments=t_tok + 1)[:t_tok]
    return y.reshape(bsz, s, d)


def setup_inputs(seed: int = 0) -> dict:
    key = jax.random.key(seed)
    keys = iter(jax.random.split(key, 64))
    L = DEPTH
    f32 = jnp.float32
    D = D_MODEL

    def nrm(shape, scale):
        return jax.random.normal(next(keys), shape, f32) * scale

    x = nrm((BATCH, SEQ, D), 1.0)
    p = nrm((DEPTH, BATCH, SEQ, D_PLE), 1.0)
    positions = (jax.random.randint(next(keys), (BATCH, 1), 0, 1024, dtype=jnp.int32)
                 + jnp.arange(SEQ, dtype=jnp.int32)[None, :])
    return {
        'x': x,
        'p': p,
        'positions': positions,
        'w_in': nrm((L, D, D_IN), D ** -0.5),
        'rwkv_mu': jax.random.uniform(next(keys), (L, D_IN_A), f32),
        'rwkv_w0': nrm((L, GROUP_WIDTH), 0.5),
        'rwkv_w_up': nrm((L, RWKV_DECAY_RANK, GROUP_WIDTH), 0.5 * RWKV_DECAY_RANK ** -0.5),
        'rwkv_a0': nrm((L, GROUP_WIDTH), 0.5),
        'rwkv_a_up': nrm((L, RWKV_ICL_RANK, GROUP_WIDTH), 0.5 * RWKV_ICL_RANK ** -0.5),
        'rwkv_g_up': nrm((L, RWKV_GATE_RANK, GROUP_WIDTH), RWKV_GATE_RANK ** -0.5),
        'rwkv_k_k': 0.85 + nrm((L, GROUP_WIDTH), 0.02),
        'rwkv_k_a': 1.0 + nrm((L, GROUP_WIDTH), 0.02),
        'rwkv_r_k': nrm((L, N_HEADS_GROUP, HEAD_DIM), 0.1),
        'rwkv_gn_g': 1.0 + nrm((L, GROUP_WIDTH), 0.02),
        'rwkv_gn_b': nrm((L, GROUP_WIDTH), 0.02),
        'gla_alpha_up': nrm((L, GLA_GATE_RANK, GLA_QK_WIDTH), GLA_GATE_RANK ** -0.5),
        'gla_alpha_b': nrm((L, GLA_QK_WIDTH), 0.5),
        'gla_norm_g': 1.0 + nrm((L, GROUP_WIDTH), 0.02),
        'mlstm_conv_w': nrm((L, MLSTM_CONV, 2 * GROUP_WIDTH), MLSTM_CONV ** -0.5),
        'mlstm_conv_b': nrm((L, 2 * GROUP_WIDTH), 0.02),
        'mlstm_i_b': nrm((L, N_HEADS_GROUP), 0.1),
        'mlstm_f_b': jnp.linspace(3.0, 6.0, N_HEADS_GROUP, dtype=f32) + nrm((L, N_HEADS_GROUP), 0.1),
        'mlstm_norm_g': 1.0 + nrm((L, GROUP_WIDTH), 0.02),
        'mla_q_norm_g': 1.0 + nrm((L, MLA_Q_RANK), 0.02),
        'mla_w_uq': nrm((L, MLA_Q_RANK, N_HEADS_GROUP * (MLA_NOPE + MLA_ROPE)), MLA_Q_RANK ** -0.5),
        'mla_kv_norm_g': 1.0 + nrm((L, MLA_KV_RANK), 0.02),
        'mla_w_ukv': nrm((L, MLA_KV_RANK, N_HEADS_GROUP * (MLA_NOPE + MLA_V)), MLA_KV_RANK ** -0.5),
        'w_out': nrm((L, D_MIX, D), DN_BETA * D_MIX ** -0.5),
        'ln1_g': 1.0 + nrm((L, D), 0.02),
        'ln1_b': nrm((L, D), 0.02),
        'moe_w_rg': nrm((L, D, N_EXPERT_GROUPS), D ** -0.5),
        'moe_b_rg': nrm((L, N_EXPERT_GROUPS), 0.01),
        'moe_w_re': nrm((L, D, N_EXPERTS), D ** -0.5),
        'moe_b_re': nrm((L, N_EXPERTS), 0.01),
        'moe_w_gate': nrm((L, N_EXPERTS, D, D_EXPERT), D ** -0.5),
        'moe_w_up': nrm((L, N_EXPERTS, D, D_EXPERT), DN_BETA * D ** -0.5),
        'moe_w_down': nrm((L, N_EXPERTS, D_EXPERT, D), DN_BETA * D_EXPERT ** -0.5),
        'ple_w_gate': nrm((L, D, D), D ** -0.5),
        'ple_b_gate': nrm((L, D), 0.02),
        'ple_w': nrm((L, D_PLE, D), DN_BETA * D_PLE ** -0.5),
        'ln2_g': 1.0 + nrm((L, D), 0.02),
        'ln2_b': nrm((L, D), 0.02),
    }


def reference(x, p, positions, w_in, rwkv_mu, rwkv_w0, rwkv_w_up, rwkv_a0, rwkv_a_up, rwkv_g_up,
              rwkv_k_k, rwkv_k_a, rwkv_r_k, rwkv_gn_g, rwkv_gn_b, gla_alpha_up, gla_alpha_b, gla_norm_g,
              mlstm_conv_w, mlstm_conv_b, mlstm_i_b, mlstm_f_b, mlstm_norm_g, mla_q_norm_g, mla_w_uq,
              mla_kv_norm_g, mla_w_ukv, w_out, ln1_g, ln1_b, moe_w_rg, moe_b_rg, moe_w_re, moe_b_re,
              moe_w_gate, moe_w_up, moe_w_down, ple_w_gate, ple_b_gate, ple_w, ln2_g, ln2_b):
    cos, sin = rope_cos_sin(positions)
    for i in range(DEPTH):
        u = x @ w_in[i]
        ua, ub, uc, ud = split_last(u, (D_IN_A, D_IN_B, D_IN_C, D_IN_D))
        ya = rwkv7_group(ua, rwkv_mu[i], rwkv_w0[i], rwkv_w_up[i], rwkv_a0[i], rwkv_a_up[i], rwkv_g_up[i],
                         rwkv_k_k[i], rwkv_k_a[i], rwkv_r_k[i], rwkv_gn_g[i], rwkv_gn_b[i])
        yb = gla_group(ub, gla_alpha_up[i], gla_alpha_b[i], gla_norm_g[i])
        yc = mlstm_group(uc, mlstm_conv_w[i], mlstm_conv_b[i], mlstm_i_b[i], mlstm_f_b[i], mlstm_norm_g[i])
        yd = mla_group(ud, cos, sin, mla_q_norm_g[i], mla_w_uq[i], mla_kv_norm_g[i], mla_w_ukv[i])
        mix = jnp.concatenate([ya, yb, yc, yd], axis=-1) @ w_out[i]
        x = layer_norm(DN_ALPHA * x + mix, ln1_g[i], ln1_b[i], LN_EPS)
        ffn = hier_moe(x, moe_w_rg[i], moe_b_rg[i], moe_w_re[i], moe_b_re[i],
                       moe_w_gate[i], moe_w_up[i], moe_w_down[i])
        ple = jax.nn.sigmoid(x @ ple_w_gate[i] + ple_b_gate[i]) * (p[i] @ ple_w[i])
        x = layer_norm(DN_ALPHA * x + ffn + ple, ln2_g[i], ln2_b[i], LN_EPS)
    return x
```

```python
import functools

import jax
import jax.numpy as jnp
import numpy as np
from jax import lax
from jax.experimental import pallas as pl
from jax.experimental.pallas import tpu as pltpu

F32 = jnp.float32
BF16 = jnp.bfloat16

D_MODEL = 1024
DEPTH = 4
HEAD_DIM = 64
N_HEADS = 4
GROUP_WIDTH = N_HEADS * HEAD_DIM
D_PLE = 256

RWKV_DECAY_RANK = 32
RWKV_ICL_RANK = 32
RWKV_GATE_RANK = 64
RWKV_GN_EPS = 64e-5
D_IN_A = 3 * GROUP_WIDTH + RWKV_DECAY_RANK + RWKV_ICL_RANK + RWKV_GATE_RANK

GLA_KEY_DIM = 32
GLA_QK_WIDTH = N_HEADS * GLA_KEY_DIM
GLA_GATE_RANK = 16
GLA_TAU = 16.0
D_IN_B = 2 * GLA_QK_WIDTH + 2 * GROUP_WIDTH + GLA_GATE_RANK

MLSTM_CONV = 4
D_IN_C = 4 * GROUP_WIDTH + 2 * N_HEADS

MLA_Q_RANK = 256
MLA_KV_RANK = 128
MLA_NOPE = 64
MLA_ROPE = 32
MLA_QK = MLA_NOPE + MLA_ROPE
D_IN_D = MLA_Q_RANK + MLA_KV_RANK + MLA_ROPE
ROPE_THETA = 10000.0

N_EXPERT_GROUPS = 4
EXPERTS_PER_GROUP = 8
N_EXPERTS = N_EXPERT_GROUPS * EXPERTS_PER_GROUP
TOP_K = 2
D_EXPERT = 512
MOE_BLOCK = 256

DN_ALPHA = (2.0 * DEPTH) ** 0.25
LN_EPS = 1e-5
NORM_EPS = 1e-6

CHUNK = 64
MISC = 128
MISC_GLA_AD = 0
MISC_IG = 16
MISC_FG = 20
MISC_KR = 32
MISC_KR_ROT = 64
VMEM_LIMIT_BYTES = 56 * 1024 * 1024


def _cparams(n_axes):
    return pltpu.CompilerParams(dimension_semantics=("arbitrary",) * n_axes,
                                vmem_limit_bytes=VMEM_LIMIT_BYTES)


def _dot(a, b):
    return jnp.dot(a.astype(BF16), b.astype(BF16), preferred_element_type=F32)


def _dot_nt(a, b):
    return lax.dot_general(a.astype(BF16), b.astype(BF16), (((1,), (1,)), ((), ())),
                           preferred_element_type=F32)


def _dot_tn(a, b):
    return lax.dot_general(a.astype(BF16), b.astype(BF16), (((0,), (0,)), ((), ())),
                           preferred_element_type=F32)


def _split(x):
    hi = x.astype(BF16)
    lo = (x - hi.astype(F32)).astype(BF16)
    return hi, lo


def _dot_sel(x, sel):
    hi, lo = _split(x)
    return (jnp.dot(hi, sel, preferred_element_type=F32) + jnp.dot(lo, sel, preferred_element_type=F32))


def _sel_dot(sel, x):
    hi, lo = _split(x)
    return (jnp.dot(sel, hi, preferred_element_type=F32) + jnp.dot(sel, lo, preferred_element_type=F32))


def _dot_tn_sel(x, sel):
    hi, lo = _split(x)
    dn = (((0,), (0,)), ((), ()))
    return (lax.dot_general(hi, sel, dn, preferred_element_type=F32)
            + lax.dot_general(lo, sel, dn, preferred_element_type=F32))


def _iota(shape, dim):
    return lax.broadcasted_iota(jnp.int32, shape, dim)


def _shr(x, n):
    return lax.shift_right_logical(x, jnp.int32(n))


def _head_mask(rows, cols, row_shift, col_shift):
    return _shr(_iota((rows, cols), 0), row_shift) == _shr(_iota((rows, cols), 1), col_shift)


def _head_ones(rows, cols, row_shift, col_shift):
    return jnp.where(_head_mask(rows, cols, row_shift, col_shift), 1.0, 0.0).astype(BF16)


def _tril_ones(n):
    return jnp.where(_iota((n, n), 1) <= _iota((n, n), 0), 1.0, 0.0).astype(BF16)


def _bd(x, row_shift=6, col_shift=6):
    t = jnp.concatenate([x] * N_HEADS, axis=0)
    return jnp.where(_head_mask(t.shape[0], t.shape[1], row_shift, col_shift), t, 0.0)


def _lane_j(shape):
    return jnp.bitwise_and(_iota(shape, 1), CHUNK - 1)


def _sigmoid(x):
    return 1.0 / (1.0 + jnp.exp(-x))


def _softplus(x):
    return jnp.maximum(x, 0.0) + jnp.log(1.0 + jnp.exp(-jnp.abs(x)))


def _shift_rows(x, prev8, s, row):
    if s == 0:
        return x
    rp = pltpu.roll(prev8, s, 0)
    rp_t = jnp.concatenate([rp] * (x.shape[0] // 8), axis=0)
    return jnp.where(row < s, rp_t, pltpu.roll(x, s, 0))


def _proj_kernel(x_ref, *refs):
    n = len(refs) // 2
    xb = x_ref[...].astype(BF16)
    for w_ref, o_ref in zip(refs[:n], refs[n:]):
        o_ref[...] = jnp.dot(xb, w_ref[...], preferred_element_type=F32)


def _project(x2d, ws, tm):
    t_tok, d = x2d.shape
    return pl.pallas_call(
        _proj_kernel,
        grid=(t_tok // tm,),
        in_specs=[pl.BlockSpec((tm, d), lambda i: (i, 0))]
        + [pl.BlockSpec(w.shape, lambda i: (0, 0)) for w in ws],
        out_specs=[pl.BlockSpec((tm, w.shape[1]), lambda i: (i, 0)) for w in ws],
        out_shape=[jax.ShapeDtypeStruct((t_tok, w.shape[1]), F32) for w in ws],
        compiler_params=_cparams(1),
        name="proj",
    )(x2d, *ws)


def _rwkv_kernel(u_ref, mu_ref, w0_ref, wup_ref, a0_ref, aup_ref, gup_ref, kk_ref, ka_ref, rk_ref,
                 gng_ref, gnb_ref, o_ref,
                 st_ref, prev_ref, r_s, lw_s, k_s, v_s, a_s, b_s, g_s, y_s):
    ct = u_ref.shape[1]
    gw = GROUP_WIDTH

    @pl.when(pl.program_id(1) == 0)
    def _():
        st_ref[...] = jnp.zeros_like(st_ref)
        prev_ref[...] = jnp.zeros_like(prev_ref)

    u = u_ref[0]
    row = _iota(u.shape, 0)
    prev = jnp.where(row == 0, prev_ref[0:1, :], pltpu.roll(u, 1, 0))
    prev_ref[0:1, :] = u_ref[0, ct - 1:ct, :]
    us = u + (prev - u) * mu_ref[...]
    r = us[:, 0:gw]
    k = us[:, gw:2 * gw]
    v = us[:, 2 * gw:3 * gw]
    misc = us[:, 3 * gw:]
    lnl = -_softplus(-(w0_ref[...] + _dot(jnp.tanh(misc), wup_ref[...]))) - 0.5
    a = _sigmoid(a0_ref[...] + _dot(misc, aup_ref[...]))
    ones_bd = _head_ones(gw, gw, 6, 6)
    kkp = k * kk_ref[...]
    kk = kkp / jnp.maximum(jnp.sqrt(_dot_sel(kkp * kkp, ones_bd)), 1e-12)
    k2 = k * (1.0 + (a - 1.0) * ka_ref[...])
    r_s[...] = r
    lw_s[...] = -jnp.exp(lnl)
    k_s[...] = k2
    v_s[...] = v
    a_s[...] = -kk
    b_s[...] = kk * a
    g_s[...] = _dot(_sigmoid(misc), gup_ref[...])

    def chunk_body(ci, carry):
        sl = pl.ds(pl.multiple_of(ci * CHUNK, CHUNK), CHUNK)
        r_c, lw_c, k_c, v_c, a_c, b_c = r_s[sl, :], lw_s[sl, :], k_s[sl, :], v_s[sl, :], a_s[sl, :], b_s[sl, :]
        st = st_ref[...]
        cs = _sel_dot(_tril_ones(CHUNK), lw_c)
        g_incl = jnp.exp(cs)
        g_prev = jnp.exp(cs - lw_c)
        g_inv = jnp.exp(-cs)
        lhs = jnp.concatenate([a_c * g_prev, r_c * g_incl], axis=0)
        sb = _dot_nt(lhs, _bd(b_c * g_inv))
        sk = _dot_nt(lhs, _bd(k_c * g_inv))
        shape = (CHUNK, gw)
        i_idx, j_idx = _iota(shape, 0), _lane_j(shape)
        strict, incl = j_idx < i_idx, j_idx <= i_idx
        a_ab = jnp.where(strict, sb[:CHUNK], 0.0)
        a_rb = jnp.where(incl, sb[CHUNK:], 0.0)
        a_k = jnp.concatenate([jnp.where(strict, sk[:CHUNK], 0.0), jnp.where(incl, sk[CHUNK:], 0.0)], axis=0)
        p = jnp.where(j_idx == i_idx, 1.0, 0.0) + a_ab
        ap = a_ab
        for _ in range(5):
            ap = _dot(ap, _bd(ap))
            p = p + _dot(p, _bd(ap))
        from_state = _dot(lhs, st)
        from_v = _dot(a_k, _bd(v_c))
        uu = _dot(p, _bd(from_state[:CHUNK] + from_v[:CHUNK]))
        y_s[sl, :] = from_state[CHUNK:] + from_v[CHUNK:] + _dot(a_rb, _bd(uu))
        tail = jnp.exp(cs[CHUNK - 1:CHUNK, :] - cs)
        upd = _dot_tn(jnp.concatenate([b_c * tail, k_c * tail], axis=0), jnp.concatenate([uu, v_c], axis=0))
        g_col = jnp.exp(_dot_tn_sel(lw_c, jnp.ones((CHUNK, gw), BF16)))
        st_ref[...] = g_col * st + jnp.where(_head_mask(gw, gw, 6, 6), upd, 0.0)
        return carry

    lax.fori_loop(0, ct // CHUNK, chunk_body, 0)

    y = y_s[...]
    mean = _dot_sel(y, ones_bd) * (1.0 / HEAD_DIM)
    yc = y - mean
    var = _dot_sel(yc * yc, ones_bd) * (1.0 / HEAD_DIM)
    yn = yc * lax.rsqrt(var + RWKV_GN_EPS) * gng_ref[...] + gnb_ref[...]
    bonus = _dot_sel(r_s[...] * k_s[...] * rk_ref[...], ones_bd) * v_s[...]
    o_ref[0] = ((yn + bonus) * g_s[...]).astype(o_ref.dtype)


def _rwkv(u_a, prm, ct):
    bsz, s, _ = u_a.shape
    gw = GROUP_WIDTH
    vec = lambda n: pl.BlockSpec((1, n), lambda b, c: (0, 0))
    mat = lambda m, n: pl.BlockSpec((m, n), lambda b, c: (0, 0))
    tile = lambda n: pltpu.VMEM((ct, n), F32)
    return pl.pallas_call(
        _rwkv_kernel,
        grid=(bsz, s // ct),
        in_specs=[pl.BlockSpec((1, ct, D_IN_A), lambda b, c: (b, c, 0)), vec(D_IN_A), vec(gw), mat(MISC, gw),
                  vec(gw), mat(MISC, gw), mat(MISC, gw), vec(gw), vec(gw), vec(gw), vec(gw), vec(gw)],
        out_specs=pl.BlockSpec((1, ct, gw), lambda b, c: (b, c, 0)),
        out_shape=jax.ShapeDtypeStruct((bsz, s, gw), BF16),
        scratch_shapes=[pltpu.VMEM((gw, gw), F32), pltpu.VMEM((8, D_IN_A), F32)] + [tile(gw)] * 8,
        compiler_params=_cparams(2),
        name="rwkv7",
    )(u_a, *prm)


def _gla_kernel(u_ref, m_ref, aup_ref, ab_ref, ng_ref, o_ref,
                st_ref, q_s, k_s, b_s, v_s, oi_s, ox_s):
    ct = u_ref.shape[1]
    qw, gw = GLA_QK_WIDTH, GROUP_WIDTH

    @pl.when(pl.program_id(1) == 0)
    def _():
        st_ref[...] = jnp.zeros_like(st_ref)

    u = u_ref[0]
    q_s[...] = u[:, 0:qw] * (GLA_KEY_DIM ** -0.5)
    k_s[...] = u[:, qw:2 * qw]
    v_s[...] = u[:, 2 * qw:2 * qw + gw]
    gate = u[:, 2 * qw + gw:]
    log_a = -_softplus(-(_dot(m_ref[0], aup_ref[...]) + ab_ref[...])) * (1.0 / GLA_TAU)
    head_mask = _head_mask(qw, gw, 5, 6)

    b_s[...] = log_a

    def chunk_body(ci, carry):
        sl = pl.ds(pl.multiple_of(ci * CHUNK, CHUNK), CHUNK)
        la_c = b_s[sl, :]
        q_c, k_c, v_c = q_s[sl, :], k_s[sl, :], v_s[sl, :]
        st = st_ref[...]
        b = _sel_dot(_tril_ones(CHUNK), la_c)
        b_s[sl, :] = b
        ox_s[sl, :] = _dot(q_c * jnp.exp(b), st)
        tail = jnp.exp(b[CHUNK - 1:CHUNK, :] - b)
        g_col = jnp.exp(_dot_tn_sel(la_c, jnp.ones((CHUNK, gw), BF16)))
        st_ref[...] = g_col * st + jnp.where(head_mask, _dot_tn(k_c * tail, v_c), 0.0)
        return carry

    lax.fori_loop(0, ct // CHUNK, chunk_body, 0)

    seg_bcast = jnp.where(head_mask, 1.0, 0.0).astype(BF16)

    def row_body(i, carry):
        c0 = pl.multiple_of(lax.shift_left(_shr(i, 6), jnp.int32(6)), CHUNK)
        sl = pl.ds(c0, CHUNK)
        k_c, b_c, v_c = k_s[sl, :], b_s[sl, :], v_s[sl, :]
        q_i, b_i = q_s[pl.ds(i, 1), :], b_s[pl.ds(i, 1), :]
        valid = (_iota((CHUNK, qw), 0) + c0) <= i
        p = jnp.where(valid, k_c * q_i * jnp.exp(jnp.minimum(b_i - b_c, 0.0)), 0.0)
        sx = jnp.dot(p.astype(BF16), seg_bcast, preferred_element_type=F32)
        oi_s[pl.ds(i, 1), :] = jnp.sum(sx * v_c, axis=0, keepdims=True)
        return carry

    lax.fori_loop(0, ct, row_body, 0, unroll=8)

    o = oi_s[...] + ox_s[...]
    ms = _dot_sel(o * o, _head_ones(gw, gw, 6, 6)) * (1.0 / HEAD_DIM)
    on = o * lax.rsqrt(ms + NORM_EPS) * ng_ref[...]
    o_ref[0] = (on * (gate * _sigmoid(gate))).astype(o_ref.dtype)


def _gla(u_b, misc, prm, ct):
    bsz, s, wb = u_b.shape
    qw, gw = GLA_QK_WIDTH, GROUP_WIDTH
    return pl.pallas_call(
        _gla_kernel,
        grid=(bsz, s // ct),
        in_specs=[pl.BlockSpec((1, ct, wb), lambda b, c: (b, c, 0)),
                  pl.BlockSpec((1, ct, MISC), lambda b, c: (b, c, 0)),
                  pl.BlockSpec((MISC, qw), lambda b, c: (0, 0)),
                  pl.BlockSpec((1, qw), lambda b, c: (0, 0)),
                  pl.BlockSpec((1, gw), lambda b, c: (0, 0))],
        out_specs=pl.BlockSpec((1, ct, gw), lambda b, c: (b, c, 0)),
        out_shape=jax.ShapeDtypeStruct((bsz, s, gw), BF16),
        scratch_shapes=[pltpu.VMEM((qw, gw), F32), pltpu.VMEM((ct, qw), F32), pltpu.VMEM((ct, qw), F32),
                        pltpu.VMEM((ct, qw), F32), pltpu.VMEM((ct, gw), F32), pltpu.VMEM((ct, gw), F32),
                        pltpu.VMEM((ct, gw), F32)],
        compiler_params=_cparams(2),
        name="gla",
    )(u_b, misc, *prm)


def _mlstm_kernel(u_ref, m_ref, cw_ref, cb_ref, eig_ref, efg_ref, ib_ref, fb_ref, ng_ref, o_ref,
                  mem_ref, nb_ref, mst_ref, prev_ref, q_s, k_s, v_s, li_s, lf_s, h_s):
    ct = u_ref.shape[1]
    gw = GROUP_WIDTH

    @pl.when(pl.program_id(1) == 0)
    def _():
        mem_ref[...] = jnp.zeros_like(mem_ref)
        nb_ref[...] = jnp.zeros_like(nb_ref)
        mst_ref[...] = jnp.zeros_like(mst_ref)
        prev_ref[...] = jnp.zeros_like(prev_ref)

    u = u_ref[0]
    qk = u[:, 0:2 * gw]
    row = _iota(qk.shape, 0)
    prev8 = prev_ref[...]
    conv = cb_ref[...] + jnp.zeros_like(qk)
    for j in range(MLSTM_CONV):
        conv = conv + cw_ref[j:j + 1, :] * _shift_rows(qk, prev8, MLSTM_CONV - 1 - j, row)
    prev_ref[...] = u_ref[0, ct - 8:ct, 0:2 * gw]
    qk_act = conv * _sigmoid(conv)
    q_s[...] = qk_act[:, 0:gw]
    k_s[...] = qk_act[:, gw:] * (HEAD_DIM ** -0.5)
    v_s[...] = u[:, 2 * gw:3 * gw]
    o_gate = u[:, 3 * gw:]
    misc = m_ref[0]
    li_s[...] = _dot_sel(misc, eig_ref[...]) + ib_ref[...]
    lf_s[...] = -_softplus(-(_dot_sel(misc, efg_ref[...]) + fb_ref[...]))
    neg_inf = -jnp.inf

    def chunk_body(ci, carry):
        sl = pl.ds(pl.multiple_of(ci * CHUNK, CHUNK), CHUNK)
        q_c, k_c, v_c, li_c, lf_c = q_s[sl, :], k_s[sl, :], v_s[sl, :], li_s[sl, :], lf_s[sl, :]
        mem, nb, m_e = mem_ref[...], nb_ref[...], mst_ref[0:1, :]
        shape = (CHUNK, gw)
        i_idx, j_idx = _iota(shape, 0), _lane_j(shape)
        incl = j_idx <= i_idx
        head_mask = _head_mask(gw, gw, 6, 6)
        fc = _sel_dot(_tril_ones(CHUNK), lf_c)
        g_row = li_c - fc
        ones_avg = jnp.full(shape, 1.0 / HEAD_DIM, BF16)
        g_hi, g_lo = _split(_bd(g_row))
        dn = (((1,), (1,)), ((), ()))
        g_t = (lax.dot_general(ones_avg, g_hi, dn, preferred_element_type=F32)
               + lax.dot_general(ones_avg, g_lo, dn, preferred_element_type=F32))
        log_w = jnp.where(incl, fc + g_t, neg_inf)
        cm = g_row
        for sh in (1, 2, 4, 8, 16, 32):
            cm = jnp.maximum(cm, jnp.where(i_idx < sh, neg_inf, pltpu.roll(cm, sh, 0)))
        log_carry = fc + m_e
        m_row = jnp.maximum(fc + cm, log_carry)
        sc = _dot_nt(q_c, _bd(k_c)) * jnp.exp(log_w - m_row)
        w_c = jnp.exp(log_carry - m_row)
        scb = sc.astype(BF16)
        num = jnp.dot(scb, _bd(v_c).astype(BF16), preferred_element_type=F32) + w_c * _dot(q_c, mem)
        den = (jnp.dot(scb, _head_ones(gw, gw, 6, 6), preferred_element_type=F32) + w_c * _dot(q_c, nb))
        h_s[sl, :] = num / jnp.maximum(jnp.abs(den), jnp.exp(-m_row))
        f_end = fc[CHUNK - 1:CHUNK, :]
        log_kv = f_end - fc + li_c
        m_new = jnp.maximum(f_end + m_e, jnp.max(log_kv, axis=0, keepdims=True))
        kwk = jnp.exp(log_kv - m_new) * k_c
        cd = jnp.exp(f_end + m_e - m_new)
        cd_rows = jnp.where(_iota((16, gw), 0) == 0, cd, 0.0)
        cd_col = _dot_tn_sel(cd_rows, jnp.ones((16, gw), BF16))
        upd = _dot_tn(kwk, jnp.concatenate([v_c, jnp.ones(shape, F32)], axis=1))
        mem_ref[...] = cd_col * mem + jnp.where(head_mask, upd[:, :gw], 0.0)
        nb_ref[...] = cd_col * nb + jnp.where(head_mask, upd[:, gw:], 0.0)
        mst_ref[0:1, :] = m_new
        return carry

    lax.fori_loop(0, ct // CHUNK, chunk_body, 0)

    h = h_s[...]
    ones_bd = _head_ones(gw, gw, 6, 6)
    mean = _dot_sel(h, ones_bd) * (1.0 / HEAD_DIM)
    hc = h - mean
    var = _dot_sel(hc * hc, ones_bd) * (1.0 / HEAD_DIM)
    hn = hc * lax.rsqrt(var + LN_EPS) * ng_ref[...]
    o_ref[0] = (hn * _sigmoid(o_gate)).astype(o_ref.dtype)


def _mlstm(u_c, misc, prm, ct):
    bsz, s, wc = u_c.shape
    gw = GROUP_WIDTH
    vec = lambda n: pl.BlockSpec((1, n), lambda b, c: (0, 0))
    mat = lambda m, n: pl.BlockSpec((m, n), lambda b, c: (0, 0))
    tile = pltpu.VMEM((ct, gw), F32)
    return pl.pallas_call(
        _mlstm_kernel,
        grid=(bsz, s // ct),
        in_specs=[pl.BlockSpec((1, ct, wc), lambda b, c: (b, c, 0)),
                  pl.BlockSpec((1, ct, MISC), lambda b, c: (b, c, 0)),
                  mat(MLSTM_CONV, 2 * gw), vec(2 * gw), mat(MISC, gw), mat(MISC, gw), vec(gw), vec(gw), vec(gw)],
        out_specs=pl.BlockSpec((1, ct, gw), lambda b, c: (b, c, 0)),
        out_shape=jax.ShapeDtypeStruct((bsz, s, gw), BF16),
        scratch_shapes=[pltpu.VMEM((gw, gw), F32), pltpu.VMEM((gw, gw), F32), pltpu.VMEM((8, gw), F32),
                        pltpu.VMEM((8, 2 * gw), F32)] + [tile] * 6,
        compiler_params=_cparams(2),
        name="mlstm",
    )(u_c, misc, *prm)


def _mla_prep_kernel(u_ref, m_ref, cq_ref, sq_ref, csk_ref, qg_ref, kvg_ref, wq_ref, wqr_ref, wk_ref, wv_ref,
                     selk_ref, q_ref, k_ref, v_ref):
    u = u_ref[0]
    cq = u[:, 0:MLA_Q_RANK]
    ckv = u[:, MLA_Q_RANK:]
    cqn = cq * lax.rsqrt(jnp.mean(cq * cq, axis=-1, keepdims=True) + NORM_EPS) * qg_ref[...]
    ckvn = ckv * lax.rsqrt(jnp.mean(ckv * ckv, axis=-1, keepdims=True) + NORM_EPS) * kvg_ref[...]
    cqb, ckvb = cqn.astype(BF16), ckvn.astype(BF16)
    k_rope = _dot(m_ref[0] * csk_ref[0], selk_ref[...])
    cos_q, sin_q = cq_ref[0], sq_ref[0]
    scale = MLA_QK ** -0.5
    for h in range(N_HEADS):
        qh = (jnp.dot(cqb, wq_ref[h], preferred_element_type=F32) * cos_q
              + jnp.dot(cqb, wqr_ref[h], preferred_element_type=F32) * sin_q) * scale
        q_ref[0, h] = qh.astype(BF16)
        k_ref[0, h] = (jnp.dot(ckvb, wk_ref[h], preferred_element_type=F32) + k_rope).astype(BF16)
        v_ref[0, h] = jnp.dot(ckvb, wv_ref[h], preferred_element_type=F32).astype(BF16)


def _mla_prep(u_d, misc, tabs, prm, tm):
    bsz, s, wd = u_d.shape
    full = lambda a: pl.BlockSpec(a.shape, lambda b, c: (0,) * a.ndim)
    tok = lambda n: pl.BlockSpec((1, tm, n), lambda b, c: (b, c, 0))
    head = lambda n: pl.BlockSpec((1, N_HEADS, tm, n), lambda b, c: (b, 0, c, 0))
    return pl.pallas_call(
        _mla_prep_kernel,
        grid=(bsz, s // tm),
        in_specs=[tok(wd), tok(MISC), tok(MLA_QK), tok(MLA_QK), tok(MISC)] + [full(a) for a in prm],
        out_specs=[head(MLA_QK), head(MLA_QK), head(HEAD_DIM)],
        out_shape=[jax.ShapeDtypeStruct((bsz, N_HEADS, s, MLA_QK), BF16),
                   jax.ShapeDtypeStruct((bsz, N_HEADS, s, MLA_QK), BF16),
                   jax.ShapeDtypeStruct((bsz, N_HEADS, s, HEAD_DIM), BF16)],
        compiler_params=_cparams(2),
        name="mla_prep",
    )(u_d, misc, *tabs, *prm)


def _attn_kernel(q_ref, k_ref, v_ref, o_ref):
    tq = q_ref.shape[2]
    qi = pl.program_id(2)
    q = q_ref[0, 0]

    def body(j, carry):
        m, l, acc = carry
        sl = pl.ds(pl.multiple_of(j * tq, tq), tq)
        kb, vb = k_ref[0, 0, sl, :], v_ref[0, 0, sl, :]
        s = lax.dot_general(q, kb, (((1,), (1,)), ((), ())), preferred_element_type=F32)
        rows = qi * tq + _iota((tq, tq), 0)
        cols = j * tq + _iota((tq, tq), 1)
        s = jnp.where(cols <= rows, s, -jnp.inf)
        m_new = jnp.maximum(m, jnp.max(s, axis=-1, keepdims=True))
        p = jnp.exp(s - m_new)
        alpha = jnp.exp(m - m_new)
        l = alpha * l + jnp.sum(p, axis=-1, keepdims=True)
        acc = alpha * acc + jnp.dot(p.astype(BF16), vb, preferred_element_type=F32)
        return m_new, l, acc

    init = (jnp.full((tq, 1), -jnp.inf, F32), jnp.zeros((tq, 1), F32), jnp.zeros((tq, HEAD_DIM), F32))
    _, l, acc = lax.fori_loop(0, qi + 1, body, init)
    o_ref[0, 0] = (acc / l).astype(o_ref.dtype)


def _attention(q, k, v, tq):
    bsz, nh, s, _ = q.shape
    return pl.pallas_call(
        _attn_kernel,
        grid=(bsz, nh, s // tq),
        in_specs=[pl.BlockSpec((1, 1, tq, MLA_QK), lambda b, h, i: (b, h, i, 0)),
                  pl.BlockSpec((1, 1, s, MLA_QK), lambda b, h, i: (b, h, 0, 0)),
                  pl.BlockSpec((1, 1, s, HEAD_DIM), lambda b, h, i: (b, h, 0, 0))],
        out_specs=pl.BlockSpec((1, 1, tq, HEAD_DIM), lambda b, h, i: (b, h, i, 0)),
        out_shape=jax.ShapeDtypeStruct((bsz, nh, s, HEAD_DIM), BF16),
        compiler_params=_cparams(3),
        name="mla_attn",
    )(q, k, v)


def _layer_norm(z, g, b):
    zc = z - jnp.mean(z, axis=-1, keepdims=True)
    return zc * lax.rsqrt(jnp.mean(zc * zc, axis=-1, keepdims=True) + LN_EPS) * g + b


def _outproj_kernel(x_ref, ya_ref, yb_ref, yc_ref, yd_ref, wo_ref, wod_ref, g_ref, b_ref, o_ref):
    gw = GROUP_WIDTH
    mix = jnp.dot(ya_ref[0], wo_ref[0:gw, :], preferred_element_type=F32)
    mix = mix + jnp.dot(yb_ref[0], wo_ref[gw:2 * gw, :], preferred_element_type=F32)
    mix = mix + jnp.dot(yc_ref[0], wo_ref[2 * gw:3 * gw, :], preferred_element_type=F32)
    for h in range(N_HEADS):
        mix = mix + jnp.dot(yd_ref[0, h], wod_ref[h], preferred_element_type=F32)
    o_ref[0] = _layer_norm(DN_ALPHA * x_ref[0] + mix, g_ref[...], b_ref[...])


def _outproj(x, ya, yb, yc, yd, wo, wod, g, b, tm):
    bsz, s, d = x.shape
    gw = GROUP_WIDTH
    tok = lambda n: pl.BlockSpec((1, tm, n), lambda bb, c: (bb, c, 0))
    full = lambda a: pl.BlockSpec(a.shape, lambda bb, c: (0,) * a.ndim)
    return pl.pallas_call(
        _outproj_kernel,
        grid=(bsz, s // tm),
        in_specs=[tok(d), tok(gw), tok(gw), tok(gw),
                  pl.BlockSpec((1, N_HEADS, tm, HEAD_DIM), lambda bb, c: (bb, 0, c, 0)),
                  full(wo), full(wod), full(g), full(b)],
        out_specs=tok(d),
        out_shape=jax.ShapeDtypeStruct((bsz, s, d), F32),
        compiler_params=_cparams(2),
        name="outproj_ln1",
    )(x, ya, yb, yc, yd, wo, wod, g, b)


def _router_kernel(x_ref, w_ref, b_ref, idx_ref, gate_ref):
    x = x_ref[...]
    xh, xl = _split(x)
    wh, wl = _split(w_ref[...])
    dn = (((1,), (1,)), ((), ()))
    logits = (lax.dot_general(wh, xh, dn, preferred_element_type=F32)
              + lax.dot_general(wh, xl, dn, preferred_element_type=F32)
              + lax.dot_general(wl, xh, dn, preferred_element_type=F32)) + b_ref[...]
    tm = x.shape[0]
    ng, ne = N_EXPERT_GROUPS, EXPERTS_PER_GROUP
    gl = [logits[i:i + 1, :] for i in range(ng)]
    gmax = functools.reduce(jnp.maximum, gl)
    gsum = functools.reduce(jnp.add, [jnp.exp(g - gmax) for g in gl])
    group_p = 1.0 / gsum
    gidx = jnp.full((1, tm), ng - 1, jnp.int32)
    for i in range(ng - 2, -1, -1):
        gidx = jnp.where(gl[i] == gmax, i, gidx)
    el = logits[8 + (ng - 1) * ne:8 + ng * ne, :]
    for i in range(ng - 2, -1, -1):
        el = jnp.where(gidx == i, logits[8 + i * ne:8 + (i + 1) * ne, :], el)
    rows = _iota((ne, tm), 0)
    m1 = jnp.max(el, axis=0, keepdims=True)
    i1 = jnp.min(jnp.where(el == m1, rows, ne), axis=0, keepdims=True)
    el2 = jnp.where(rows == i1, -jnp.inf, el)
    m2 = jnp.max(el2, axis=0, keepdims=True)
    i2 = jnp.min(jnp.where(el2 == m2, rows, ne), axis=0, keepdims=True)
    z = jnp.sum(jnp.exp(el - m1), axis=0, keepdims=True)
    p1 = 1.0 / z
    p2 = jnp.exp(m2 - m1) / z
    base = gidx * ne
    idx_ref[...] = jnp.concatenate([base + i1, base + i2], axis=0)
    gate_ref[...] = jnp.concatenate([group_p * p1 / (p1 + p2), group_p * p2 / (p1 + p2)], axis=0)


def _router(x2d, w_t, b_col, tm):
    t_tok, d = x2d.shape
    return pl.pallas_call(
        _router_kernel,
        grid=(t_tok // tm,),
        in_specs=[pl.BlockSpec((tm, d), lambda i: (i, 0)), pl.BlockSpec(w_t.shape, lambda i: (0, 0)),
                  pl.BlockSpec(b_col.shape, lambda i: (0, 0))],
        out_specs=[pl.BlockSpec((TOP_K, tm), lambda i: (0, i)), pl.BlockSpec((TOP_K, tm), lambda i: (0, i))],
        out_shape=[jax.ShapeDtypeStruct((TOP_K, t_tok), jnp.int32), jax.ShapeDtypeStruct((TOP_K, t_tok), F32)],
        compiler_params=_cparams(1),
        name="router",
    )(x2d, w_t, b_col)


def _moe_kernel(be_ref, x_ref, gate_ref, wg_ref, wu_ref, wd_ref, o_ref):
    xb = x_ref[...]
    hg = jnp.dot(xb, wg_ref[0], preferred_element_type=F32)
    hu = jnp.dot(xb, wu_ref[0], preferred_element_type=F32)
    hid = (hg * _sigmoid(hg)) * hu
    o_ref[...] = jnp.dot(hid.astype(BF16), wd_ref[0], preferred_element_type=F32) * gate_ref[...]


def _moe(x_rows, block_expert, row_gate, wg, wu, wd):
    n_rows, d = x_rows.shape
    n_blocks = n_rows // MOE_BLOCK
    grid_spec = pltpu.PrefetchScalarGridSpec(
        num_scalar_prefetch=1,
        grid=(n_blocks,),
        in_specs=[pl.BlockSpec((MOE_BLOCK, d), lambda i, be: (i, 0)),
                  pl.BlockSpec((MOE_BLOCK, 1), lambda i, be: (i, 0)),
                  pl.BlockSpec((1, d, D_EXPERT), lambda i, be: (be[i], 0, 0)),
                  pl.BlockSpec((1, d, D_EXPERT), lambda i, be: (be[i], 0, 0)),
                  pl.BlockSpec((1, D_EXPERT, d), lambda i, be: (be[i], 0, 0))],
        out_specs=pl.BlockSpec((MOE_BLOCK, d), lambda i, be: (i, 0)),
    )
    return pl.pallas_call(
        _moe_kernel,
        grid_spec=grid_spec,
        out_shape=jax.ShapeDtypeStruct((n_rows, d), F32),
        compiler_params=_cparams(1),
        name="moe_experts",
    )(block_expert, x_rows, row_gate.reshape(n_rows, 1), wg, wu, wd)


def _dispatch(idx, gate, t_tok):
    n_asg = t_tok * TOP_K
    flat_e = idx.T.reshape(n_asg)
    flat_g = gate.T.reshape(n_asg)
    order = jnp.argsort(flat_e)
    sorted_e = flat_e[order]
    counts = jnp.bincount(flat_e, length=N_EXPERTS)
    starts = jnp.cumsum(counts) - counts
    padded = (counts + MOE_BLOCK - 1) // MOE_BLOCK * MOE_BLOCK
    pad_ends = jnp.cumsum(padded)
    dest = pad_ends[sorted_e] - padded[sorted_e] + jnp.arange(n_asg) - starts[sorted_e]
    n_rows = -(-(n_asg + N_EXPERTS * (MOE_BLOCK - 1)) // MOE_BLOCK) * MOE_BLOCK
    n_blocks = n_rows // MOE_BLOCK
    row_asg = jnp.full((n_rows,), n_asg, jnp.int32).at[dest].set(order.astype(jnp.int32))
    row_tok = jnp.where(row_asg < n_asg, row_asg // TOP_K, 0).astype(jnp.int32)
    row_gate = jnp.zeros((n_rows,), F32).at[dest].set(flat_g[order])
    block_expert = jnp.minimum(jnp.searchsorted(pad_ends, jnp.arange(n_blocks) * MOE_BLOCK, side='right'),
                               N_EXPERTS - 1).astype(jnp.int32)
    pos = jnp.zeros((n_asg,), jnp.int32).at[order].set(dest.astype(jnp.int32))
    return block_expert, row_tok, pos, row_gate


def _final_kernel(x_ref, p_ref, f_ref, wg_ref, bg_ref, wp_ref, g_ref, b_ref, o_ref):
    x = x_ref[...]
    d = x.shape[1]
    gate = _sigmoid(_dot(x, wg_ref[...]) + bg_ref[...])
    ple = gate * _dot(p_ref[...], wp_ref[...])
    ffn = f_ref[:, 0:d] + f_ref[:, d:]
    o_ref[...] = _layer_norm(DN_ALPHA * x + ffn + ple, g_ref[...], b_ref[...])


def _final(x2d, p2d, ffn2, wg, bg, wp, g, b, tm):
    t_tok, d = x2d.shape
    full = lambda a: pl.BlockSpec(a.shape, lambda i: (0,) * a.ndim)
    return pl.pallas_call(
        _final_kernel,
        grid=(t_tok // tm,),
        in_specs=[pl.BlockSpec((tm, d), lambda i: (i, 0)), pl.BlockSpec((tm, p2d.shape[1]), lambda i: (i, 0)),
                  pl.BlockSpec((tm, TOP_K * d), lambda i: (i, 0)), full(wg), full(bg), full(wp), full(g), full(b)],
        out_specs=pl.BlockSpec((tm, d), lambda i: (i, 0)),
        out_shape=jax.ShapeDtypeStruct((t_tok, d), F32),
        compiler_params=_cparams(1),
        name="ple_ln2",
    )(x2d, p2d, ffn2, wg, bg, wp, g, b)


def _pad_rows(w, offset, total=MISC):
    return jnp.zeros((total, w.shape[1]), w.dtype).at[offset:offset + w.shape[0]].set(w)


def _rot_cols(w):
    half = w.shape[1] // 2
    return jnp.concatenate([-w[:, half:], w[:, :half]], axis=1)


def _row(v):
    return v.reshape(1, -1)


def _expand_heads(v):
    return jnp.repeat(v, HEAD_DIM).reshape(1, GROUP_WIDTH)


def _head_select(offset):
    sel = np.zeros((MISC, GROUP_WIDTH), np.float32)
    for h in range(N_HEADS):
        sel[offset + h, h * HEAD_DIM:(h + 1) * HEAD_DIM] = 1.0
    return jnp.asarray(sel, BF16)


def _rope_tables(positions):
    inv_freq = ROPE_THETA ** (-jnp.arange(0, MLA_ROPE, 2, dtype=F32) / MLA_ROPE)
    ang = positions.astype(F32)[..., None] * inv_freq
    cos, sin = jnp.cos(ang), jnp.sin(ang)
    cos2 = jnp.concatenate([cos, cos], axis=-1)
    sin2 = jnp.concatenate([sin, sin], axis=-1)
    lead = cos.shape[:-1]
    cos_q = jnp.concatenate([jnp.ones(lead + (MLA_NOPE,), F32), cos2], axis=-1)
    sin_q = jnp.concatenate([jnp.zeros(lead + (MLA_NOPE,), F32), sin2], axis=-1)
    cs_k = jnp.concatenate([jnp.zeros(lead + (MISC_KR,), F32), cos2, sin2,
                            jnp.zeros(lead + (MISC - MISC_KR_ROT - MLA_ROPE,), F32)], axis=-1)
    return cos_q, sin_q, cs_k


def _rope_key_select():
    sel = np.zeros((MISC, MLA_QK), np.float32)
    for c in range(MLA_ROPE):
        sel[MISC_KR + c, MLA_NOPE + c] = 1.0
        sel[MISC_KR_ROT + c, MLA_NOPE + c] = 1.0
    return jnp.asarray(sel, BF16)


def _split_w_in(w):
    a0, b0 = 0, D_IN_A
    c0, d0 = b0 + D_IN_B, b0 + D_IN_B + D_IN_C
    qw, gw = GLA_QK_WIDTH, GROUP_WIDTH
    w_a = w[:, a0:b0]
    gla_ad = w[:, b0 + 2 * qw + gw:b0 + 2 * qw + gw + GLA_GATE_RANK]
    w_b = jnp.concatenate([w[:, b0:b0 + 2 * qw + gw], w[:, b0 + 2 * qw + gw + GLA_GATE_RANK:c0]], axis=1)
    w_c = w[:, c0:c0 + 4 * gw]
    gates = w[:, c0 + 4 * gw:d0]
    w_d = w[:, d0:d0 + MLA_Q_RANK + MLA_KV_RANK]
    kr = w[:, d0 + MLA_Q_RANK + MLA_KV_RANK:]
    z = lambda n: jnp.zeros((w.shape[0], n), w.dtype)
    w_m = jnp.concatenate([gla_ad, gates, z(MISC_KR - MISC_FG - N_HEADS), kr, _rot_cols(kr),
                           z(MISC - MISC_KR_ROT - MLA_ROPE)], axis=1)
    return [t.astype(BF16) for t in (w_a, w_b, w_c, w_d, w_m)]


def _tiles(s):
    return min(512, s), min(256, s)


def kernel(x, p, positions, w_in, rwkv_mu, rwkv_w0, rwkv_w_up, rwkv_a0, rwkv_a_up, rwkv_g_up, rwkv_k_k, rwkv_k_a, rwkv_r_k, rwkv_gn_g, rwkv_gn_b, gla_alpha_up, gla_alpha_b, gla_norm_g, mlstm_conv_w, mlstm_conv_b, mlstm_i_b, mlstm_f_b, mlstm_norm_g, mla_q_norm_g, mla_w_uq, mla_kv_norm_g, mla_w_ukv, w_out, ln1_g, ln1_b, moe_w_rg, moe_b_rg, moe_w_re, moe_b_re, moe_w_gate, moe_w_up, moe_w_down, ple_w_gate, ple_b_gate, ple_w, ln2_g, ln2_b):
    bsz, s, d = x.shape
    t_tok = bsz * s
    depth = w_in.shape[0]
    ct, tq = _tiles(s)
    tm = min(512, t_tok)
    rope_tabs = _rope_tables(positions)
    sel_ig, sel_fg, sel_k = _head_select(MISC_IG), _head_select(MISC_FG), _rope_key_select()
    for i in range(depth):
        u_a, u_b, u_c, u_d, u_m = _project(x.reshape(t_tok, d), _split_w_in(w_in[i]), tm)
        u_a, u_b, u_c, u_d, u_m = (t.reshape(bsz, s, -1) for t in (u_a, u_b, u_c, u_d, u_m))

        rk = RWKV_DECAY_RANK
        ya = _rwkv(u_a, (_row(rwkv_mu[i]), _row(rwkv_w0[i]), _pad_rows(rwkv_w_up[i], 0).astype(BF16),
                         _row(rwkv_a0[i]), _pad_rows(rwkv_a_up[i], rk).astype(BF16),
                         _pad_rows(rwkv_g_up[i], rk + RWKV_ICL_RANK).astype(BF16),
                         _row(rwkv_k_k[i]), _row(rwkv_k_a[i]), _row(rwkv_r_k[i]),
                         _row(rwkv_gn_g[i]), _row(rwkv_gn_b[i])), ct)
        yb = _gla(u_b, u_m, (_pad_rows(gla_alpha_up[i], MISC_GLA_AD).astype(BF16), _row(gla_alpha_b[i]),
                             _row(gla_norm_g[i])), ct)
        yc = _mlstm(u_c, u_m, (mlstm_conv_w[i], _row(mlstm_conv_b[i]), sel_ig, sel_fg,
                               _expand_heads(mlstm_i_b[i]), _expand_heads(mlstm_f_b[i]),
                               _row(mlstm_norm_g[i])), ct)

        wq = mla_w_uq[i].reshape(MLA_Q_RANK, N_HEADS, MLA_QK).transpose(1, 0, 2)
        wq_rot = jnp.concatenate([jnp.zeros((N_HEADS, MLA_Q_RANK, MLA_NOPE), F32),
                                  jax.vmap(_rot_cols)(wq[:, :, MLA_NOPE:])], axis=2)
        wkv = mla_w_ukv[i].reshape(MLA_KV_RANK, N_HEADS, MLA_NOPE + HEAD_DIM).transpose(1, 0, 2)
        wk = jnp.concatenate([wkv[:, :, :MLA_NOPE], jnp.zeros((N_HEADS, MLA_KV_RANK, MLA_ROPE), F32)], axis=2)
        q, k, v = _mla_prep(u_d, u_m, rope_tabs,
                            (_row(mla_q_norm_g[i]), _row(mla_kv_norm_g[i]), wq.astype(BF16), wq_rot.astype(BF16),
                             wk.astype(BF16), wkv[:, :, MLA_NOPE:].astype(BF16), sel_k), ct)
        yd = _attention(q, k, v, tq)

        wo = w_out[i].astype(BF16)
        x = _outproj(x, ya, yb, yc, yd, wo[:3 * GROUP_WIDTH], wo[3 * GROUP_WIDTH:].reshape(N_HEADS, HEAD_DIM, d),
                     _row(ln1_g[i]), _row(ln1_b[i]), ct)

        x2d = x.reshape(t_tok, d)
        w_route = jnp.concatenate([moe_w_rg[i].T, jnp.zeros((8 - N_EXPERT_GROUPS, d), F32), moe_w_re[i].T], axis=0)
        b_route = jnp.concatenate([moe_b_rg[i], jnp.zeros((8 - N_EXPERT_GROUPS,), F32), moe_b_re[i]]).reshape(-1, 1)
        idx, gate = _router(x2d, w_route, b_route, tm)
        block_expert, row_tok, pos, row_gate = _dispatch(idx, gate, t_tok)
        y_rows = _moe(x2d.astype(BF16)[row_tok], block_expert, row_gate,
                      moe_w_gate[i].astype(BF16), moe_w_up[i].astype(BF16), moe_w_down[i].astype(BF16))
        ffn = y_rows[pos]
        x = _final(x2d, p[i].reshape(t_tok, -1), ffn.reshape(t_tok, TOP_K * d), ple_w_gate[i].astype(BF16),
                   _row(ple_b_gate[i]), ple_w[i].astype(BF16), _row(ln2_g[i]), _row(ln2_b[i]), tm).reshape(bsz, s, d)
    return x
```

```python
import functools

import jax
import jax.numpy as jnp
import numpy as np
from jax import lax
from jax.experimental import pallas as pl
from jax.experimental.pallas import tpu as pltpu

F32 = jnp.float32
BF16 = jnp.bfloat16

D_MODEL = 1024
DEPTH = 4
HEAD_DIM = 64
N_HEADS = 4
GROUP_WIDTH = N_HEADS * HEAD_DIM
D_PLE = 256

RWKV_DECAY_RANK = 32
RWKV_ICL_RANK = 32
RWKV_GATE_RANK = 64
RWKV_GN_EPS = 64e-5
D_IN_A = 3 * GROUP_WIDTH + RWKV_DECAY_RANK + RWKV_ICL_RANK + RWKV_GATE_RANK

GLA_KEY_DIM = 32
GLA_QK_WIDTH = N_HEADS * GLA_KEY_DIM
GLA_GATE_RANK = 16
GLA_TAU = 16.0
D_IN_B = 2 * GLA_QK_WIDTH + 2 * GROUP_WIDTH + GLA_GATE_RANK

MLSTM_CONV = 4
D_IN_C = 4 * GROUP_WIDTH + 2 * N_HEADS

MLA_Q_RANK = 256
MLA_KV_RANK = 128
MLA_NOPE = 64
MLA_ROPE = 32
MLA_QK = MLA_NOPE + MLA_ROPE
D_IN_D = MLA_Q_RANK + MLA_KV_RANK + MLA_ROPE
ROPE_THETA = 10000.0

N_EXPERT_GROUPS = 4
EXPERTS_PER_GROUP = 8
N_EXPERTS = N_EXPERT_GROUPS * EXPERTS_PER_GROUP
TOP_K = 2
D_EXPERT = 512
MOE_BLOCK = 256

DN_ALPHA = (2.0 * DEPTH) ** 0.25
LN_EPS = 1e-5
NORM_EPS = 1e-6

CHUNK = 64
MISC = 128
MISC_GLA_AD = 0
MISC_IG = 16
MISC_FG = 20
MISC_KR = 32
MISC_KR_ROT = 64
VMEM_LIMIT_BYTES = 56 * 1024 * 1024


def _cparams(n_axes):
    return pltpu.CompilerParams(dimension_semantics=("arbitrary",) * n_axes,
                                vmem_limit_bytes=VMEM_LIMIT_BYTES)


def _dot(a, b):
    return jnp.dot(a.astype(BF16), b.astype(BF16), preferred_element_type=F32)


def _dot_nt(a, b):
    return lax.dot_general(a.astype(BF16), b.astype(BF16), (((1,), (1,)), ((), ())),
                           preferred_element_type=F32)


def _dot_tn(a, b):
    return lax.dot_general(a.astype(BF16), b.astype(BF16), (((0,), (0,)), ((), ())),
                           preferred_element_type=F32)


def _split(x):
    hi = x.astype(BF16)
    lo = (x - hi.astype(F32)).astype(BF16)
    return hi, lo


def _dot_sel(x, sel):
    hi, lo = _split(x)
    return (jnp.dot(hi, sel, preferred_element_type=F32) + jnp.dot(lo, sel, preferred_element_type=F32))


def _sel_dot(sel, x):
    hi, lo = _split(x)
    return (jnp.dot(sel, hi, preferred_element_type=F32) + jnp.dot(sel, lo, preferred_element_type=F32))


def _dot_tn_sel(x, sel):
    hi, lo = _split(x)
    dn = (((0,), (0,)), ((), ()))
    return (lax.dot_general(hi, sel, dn, preferred_element_type=F32)
            + lax.dot_general(lo, sel, dn, preferred_element_type=F32))


def _iota(shape, dim):
    return lax.broadcasted_iota(jnp.int32, shape, dim)


def _shr(x, n):
    return lax.shift_right_logical(x, jnp.int32(n))


def _head_mask(rows, cols, row_shift, col_shift):
    return _shr(_iota((rows, cols), 0), row_shift) == _shr(_iota((rows, cols), 1), col_shift)


def _head_ones(rows, cols, row_shift, col_shift):
    return jnp.where(_head_mask(rows, cols, row_shift, col_shift), 1.0, 0.0).astype(BF16)


def _tril_ones(n):
    return jnp.where(_iota((n, n), 1) <= _iota((n, n), 0), 1.0, 0.0).astype(BF16)


def _bd(x, row_shift=6, col_shift=6):
    t = jnp.concatenate([x] * N_HEADS, axis=0)
    return jnp.where(_head_mask(t.shape[0], t.shape[1], row_shift, col_shift), t, 0.0)


def _lane_j(shape):
    return jnp.bitwise_and(_iota(shape, 1), CHUNK - 1)


def _sigmoid(x):
    return 1.0 / (1.0 + jnp.exp(-x))


def _softplus(x):
    return jnp.maximum(x, 0.0) + jnp.log(1.0 + jnp.exp(-jnp.abs(x)))


def _shift_rows(x, prev8, s, row):
    if s == 0:
        return x
    rp = pltpu.roll(prev8, s, 0)
    rp_t = jnp.concatenate([rp] * (x.shape[0] // 8), axis=0)
    return jnp.where(row < s, rp_t, pltpu.roll(x, s, 0))


def _proj_kernel(x_ref, *refs):
    n = len(refs) // 2
    xb = x_ref[...].astype(BF16)
    for w_ref, o_ref in zip(refs[:n], refs[n:]):
        o_ref[...] = jnp.dot(xb, w_ref[...], preferred_element_type=F32)


def _project(x2d, ws, tm):
    t_tok, d = x2d.shape
    return pl.pallas_call(
        _proj_kernel,
        grid=(t_tok // tm,),
        in_specs=[pl.BlockSpec((tm, d), lambda i: (i, 0))]
        + [pl.BlockSpec(w.shape, lambda i: (0, 0)) for w in ws],
        out_specs=[pl.BlockSpec((tm, w.shape[1]), lambda i: (i, 0)) for w in ws],
        out_shape=[jax.ShapeDtypeStruct((t_tok, w.shape[1]), F32) for w in ws],
        compiler_params=_cparams(1),
        name="proj",
    )(x2d, *ws)


def _rwkv_kernel(u_ref, mu_ref, w0_ref, wup_ref, a0_ref, aup_ref, gup_ref, kk_ref, ka_ref, rk_ref,
                 gng_ref, gnb_ref, o_ref,
                 st_ref, prev_ref, r_s, lw_s, k_s, v_s, a_s, b_s, g_s, y_s):
    ct = u_ref.shape[1]
    gw = GROUP_WIDTH

    @pl.when(pl.program_id(1) == 0)
    def _():
        st_ref[...] = jnp.zeros_like(st_ref)
        prev_ref[...] = jnp.zeros_like(prev_ref)

    u = u_ref[0]
    row = _iota(u.shape, 0)
    prev = jnp.where(row == 0, prev_ref[0:1, :], pltpu.roll(u, 1, 0))
    prev_ref[0:1, :] = u_ref[0, ct - 1:ct, :]
    us = u + (prev - u) * mu_ref[...]
    r = us[:, 0:gw]
    k = us[:, gw:2 * gw]
    v = us[:, 2 * gw:3 * gw]
    misc = us[:, 3 * gw:]
    lnl = -_softplus(-(w0_ref[...] + _dot(jnp.tanh(misc), wup_ref[...]))) - 0.5
    a = _sigmoid(a0_ref[...] + _dot(misc, aup_ref[...]))
    ones_bd = _head_ones(gw, gw, 6, 6)
    kkp = k * kk_ref[...]
    kk = kkp / jnp.maximum(jnp.sqrt(_dot_sel(kkp * kkp, ones_bd)), 1e-12)
    k2 = k * (1.0 + (a - 1.0) * ka_ref[...])
    r_s[...] = r
    lw_s[...] = -jnp.exp(lnl)
    k_s[...] = k2
    v_s[...] = v
    a_s[...] = -kk
    b_s[...] = kk * a
    g_s[...] = _dot(_sigmoid(misc), gup_ref[...])

    def chunk_body(ci, carry):
        sl = pl.ds(pl.multiple_of(ci * CHUNK, CHUNK), CHUNK)
        r_c, lw_c, k_c, v_c, a_c, b_c = r_s[sl, :], lw_s[sl, :], k_s[sl, :], v_s[sl, :], a_s[sl, :], b_s[sl, :]
        st = st_ref[...]
        cs = _sel_dot(_tril_ones(CHUNK), lw_c)
        g_incl = jnp.exp(cs)
        g_prev = jnp.exp(cs - lw_c)
        g_inv = jnp.exp(-cs)
        lhs = jnp.concatenate([a_c * g_prev, r_c * g_incl], axis=0)
        sb = _dot_nt(lhs, _bd(b_c * g_inv))
        sk = _dot_nt(lhs, _bd(k_c * g_inv))
        shape = (CHUNK, gw)
        i_idx, j_idx = _iota(shape, 0), _lane_j(shape)
        strict, incl = j_idx < i_idx, j_idx <= i_idx
        a_ab = jnp.where(strict, sb[:CHUNK], 0.0)
        a_rb = jnp.where(incl, sb[CHUNK:], 0.0)
        a_k = jnp.concatenate([jnp.where(strict, sk[:CHUNK], 0.0), jnp.where(incl, sk[CHUNK:], 0.0)], axis=0)
        p = jnp.where(j_idx == i_idx, 1.0, 0.0) + a_ab
        ap = a_ab
        for _ in range(5):
            ap = _dot(ap, _bd(ap))
            p = p + _dot(p, _bd(ap))
        from_state = _dot(lhs, st)
        from_v = _dot(a_k, _bd(v_c))
        uu = _dot(p, _bd(from_state[:CHUNK] + from_v[:CHUNK]))
        y_s[sl, :] = from_state[CHUNK:] + from_v[CHUNK:] + _dot(a_rb, _bd(uu))
        tail = jnp.exp(cs[CHUNK - 1:CHUNK, :] - cs)
        upd = _dot_tn(jnp.concatenate([b_c * tail, k_c * tail], axis=0), jnp.concatenate([uu, v_c], axis=0))
        g_col = jnp.exp(_dot_tn_sel(lw_c, jnp.ones((CHUNK, gw), BF16)))
        st_ref[...] = g_col * st + jnp.where(_head_mask(gw, gw, 6, 6), upd, 0.0)
        return carry

    lax.fori_loop(0, ct // CHUNK, chunk_body, 0)

    y = y_s[...]
    mean = _dot_sel(y, ones_bd) * (1.0 / HEAD_DIM)
    yc = y - mean
    var = _dot_sel(yc * yc, ones_bd) * (1.0 / HEAD_DIM)
    yn = yc * lax.rsqrt(var + RWKV_GN_EPS) * gng_ref[...] + gnb_ref[...]
    bonus = _dot_sel(r_s[...] * k_s[...] * rk_ref[...], ones_bd) * v_s[...]
    o_ref[0] = ((yn + bonus) * g_s[...]).astype(o_ref.dtype)


def _rwkv(u_a, prm, ct):
    bsz, s, _ = u_a.shape
    gw = GROUP_WIDTH
    vec = lambda n: pl.BlockSpec((1, n), lambda b, c: (0, 0))
    mat = lambda m, n: pl.BlockSpec((m, n), lambda b, c: (0, 0))
    tile = lambda n: pltpu.VMEM((ct, n), F32)
    return pl.pallas_call(
        _rwkv_kernel,
        grid=(bsz, s // ct),
        in_specs=[pl.BlockSpec((1, ct, D_IN_A), lambda b, c: (b, c, 0)), vec(D_IN_A), vec(gw), mat(MISC, gw),
                  vec(gw), mat(MISC, gw), mat(MISC, gw), vec(gw), vec(gw), vec(gw), vec(gw), vec(gw)],
        out_specs=pl.BlockSpec((1, ct, gw), lambda b, c: (b, c, 0)),
        out_shape=jax.ShapeDtypeStruct((bsz, s, gw), BF16),
        scratch_shapes=[pltpu.VMEM((gw, gw), F32), pltpu.VMEM((8, D_IN_A), F32)] + [tile(gw)] * 8,
        compiler_params=_cparams(2),
        name="rwkv7",
    )(u_a, *prm)


def _gla_kernel(u_ref, m_ref, aup_ref, ab_ref, ng_ref, o_ref,
                st_ref, q_s, k_s, b_s, v_s, oi_s, ox_s):
    ct = u_ref.shape[1]
    qw, gw = GLA_QK_WIDTH, GROUP_WIDTH

    @pl.when(pl.program_id(1) == 0)
    def _():
        st_ref[...] = jnp.zeros_like(st_ref)

    u = u_ref[0]
    q_s[...] = u[:, 0:qw] * (GLA_KEY_DIM ** -0.5)
    k_s[...] = u[:, qw:2 * qw]
    v_s[...] = u[:, 2 * qw:2 * qw + gw]
    gate = u[:, 2 * qw + gw:]
    log_a = -_softplus(-(_dot(m_ref[0], aup_ref[...]) + ab_ref[...])) * (1.0 / GLA_TAU)
    head_mask = _head_mask(qw, gw, 5, 6)

    b_s[...] = log_a

    def chunk_body(ci, carry):
        sl = pl.ds(pl.multiple_of(ci * CHUNK, CHUNK), CHUNK)
        la_c = b_s[sl, :]
        q_c, k_c, v_c = q_s[sl, :], k_s[sl, :], v_s[sl, :]
        st = st_ref[...]
        b = _sel_dot(_tril_ones(CHUNK), la_c)
        b_s[sl, :] = b
        ox_s[sl, :] = _dot(q_c * jnp.exp(b), st)
        tail = jnp.exp(b[CHUNK - 1:CHUNK, :] - b)
        g_col = jnp.exp(_dot_tn_sel(la_c, jnp.ones((CHUNK, gw), BF16)))
        st_ref[...] = g_col * st + jnp.where(head_mask, _dot_tn(k_c * tail, v_c), 0.0)
        return carry

    lax.fori_loop(0, ct // CHUNK, chunk_body, 0)

    seg_bcast = jnp.where(head_mask, 1.0, 0.0).astype(BF16)

    def row_body(i, carry):
        c0 = pl.multiple_of(lax.shift_left(_shr(i, 6), jnp.int32(6)), CHUNK)
        sl = pl.ds(c0, CHUNK)
        k_c, b_c, v_c = k_s[sl, :], b_s[sl, :], v_s[sl, :]
        q_i, b_i = q_s[pl.ds(i, 1), :], b_s[pl.ds(i, 1), :]
        valid = (_iota((CHUNK, qw), 0) + c0) <= i
        p = jnp.where(valid, k_c * q_i * jnp.exp(jnp.minimum(b_i - b_c, 0.0)), 0.0)
        sx = jnp.dot(p.astype(BF16), seg_bcast, preferred_element_type=F32)
        oi_s[pl.ds(i, 1), :] = jnp.sum(sx * v_c, axis=0, keepdims=True)
        return carry

    lax.fori_loop(0, ct, row_body, 0, unroll=8)

    o = oi_s[...] + ox_s[...]
    ms = _dot_sel(o * o, _head_ones(gw, gw, 6, 6)) * (1.0 / HEAD_DIM)
    on = o * lax.rsqrt(ms + NORM_EPS) * ng_ref[...]
    o_ref[0] = (on * (gate * _sigmoid(gate))).astype(o_ref.dtype)


def _gla(u_b, misc, prm, ct):
    bsz, s, wb = u_b.shape
    qw, gw = GLA_QK_WIDTH, GROUP_WIDTH
    return pl.pallas_call(
        _gla_kernel,
        grid=(bsz, s // ct),
        in_specs=[pl.BlockSpec((1, ct, wb), lambda b, c: (b, c, 0)),
                  pl.BlockSpec((1, ct, MISC), lambda b, c: (b, c, 0)),
                  pl.BlockSpec((MISC, qw), lambda b, c: (0, 0)),
                  pl.BlockSpec((1, qw), lambda b, c: (0, 0)),
                  pl.BlockSpec((1, gw), lambda b, c: (0, 0))],
        out_specs=pl.BlockSpec((1, ct, gw), lambda b, c: (b, c, 0)),
        out_shape=jax.ShapeDtypeStruct((bsz, s, gw), BF16),
        scratch_shapes=[pltpu.VMEM((qw, gw), F32), pltpu.VMEM((ct, qw), F32), pltpu.VMEM((ct, qw), F32),
                        pltpu.VMEM((ct, qw), F32), pltpu.VMEM((ct, gw), F32), pltpu.VMEM((ct, gw), F32),
                        pltpu.VMEM((ct, gw), F32)],
        compiler_params=_cparams(2),
        name="gla",
    )(u_b, misc, *prm)


def _mlstm_kernel(u_ref, m_ref, cw_ref, cb_ref, eig_ref, efg_ref, ib_ref, fb_ref, ng_ref, o_ref,
                  mem_ref, nb_ref, mst_ref, prev_ref, q_s, k_s, v_s, li_s, lf_s, h_s):
    ct = u_ref.shape[1]
    gw = GROUP_WIDTH

    @pl.when(pl.program_id(1) == 0)
    def _():
        mem_ref[...] = jnp.zeros_like(mem_ref)
        nb_ref[...] = jnp.zeros_like(nb_ref)
        mst_ref[...] = jnp.zeros_like(mst_ref)
        prev_ref[...] = jnp.zeros_like(prev_ref)

    u = u_ref[0]
    qk = u[:, 0:2 * gw]
    row = _iota(qk.shape, 0)
    prev8 = prev_ref[...]
    conv = cb_ref[...] + jnp.zeros_like(qk)
    for j in range(MLSTM_CONV):
        conv = conv + cw_ref[j:j + 1, :] * _shift_rows(qk, prev8, MLSTM_CONV - 1 - j, row)
    prev_ref[...] = u_ref[0, ct - 8:ct, 0:2 * gw]
    qk_act = conv * _sigmoid(conv)
    q_s[...] = qk_act[:, 0:gw]
    k_s[...] = qk_act[:, gw:] * (HEAD_DIM ** -0.5)
    v_s[...] = u[:, 2 * gw:3 * gw]
    o_gate = u[:, 3 * gw:]
    misc = m_ref[0]
    li_s[...] = _dot_sel(misc, eig_ref[...]) + ib_ref[...]
    lf_s[...] = -_softplus(-(_dot_sel(misc, efg_ref[...]) + fb_ref[...]))
    neg_inf = -jnp.inf

    def chunk_body(ci, carry):
        sl = pl.ds(pl.multiple_of(ci * CHUNK, CHUNK), CHUNK)
        q_c, k_c, v_c, li_c, lf_c = q_s[sl, :], k_s[sl, :], v_s[sl, :], li_s[sl, :], lf_s[sl, :]
        mem, nb, m_e = mem_ref[...], nb_ref[...], mst_ref[0:1, :]
        shape = (CHUNK, gw)
        i_idx, j_idx = _iota(shape, 0), _lane_j(shape)
        incl = j_idx <= i_idx
        head_mask = _head_mask(gw, gw, 6, 6)
        fc = _sel_dot(_tril_ones(CHUNK), lf_c)
        g_row = li_c - fc
        ones_avg = jnp.full(shape, 1.0 / HEAD_DIM, BF16)
        g_hi, g_lo = _split(_bd(g_row))
        dn = (((1,), (1,)), ((), ()))
        g_t = (lax.dot_general(ones_avg, g_hi, dn, preferred_element_type=F32)
               + lax.dot_general(ones_avg, g_lo, dn, preferred_element_type=F32))
        log_w = jnp.where(incl, fc + g_t, neg_inf)
        cm = g_row
        for sh in (1, 2, 4, 8, 16, 32):
            cm = jnp.maximum(cm, jnp.where(i_idx < sh, neg_inf, pltpu.roll(cm, sh, 0)))
        log_carry = fc + m_e
        m_row = jnp.maximum(fc + cm, log_carry)
        sc = _dot_nt(q_c, _bd(k_c)) * jnp.exp(log_w - m_row)
        w_c = jnp.exp(log_carry - m_row)
        scb = sc.astype(BF16)
        num = jnp.dot(scb, _bd(v_c).astype(BF16), preferred_element_type=F32) + w_c * _dot(q_c, mem)
        den = (jnp.dot(scb, _head_ones(gw, gw, 6, 6), preferred_element_type=F32) + w_c * _dot(q_c, nb))
        h_s[sl, :] = num / jnp.maximum(jnp.abs(den), jnp.exp(-m_row))
        f_end = fc[CHUNK - 1:CHUNK, :]
        log_kv = f_end - fc + li_c
        m_new = jnp.maximum(f_end + m_e, jnp.max(log_kv, axis=0, keepdims=True))
        kwk = jnp.exp(log_kv - m_new) * k_c
        cd = jnp.exp(f_end + m_e - m_new)
        cd_rows = jnp.where(_iota((16, gw), 0) == 0, cd, 0.0)
        cd_col = _dot_tn_sel(cd_rows, jnp.ones((16, gw), BF16))
        upd = _dot_tn(kwk, jnp.concatenate([v_c, jnp.ones(shape, F32)], axis=1))
        mem_ref[...] = cd_col * mem + jnp.where(head_mask, upd[:, :gw], 0.0)
        nb_ref[...] = cd_col * nb + jnp.where(head_mask, upd[:, gw:], 0.0)
        mst_ref[0:1, :] = m_new
        return carry

    lax.fori_loop(0, ct // CHUNK, chunk_body, 0)

    h = h_s[...]
    ones_bd = _head_ones(gw, gw, 6, 6)
    mean = _dot_sel(h, ones_bd) * (1.0 / HEAD_DIM)
    hc = h - mean
    var = _dot_sel(hc * hc, ones_bd) * (1.0 / HEAD_DIM)
    hn = hc * lax.rsqrt(var + LN_EPS) * ng_ref[...]
    o_ref[0] = (hn * _sigmoid(o_gate)).astype(o_ref.dtype)


def _mlstm(u_c, misc, prm, ct):
    bsz, s, wc = u_c.shape
    gw = GROUP_WIDTH
    vec = lambda n: pl.BlockSpec((1, n), lambda b, c: (0, 0))
    mat = lambda m, n: pl.BlockSpec((m, n), lambda b, c: (0, 0))
    tile = pltpu.VMEM((ct, gw), F32)
    return pl.pallas_call(
        _mlstm_kernel,
        grid=(bsz, s // ct),
        in_specs=[pl.BlockSpec((1, ct, wc), lambda b, c: (b, c, 0)),
                  pl.BlockSpec((1, ct, MISC), lambda b, c: (b, c, 0)),
                  mat(MLSTM_CONV, 2 * gw), vec(2 * gw), mat(MISC, gw), mat(MISC, gw), vec(gw), vec(gw), vec(gw)],
        out_specs=pl.BlockSpec((1, ct, gw), lambda b, c: (b, c, 0)),
        out_shape=jax.ShapeDtypeStruct((bsz, s, gw), BF16),
        scratch_shapes=[pltpu.VMEM((gw, gw), F32), pltpu.VMEM((gw, gw), F32), pltpu.VMEM((8, gw), F32),
                        pltpu.VMEM((8, 2 * gw), F32)] + [tile] * 6,
        compiler_params=_cparams(2),
        name="mlstm",
    )(u_c, misc, *prm)


def _mla_prep_kernel(u_ref, m_ref, cq_ref, sq_ref, csk_ref, qg_ref, kvg_ref, wq_ref, wqr_ref, wk_ref, wv_ref,
                     selk_ref, q_ref, k_ref, v_ref):
    u = u_ref[0]
    cq = u[:, 0:MLA_Q_RANK]
    ckv = u[:, MLA_Q_RANK:]
    cqn = cq * lax.rsqrt(jnp.mean(cq * cq, axis=-1, keepdims=True) + NORM_EPS) * qg_ref[...]
    ckvn = ckv * lax.rsqrt(jnp.mean(ckv * ckv, axis=-1, keepdims=True) + NORM_EPS) * kvg_ref[...]
    cqb, ckvb = cqn.astype(BF16), ckvn.astype(BF16)
    k_rope = _dot(m_ref[0] * csk_ref[0], selk_ref[...])
    cos_q, sin_q = cq_ref[0], sq_ref[0]
    scale = MLA_QK ** -0.5
    for h in range(N_HEADS):
        qh = (jnp.dot(cqb, wq_ref[h], preferred_element_type=F32) * cos_q
              + jnp.dot(cqb, wqr_ref[h], preferred_element_type=F32) * sin_q) * scale
        q_ref[0, h] = qh.astype(BF16)
        k_ref[0, h] = (jnp.dot(ckvb, wk_ref[h], preferred_element_type=F32) + k_rope).astype(BF16)
        v_ref[0, h] = jnp.dot(ckvb, wv_ref[h], preferred_element_type=F32).astype(BF16)


def _mla_prep(u_d, misc, tabs, prm, tm):
    bsz, s, wd = u_d.shape
    full = lambda a: pl.BlockSpec(a.shape, lambda b, c: (0,) * a.ndim)
    tok = lambda n: pl.BlockSpec((1, tm, n), lambda b, c: (b, c, 0))
    head = lambda n: pl.BlockSpec((1, N_HEADS, tm, n), lambda b, c: (b, 0, c, 0))
    return pl.pallas_call(
        _mla_prep_kernel,
        grid=(bsz, s // tm),
        in_specs=[tok(wd), tok(MISC), tok(MLA_QK), tok(MLA_QK), tok(MISC)] + [full(a) for a in prm],
        out_specs=[head(MLA_QK), head(MLA_QK), head(HEAD_DIM)],
        out_shape=[jax.ShapeDtypeStruct((bsz, N_HEADS, s, MLA_QK), BF16),
                   jax.ShapeDtypeStruct((bsz, N_HEADS, s, MLA_QK), BF16),
                   jax.ShapeDtypeStruct((bsz, N_HEADS, s, HEAD_DIM), BF16)],
        compiler_params=_cparams(2),
        name="mla_prep",
    )(u_d, misc, *tabs, *prm)


def _attn_kernel(q_ref, k_ref, v_ref, o_ref):
    tq = q_ref.shape[2]
    qi = pl.program_id(2)
    q = q_ref[0, 0]

    def step(j, carry, diagonal):
        m, l, acc = carry
        sl = pl.ds(pl.multiple_of(j * tq, tq), tq)
        kb, vb = k_ref[0, 0, sl, :], v_ref[0, 0, sl, :]
        s = lax.dot_general(q, kb, (((1,), (1,)), ((), ())), preferred_element_type=F32)
        if diagonal:
            s = jnp.where(_iota((tq, tq), 1) <= _iota((tq, tq), 0), s, -jnp.inf)
        m_new = jnp.maximum(m, jnp.max(s, axis=-1, keepdims=True))
        p = jnp.exp(s - m_new)
        alpha = jnp.exp(m - m_new)
        l = alpha * l + jnp.sum(p, axis=-1, keepdims=True)
        acc = alpha * acc + jnp.dot(p.astype(BF16), vb, preferred_element_type=F32)
        return m_new, l, acc

    init = (jnp.full((tq, 1), -jnp.inf, F32), jnp.zeros((tq, 1), F32), jnp.zeros((tq, HEAD_DIM), F32))
    carry = lax.fori_loop(0, qi, lambda j, c: step(j, c, False), init)
    _, l, acc = step(qi, carry, True)
    o_ref[0, 0] = (acc / l).astype(o_ref.dtype)


def _attention(q, k, v, tq):
    bsz, nh, s, _ = q.shape
    return pl.pallas_call(
        _attn_kernel,
        grid=(bsz, nh, s // tq),
        in_specs=[pl.BlockSpec((1, 1, tq, MLA_QK), lambda b, h, i: (b, h, i, 0)),
                  pl.BlockSpec((1, 1, s, MLA_QK), lambda b, h, i: (b, h, 0, 0)),
                  pl.BlockSpec((1, 1, s, HEAD_DIM), lambda b, h, i: (b, h, 0, 0))],
        out_specs=pl.BlockSpec((1, 1, tq, HEAD_DIM), lambda b, h, i: (b, h, i, 0)),
        out_shape=jax.ShapeDtypeStruct((bsz, nh, s, HEAD_DIM), BF16),
        compiler_params=_cparams(3),
        name="mla_attn",
    )(q, k, v)


def _layer_norm(z, g, b):
    zc = z - jnp.mean(z, axis=-1, keepdims=True)
    return zc * lax.rsqrt(jnp.mean(zc * zc, axis=-1, keepdims=True) + LN_EPS) * g + b


def _outproj_kernel(x_ref, ya_ref, yb_ref, yc_ref, yd_ref, wo_ref, wod_ref, g_ref, b_ref, o_ref):
    gw = GROUP_WIDTH
    mix = jnp.dot(ya_ref[0], wo_ref[0:gw, :], preferred_element_type=F32)
    mix = mix + jnp.dot(yb_ref[0], wo_ref[gw:2 * gw, :], preferred_element_type=F32)
    mix = mix + jnp.dot(yc_ref[0], wo_ref[2 * gw:3 * gw, :], preferred_element_type=F32)
    for h in range(N_HEADS):
        mix = mix + jnp.dot(yd_ref[0, h], wod_ref[h], preferred_element_type=F32)
    o_ref[0] = _layer_norm(DN_ALPHA * x_ref[0] + mix, g_ref[...], b_ref[...])


def _outproj(x, ya, yb, yc, yd, wo, wod, g, b, tm):
    bsz, s, d = x.shape
    gw = GROUP_WIDTH
    tok = lambda n: pl.BlockSpec((1, tm, n), lambda bb, c: (bb, c, 0))
    full = lambda a: pl.BlockSpec(a.shape, lambda bb, c: (0,) * a.ndim)
    return pl.pallas_call(
        _outproj_kernel,
        grid=(bsz, s // tm),
        in_specs=[tok(d), tok(gw), tok(gw), tok(gw),
                  pl.BlockSpec((1, N_HEADS, tm, HEAD_DIM), lambda bb, c: (bb, 0, c, 0)),
                  full(wo), full(wod), full(g), full(b)],
        out_specs=tok(d),
        out_shape=jax.ShapeDtypeStruct((bsz, s, d), F32),
        compiler_params=_cparams(2),
        name="outproj_ln1",
    )(x, ya, yb, yc, yd, wo, wod, g, b)


def _router_kernel(x_ref, w_ref, b_ref, idx_ref, gate_ref):
    x = x_ref[...]
    xh, xl = _split(x)
    wh, wl = _split(w_ref[...])
    dn = (((1,), (1,)), ((), ()))
    logits = (lax.dot_general(wh, xh, dn, preferred_element_type=F32)
              + lax.dot_general(wh, xl, dn, preferred_element_type=F32)
              + lax.dot_general(wl, xh, dn, preferred_element_type=F32)) + b_ref[...]
    tm = x.shape[0]
    ng, ne = N_EXPERT_GROUPS, EXPERTS_PER_GROUP
    gl = [logits[i:i + 1, :] for i in range(ng)]
    gmax = functools.reduce(jnp.maximum, gl)
    gsum = functools.reduce(jnp.add, [jnp.exp(g - gmax) for g in gl])
    group_p = 1.0 / gsum
    gidx = jnp.full((1, tm), ng - 1, jnp.int32)
    for i in range(ng - 2, -1, -1):
        gidx = jnp.where(gl[i] == gmax, i, gidx)
    el = logits[8 + (ng - 1) * ne:8 + ng * ne, :]
    for i in range(ng - 2, -1, -1):
        el = jnp.where(gidx == i, logits[8 + i * ne:8 + (i + 1) * ne, :], el)
    rows = _iota((ne, tm), 0)
    m1 = jnp.max(el, axis=0, keepdims=True)
    i1 = jnp.min(jnp.where(el == m1, rows, ne), axis=0, keepdims=True)
    el2 = jnp.where(rows == i1, -jnp.inf, el)
    m2 = jnp.max(el2, axis=0, keepdims=True)
    i2 = jnp.min(jnp.where(el2 == m2, rows, ne), axis=0, keepdims=True)
    z = jnp.sum(jnp.exp(el - m1), axis=0, keepdims=True)
    p1 = 1.0 / z
    p2 = jnp.exp(m2 - m1) / z
    base = gidx * ne
    idx_ref[...] = jnp.concatenate([base + i1, base + i2], axis=0)
    gate_ref[...] = jnp.concatenate([group_p * p1 / (p1 + p2), group_p * p2 / (p1 + p2)], axis=0)


def _router(x2d, w_t, b_col, tm):
    t_tok, d = x2d.shape
    return pl.pallas_call(
        _router_kernel,
        grid=(t_tok // tm,),
        in_specs=[pl.BlockSpec((tm, d), lambda i: (i, 0)), pl.BlockSpec(w_t.shape, lambda i: (0, 0)),
                  pl.BlockSpec(b_col.shape, lambda i: (0, 0))],
        out_specs=[pl.BlockSpec((TOP_K, tm), lambda i: (0, i)), pl.BlockSpec((TOP_K, tm), lambda i: (0, i))],
        out_shape=[jax.ShapeDtypeStruct((TOP_K, t_tok), jnp.int32), jax.ShapeDtypeStruct((TOP_K, t_tok), F32)],
        compiler_params=_cparams(1),
        name="router",
    )(x2d, w_t, b_col)


def _moe_kernel(be_ref, x_ref, gate_ref, wg_ref, wu_ref, wd_ref, o_ref, wg_s, wu_s, wd_s):
    i = pl.program_id(0)

    @pl.when(jnp.logical_or(i == 0, be_ref[i] != be_ref[jnp.maximum(i - 1, 0)]))
    def _():
        wg_s[...] = wg_ref[0].astype(BF16)
        wu_s[...] = wu_ref[0].astype(BF16)
        wd_s[...] = wd_ref[0].astype(BF16)

    xb = x_ref[...]
    hg = jnp.dot(xb, wg_s[...], preferred_element_type=F32)
    hu = jnp.dot(xb, wu_s[...], preferred_element_type=F32)
    hid = (hg * _sigmoid(hg)) * hu
    o_ref[...] = jnp.dot(hid.astype(BF16), wd_s[...], preferred_element_type=F32) * gate_ref[...]


def _moe(x_rows, block_expert, row_gate, wg, wu, wd):
    n_rows, d = x_rows.shape
    n_blocks = n_rows // MOE_BLOCK
    grid_spec = pltpu.PrefetchScalarGridSpec(
        num_scalar_prefetch=1,
        grid=(n_blocks,),
        in_specs=[pl.BlockSpec((MOE_BLOCK, d), lambda i, be: (i, 0)),
                  pl.BlockSpec((MOE_BLOCK, 1), lambda i, be: (i, 0)),
                  pl.BlockSpec((1, d, D_EXPERT), lambda i, be: (be[i], 0, 0)),
                  pl.BlockSpec((1, d, D_EXPERT), lambda i, be: (be[i], 0, 0)),
                  pl.BlockSpec((1, D_EXPERT, d), lambda i, be: (be[i], 0, 0))],
        out_specs=pl.BlockSpec((MOE_BLOCK, d), lambda i, be: (i, 0)),
        scratch_shapes=[pltpu.VMEM((d, D_EXPERT), BF16), pltpu.VMEM((d, D_EXPERT), BF16),
                        pltpu.VMEM((D_EXPERT, d), BF16)],
    )
    return pl.pallas_call(
        _moe_kernel,
        grid_spec=grid_spec,
        out_shape=jax.ShapeDtypeStruct((n_rows, d), F32),
        compiler_params=_cparams(1),
        name="moe_experts",
    )(block_expert, x_rows, row_gate.reshape(n_rows, 1), wg, wu, wd)


RANK_BLOCK = 512


def _dispatch(idx, gate, t_tok):
    n_asg = t_tok * TOP_K
    flat_e = idx.reshape(n_asg)
    flat_g = gate.reshape(n_asg)
    rb = min(RANK_BLOCK, n_asg)
    onehot = (flat_e.reshape(n_asg // rb, rb, 1) == jnp.arange(N_EXPERTS, dtype=jnp.int32)).astype(BF16)
    before = (jnp.arange(rb)[None, :] < jnp.arange(rb)[:, None]).astype(BF16)
    within = jnp.einsum('ij,bjk->bik', before, onehot, preferred_element_type=F32)
    totals = jnp.sum(onehot.astype(F32), axis=1)
    offsets = jnp.cumsum(totals, axis=0) - totals
    rank = jnp.sum((within + offsets[:, None, :]) * onehot.astype(F32), axis=-1).reshape(n_asg).astype(jnp.int32)
    counts = jnp.sum(totals, axis=0).astype(jnp.int32)
    starts = jnp.cumsum(counts) - counts
    padded = (counts + MOE_BLOCK - 1) // MOE_BLOCK * MOE_BLOCK
    pad_ends = jnp.cumsum(padded)
    pad_starts = pad_ends - padded
    n_rows = -(-(n_asg + N_EXPERTS * (MOE_BLOCK - 1)) // MOE_BLOCK) * MOE_BLOCK
    n_blocks = n_rows // MOE_BLOCK
    block_expert = jnp.minimum(jnp.searchsorted(pad_ends, jnp.arange(n_blocks) * MOE_BLOCK, side='right'),
                               N_EXPERTS - 1).astype(jnp.int32)
    pos = pad_starts[flat_e] + rank
    order = jnp.argsort(flat_e)
    row_e = jnp.repeat(block_expert, MOE_BLOCK)
    row_rank = jnp.arange(n_rows, dtype=jnp.int32) - pad_starts[row_e]
    valid = row_rank < counts[row_e]
    asg = order[jnp.where(valid, starts[row_e] + row_rank, 0)]
    row_tok = jnp.where(valid, asg % t_tok, 0).astype(jnp.int32)
    row_gate = jnp.where(valid, flat_g[asg], 0.0)
    return block_expert, row_tok, pos.reshape(TOP_K, t_tok), row_gate


def _final_kernel(x_ref, p_ref, f0_ref, f1_ref, wg_ref, bg_ref, wp_ref, g_ref, b_ref, o_ref):
    x = x_ref[...]
    gate = _sigmoid(_dot(x, wg_ref[...]) + bg_ref[...])
    ple = gate * _dot(p_ref[...], wp_ref[...])
    ffn = f0_ref[...] + f1_ref[...]
    o_ref[...] = _layer_norm(DN_ALPHA * x + ffn + ple, g_ref[...], b_ref[...])


def _final(x2d, p2d, ffn0, ffn1, wg, bg, wp, g, b, tm):
    t_tok, d = x2d.shape
    full = lambda a: pl.BlockSpec(a.shape, lambda i: (0,) * a.ndim)
    return pl.pallas_call(
        _final_kernel,
        grid=(t_tok // tm,),
        in_specs=[pl.BlockSpec((tm, d), lambda i: (i, 0)), pl.BlockSpec((tm, p2d.shape[1]), lambda i: (i, 0)),
                  pl.BlockSpec((tm, d), lambda i: (i, 0)), pl.BlockSpec((tm, d), lambda i: (i, 0)),
                  full(wg), full(bg), full(wp), full(g), full(b)],
        out_specs=pl.BlockSpec((tm, d), lambda i: (i, 0)),
        out_shape=jax.ShapeDtypeStruct((t_tok, d), F32),
        compiler_params=_cparams(1),
        name="ple_ln2",
    )(x2d, p2d, ffn0, ffn1, wg, bg, wp, g, b)


def _pad_rows(w, offset, total=MISC):
    return jnp.zeros((total, w.shape[1]), w.dtype).at[offset:offset + w.shape[0]].set(w)


def _rot_cols(w):
    half = w.shape[1] // 2
    return jnp.concatenate([-w[:, half:], w[:, :half]], axis=1)


def _row(v):
    return v.reshape(1, -1)


def _expand_heads(v):
    return jnp.repeat(v, HEAD_DIM).reshape(1, GROUP_WIDTH)


def _head_select(offset):
    sel = np.zeros((MISC, GROUP_WIDTH), np.float32)
    for h in range(N_HEADS):
        sel[offset + h, h * HEAD_DIM:(h + 1) * HEAD_DIM] = 1.0
    return jnp.asarray(sel, BF16)


def _rope_tables(positions):
    inv_freq = ROPE_THETA ** (-jnp.arange(0, MLA_ROPE, 2, dtype=F32) / MLA_ROPE)
    ang = positions.astype(F32)[..., None] * inv_freq
    cos, sin = jnp.cos(ang), jnp.sin(ang)
    cos2 = jnp.concatenate([cos, cos], axis=-1)
    sin2 = jnp.concatenate([sin, sin], axis=-1)
    lead = cos.shape[:-1]
    cos_q = jnp.concatenate([jnp.ones(lead + (MLA_NOPE,), F32), cos2], axis=-1)
    sin_q = jnp.concatenate([jnp.zeros(lead + (MLA_NOPE,), F32), sin2], axis=-1)
    cs_k = jnp.concatenate([jnp.zeros(lead + (MISC_KR,), F32), cos2, sin2,
                            jnp.zeros(lead + (MISC - MISC_KR_ROT - MLA_ROPE,), F32)], axis=-1)
    return cos_q, sin_q, cs_k


def _rope_key_select():
    sel = np.zeros((MISC, MLA_QK), np.float32)
    for c in range(MLA_ROPE):
        sel[MISC_KR + c, MLA_NOPE + c] = 1.0
        sel[MISC_KR_ROT + c, MLA_NOPE + c] = 1.0
    return jnp.asarray(sel, BF16)


def _split_w_in(w):
    a0, b0 = 0, D_IN_A
    c0, d0 = b0 + D_IN_B, b0 + D_IN_B + D_IN_C
    qw, gw = GLA_QK_WIDTH, GROUP_WIDTH
    w_a = w[:, a0:b0]
    gla_ad = w[:, b0 + 2 * qw + gw:b0 + 2 * qw + gw + GLA_GATE_RANK]
    w_b = jnp.concatenate([w[:, b0:b0 + 2 * qw + gw], w[:, b0 + 2 * qw + gw + GLA_GATE_RANK:c0]], axis=1)
    w_c = w[:, c0:c0 + 4 * gw]
    gates = w[:, c0 + 4 * gw:d0]
    w_d = w[:, d0:d0 + MLA_Q_RANK + MLA_KV_RANK]
    kr = w[:, d0 + MLA_Q_RANK + MLA_KV_RANK:]
    z = lambda n: jnp.zeros((w.shape[0], n), w.dtype)
    w_m = jnp.concatenate([gla_ad, gates, z(MISC_KR - MISC_FG - N_HEADS), kr, _rot_cols(kr),
                           z(MISC - MISC_KR_ROT - MLA_ROPE)], axis=1)
    return [t.astype(BF16) for t in (w_a, w_b, w_c, w_d, w_m)]


def _tiles(s):
    return min(512, s), min(512, s)


def kernel(x, p, positions, w_in, rwkv_mu, rwkv_w0, rwkv_w_up, rwkv_a0, rwkv_a_up, rwkv_g_up, rwkv_k_k, rwkv_k_a, rwkv_r_k, rwkv_gn_g, rwkv_gn_b, gla_alpha_up, gla_alpha_b, gla_norm_g, mlstm_conv_w, mlstm_conv_b, mlstm_i_b, mlstm_f_b, mlstm_norm_g, mla_q_norm_g, mla_w_uq, mla_kv_norm_g, mla_w_ukv, w_out, ln1_g, ln1_b, moe_w_rg, moe_b_rg, moe_w_re, moe_b_re, moe_w_gate, moe_w_up, moe_w_down, ple_w_gate, ple_b_gate, ple_w, ln2_g, ln2_b):
    bsz, s, d = x.shape
    t_tok = bsz * s
    depth = w_in.shape[0]
    ct, tq = _tiles(s)
    tm = min(512, t_tok)
    rope_tabs = _rope_tables(positions)
    sel_ig, sel_fg, sel_k = _head_select(MISC_IG), _head_select(MISC_FG), _rope_key_select()
    for i in range(depth):
        u_a, u_b, u_c, u_d, u_m = _project(x.reshape(t_tok, d), _split_w_in(w_in[i]), tm)
        u_a, u_b, u_c, u_d, u_m = (t.reshape(bsz, s, -1) for t in (u_a, u_b, u_c, u_d, u_m))

        rk = RWKV_DECAY_RANK
        ya = _rwkv(u_a, (_row(rwkv_mu[i]), _row(rwkv_w0[i]), _pad_rows(rwkv_w_up[i], 0).astype(BF16),
                         _row(rwkv_a0[i]), _pad_rows(rwkv_a_up[i], rk).astype(BF16),
                         _pad_rows(rwkv_g_up[i], rk + RWKV_ICL_RANK).astype(BF16),
                         _row(rwkv_k_k[i]), _row(rwkv_k_a[i]), _row(rwkv_r_k[i]),
                         _row(rwkv_gn_g[i]), _row(rwkv_gn_b[i])), ct)
        yb = _gla(u_b, u_m, (_pad_rows(gla_alpha_up[i], MISC_GLA_AD).astype(BF16), _row(gla_alpha_b[i]),
                             _row(gla_norm_g[i])), ct)
        yc = _mlstm(u_c, u_m, (mlstm_conv_w[i], _row(mlstm_conv_b[i]), sel_ig, sel_fg,
                               _expand_heads(mlstm_i_b[i]), _expand_heads(mlstm_f_b[i]),
                               _row(mlstm_norm_g[i])), ct)

        wq = mla_w_uq[i].reshape(MLA_Q_RANK, N_HEADS, MLA_QK).transpose(1, 0, 2)
        wq_rot = jnp.concatenate([jnp.zeros((N_HEADS, MLA_Q_RANK, MLA_NOPE), F32),
                                  jax.vmap(_rot_cols)(wq[:, :, MLA_NOPE:])], axis=2)
        wkv = mla_w_ukv[i].reshape(MLA_KV_RANK, N_HEADS, MLA_NOPE + HEAD_DIM).transpose(1, 0, 2)
        wk = jnp.concatenate([wkv[:, :, :MLA_NOPE], jnp.zeros((N_HEADS, MLA_KV_RANK, MLA_ROPE), F32)], axis=2)
        q, k, v = _mla_prep(u_d, u_m, rope_tabs,
                            (_row(mla_q_norm_g[i]), _row(mla_kv_norm_g[i]), wq.astype(BF16), wq_rot.astype(BF16),
                             wk.astype(BF16), wkv[:, :, MLA_NOPE:].astype(BF16), sel_k), ct)
        yd = _attention(q, k, v, tq)

        wo = w_out[i].astype(BF16)
        x = _outproj(x, ya, yb, yc, yd, wo[:3 * GROUP_WIDTH], wo[3 * GROUP_WIDTH:].reshape(N_HEADS, HEAD_DIM, d),
                     _row(ln1_g[i]), _row(ln1_b[i]), ct)

        x2d = x.reshape(t_tok, d)
        w_route = jnp.concatenate([moe_w_rg[i].T, jnp.zeros((8 - N_EXPERT_GROUPS, d), F32), moe_w_re[i].T], axis=0)
        b_route = jnp.concatenate([moe_b_rg[i], jnp.zeros((8 - N_EXPERT_GROUPS,), F32), moe_b_re[i]]).reshape(-1, 1)
        idx, gate = _router(x2d, w_route, b_route, tm)
        block_expert, row_tok, pos, row_gate = _dispatch(idx, gate, t_tok)
        y_rows = _moe(x2d.astype(BF16)[row_tok], block_expert, row_gate, moe_w_gate[i], moe_w_up[i], moe_w_down[i])
        x = _final(x2d, p[i].reshape(t_tok, -1), y_rows[pos[0]], y_rows[pos[1]], ple_w_gate[i].astype(BF16),
                   _row(ple_b_gate[i]), ple_w[i].astype(BF16), _row(ln2_g[i]), _row(ln2_b[i]), tm).reshape(bsz, s, d)
    return x
```

```python
import functools

import jax
import jax.numpy as jnp
import numpy as np
from jax import lax
from jax.experimental import pallas as pl
from jax.experimental.pallas import tpu as pltpu

F32 = jnp.float32
BF16 = jnp.bfloat16

D_MODEL = 1024
DEPTH = 4
HEAD_DIM = 64
N_HEADS = 4
GROUP_WIDTH = N_HEADS * HEAD_DIM
D_PLE = 256

RWKV_DECAY_RANK = 32
RWKV_ICL_RANK = 32
RWKV_GATE_RANK = 64
RWKV_GN_EPS = 64e-5
D_IN_A = 3 * GROUP_WIDTH + RWKV_DECAY_RANK + RWKV_ICL_RANK + RWKV_GATE_RANK

GLA_KEY_DIM = 32
GLA_QK_WIDTH = N_HEADS * GLA_KEY_DIM
GLA_GATE_RANK = 16
GLA_TAU = 16.0
D_IN_B = 2 * GLA_QK_WIDTH + 2 * GROUP_WIDTH + GLA_GATE_RANK

MLSTM_CONV = 4
D_IN_C = 4 * GROUP_WIDTH + 2 * N_HEADS

MLA_Q_RANK = 256
MLA_KV_RANK = 128
MLA_NOPE = 64
MLA_ROPE = 32
MLA_QK = MLA_NOPE + MLA_ROPE
D_IN_D = MLA_Q_RANK + MLA_KV_RANK + MLA_ROPE
ROPE_THETA = 10000.0

N_EXPERT_GROUPS = 4
EXPERTS_PER_GROUP = 8
N_EXPERTS = N_EXPERT_GROUPS * EXPERTS_PER_GROUP
TOP_K = 2
D_EXPERT = 512
MOE_BLOCK = 256

DN_ALPHA = (2.0 * DEPTH) ** 0.25
LN_EPS = 1e-5
NORM_EPS = 1e-6

CHUNK = 64
MISC = 128
MISC_GLA_AD = 0
MISC_IG = 16
MISC_FG = 20
MISC_KR = 32
MISC_KR_ROT = 64
VMEM_LIMIT_BYTES = 56 * 1024 * 1024


def _cparams(n_axes):
    return pltpu.CompilerParams(dimension_semantics=("arbitrary",) * n_axes,
                                vmem_limit_bytes=VMEM_LIMIT_BYTES)


def _dot(a, b):
    return jnp.dot(a.astype(BF16), b.astype(BF16), preferred_element_type=F32)


def _dot_nt(a, b):
    return lax.dot_general(a.astype(BF16), b.astype(BF16), (((1,), (1,)), ((), ())),
                           preferred_element_type=F32)


def _dot_tn(a, b):
    return lax.dot_general(a.astype(BF16), b.astype(BF16), (((0,), (0,)), ((), ())),
                           preferred_element_type=F32)


def _split(x):
    hi = x.astype(BF16)
    lo = (x - hi.astype(F32)).astype(BF16)
    return hi, lo


def _dot_sel(x, sel):
    hi, lo = _split(x)
    return (jnp.dot(hi, sel, preferred_element_type=F32) + jnp.dot(lo, sel, preferred_element_type=F32))


def _sel_dot(sel, x):
    hi, lo = _split(x)
    return (jnp.dot(sel, hi, preferred_element_type=F32) + jnp.dot(sel, lo, preferred_element_type=F32))


def _dot_tn_sel(x, sel):
    hi, lo = _split(x)
    dn = (((0,), (0,)), ((), ()))
    return (lax.dot_general(hi, sel, dn, preferred_element_type=F32)
            + lax.dot_general(lo, sel, dn, preferred_element_type=F32))


def _iota(shape, dim):
    return lax.broadcasted_iota(jnp.int32, shape, dim)


def _shr(x, n):
    return lax.shift_right_logical(x, jnp.int32(n))


def _head_mask(rows, cols, row_shift, col_shift):
    return _shr(_iota((rows, cols), 0), row_shift) == _shr(_iota((rows, cols), 1), col_shift)


def _head_ones(rows, cols, row_shift, col_shift):
    return jnp.where(_head_mask(rows, cols, row_shift, col_shift), 1.0, 0.0).astype(BF16)


def _tril_ones(n):
    return jnp.where(_iota((n, n), 1) <= _iota((n, n), 0), 1.0, 0.0).astype(BF16)


def _bd(x, row_shift=6, col_shift=6):
    t = jnp.concatenate([x] * N_HEADS, axis=0)
    return jnp.where(_head_mask(t.shape[0], t.shape[1], row_shift, col_shift), t, 0.0)


def _lane_j(shape):
    return jnp.bitwise_and(_iota(shape, 1), CHUNK - 1)


def _sigmoid(x):
    return 1.0 / (1.0 + jnp.exp(-x))


def _softplus(x):
    return jnp.maximum(x, 0.0) + jnp.log(1.0 + jnp.exp(-jnp.abs(x)))


def _shift_rows(x, prev8, s, row):
    if s == 0:
        return x
    rp = pltpu.roll(prev8, s, 0)
    rp_t = jnp.concatenate([rp] * (x.shape[0] // 8), axis=0)
    return jnp.where(row < s, rp_t, pltpu.roll(x, s, 0))


def _proj_kernel(x_ref, *refs):
    n = len(refs) // 2
    xb = x_ref[...].astype(BF16)
    for w_ref, o_ref in zip(refs[:n], refs[n:]):
        o_ref[...] = jnp.dot(xb, w_ref[...], preferred_element_type=F32)


def _project(x2d, ws, tm):
    t_tok, d = x2d.shape
    return pl.pallas_call(
        _proj_kernel,
        grid=(t_tok // tm,),
        in_specs=[pl.BlockSpec((tm, d), lambda i: (i, 0))]
        + [pl.BlockSpec(w.shape, lambda i: (0, 0)) for w in ws],
        out_specs=[pl.BlockSpec((tm, w.shape[1]), lambda i: (i, 0)) for w in ws],
        out_shape=[jax.ShapeDtypeStruct((t_tok, w.shape[1]), F32) for w in ws],
        compiler_params=_cparams(1),
        name="proj",
    )(x2d, *ws)


def _rwkv_kernel(u_ref, mu_ref, w0_ref, wup_ref, a0_ref, aup_ref, gup_ref, kk_ref, ka_ref, rk_ref,
                 gng_ref, gnb_ref, o_ref,
                 st_ref, prev_ref, r_s, lw_s, k_s, v_s, a_s, b_s, g_s, y_s):
    ct = u_ref.shape[1]
    gw = GROUP_WIDTH

    @pl.when(pl.program_id(1) == 0)
    def _():
        st_ref[...] = jnp.zeros_like(st_ref)
        prev_ref[...] = jnp.zeros_like(prev_ref)

    u = u_ref[0]
    row = _iota(u.shape, 0)
    prev = jnp.where(row == 0, prev_ref[0:1, :], pltpu.roll(u, 1, 0))
    prev_ref[0:1, :] = u_ref[0, ct - 1:ct, :]
    us = u + (prev - u) * mu_ref[...]
    r = us[:, 0:gw]
    k = us[:, gw:2 * gw]
    v = us[:, 2 * gw:3 * gw]
    misc = us[:, 3 * gw:]
    lnl = -_softplus(-(w0_ref[...] + _dot(jnp.tanh(misc), wup_ref[...]))) - 0.5
    a = _sigmoid(a0_ref[...] + _dot(misc, aup_ref[...]))
    ones_bd = _head_ones(gw, gw, 6, 6)
    kkp = k * kk_ref[...]
    kk = kkp / jnp.maximum(jnp.sqrt(_dot_sel(kkp * kkp, ones_bd)), 1e-12)
    k2 = k * (1.0 + (a - 1.0) * ka_ref[...])
    r_s[...] = r
    lw_s[...] = -jnp.exp(lnl)
    k_s[...] = k2
    v_s[...] = v
    a_s[...] = -kk
    b_s[...] = kk * a
    g_s[...] = _dot(_sigmoid(misc), gup_ref[...])

    def chunk_body(ci, carry):
        sl = pl.ds(pl.multiple_of(ci * CHUNK, CHUNK), CHUNK)
        r_c, lw_c, k_c, v_c, a_c, b_c = r_s[sl, :], lw_s[sl, :], k_s[sl, :], v_s[sl, :], a_s[sl, :], b_s[sl, :]
        st = st_ref[...]
        cs = _sel_dot(_tril_ones(CHUNK), lw_c)
        g_incl = jnp.exp(cs)
        g_prev = jnp.exp(cs - lw_c)
        g_inv = jnp.exp(-cs)
        lhs = jnp.concatenate([a_c * g_prev, r_c * g_incl], axis=0)
        sb = _dot_nt(lhs, _bd(b_c * g_inv))
        sk = _dot_nt(lhs, _bd(k_c * g_inv))
        shape = (CHUNK, gw)
        i_idx, j_idx = _iota(shape, 0), _lane_j(shape)
        strict, incl = j_idx < i_idx, j_idx <= i_idx
        a_ab = jnp.where(strict, sb[:CHUNK], 0.0)
        a_rb = jnp.where(incl, sb[CHUNK:], 0.0)
        a_k = jnp.concatenate([jnp.where(strict, sk[:CHUNK], 0.0), jnp.where(incl, sk[CHUNK:], 0.0)], axis=0)
        p = jnp.where(j_idx == i_idx, 1.0, 0.0) + a_ab
        ap = a_ab
        for _ in range(5):
            ap = _dot(ap, _bd(ap))
            p = p + _dot(p, _bd(ap))
        from_state = _dot(lhs, st)
        from_v = _dot(a_k, _bd(v_c))
        uu = _dot(p, _bd(from_state[:CHUNK] + from_v[:CHUNK]))
        y_s[sl, :] = from_state[CHUNK:] + from_v[CHUNK:] + _dot(a_rb, _bd(uu))
        tail = jnp.exp(cs[CHUNK - 1:CHUNK, :] - cs)
        upd = _dot_tn(jnp.concatenate([b_c * tail, k_c * tail], axis=0), jnp.concatenate([uu, v_c], axis=0))
        g_col = jnp.exp(_dot_tn_sel(lw_c, jnp.ones((CHUNK, gw), BF16)))
        st_ref[...] = g_col * st + jnp.where(_head_mask(gw, gw, 6, 6), upd, 0.0)
        return carry

    lax.fori_loop(0, ct // CHUNK, chunk_body, 0)

    y = y_s[...]
    mean = _dot_sel(y, ones_bd) * (1.0 / HEAD_DIM)
    yc = y - mean
    var = _dot_sel(yc * yc, ones_bd) * (1.0 / HEAD_DIM)
    yn = yc * lax.rsqrt(var + RWKV_GN_EPS) * gng_ref[...] + gnb_ref[...]
    bonus = _dot_sel(r_s[...] * k_s[...] * rk_ref[...], ones_bd) * v_s[...]
    o_ref[0] = ((yn + bonus) * g_s[...]).astype(o_ref.dtype)


def _rwkv(u_a, prm, ct):
    bsz, s, _ = u_a.shape
    gw = GROUP_WIDTH
    vec = lambda n: pl.BlockSpec((1, n), lambda b, c: (0, 0))
    mat = lambda m, n: pl.BlockSpec((m, n), lambda b, c: (0, 0))
    tile = lambda n: pltpu.VMEM((ct, n), F32)
    return pl.pallas_call(
        _rwkv_kernel,
        grid=(bsz, s // ct),
        in_specs=[pl.BlockSpec((1, ct, D_IN_A), lambda b, c: (b, c, 0)), vec(D_IN_A), vec(gw), mat(MISC, gw),
                  vec(gw), mat(MISC, gw), mat(MISC, gw), vec(gw), vec(gw), vec(gw), vec(gw), vec(gw)],
        out_specs=pl.BlockSpec((1, ct, gw), lambda b, c: (b, c, 0)),
        out_shape=jax.ShapeDtypeStruct((bsz, s, gw), BF16),
        scratch_shapes=[pltpu.VMEM((gw, gw), F32), pltpu.VMEM((8, D_IN_A), F32)] + [tile(gw)] * 8,
        compiler_params=_cparams(2),
        name="rwkv7",
    )(u_a, *prm)


def _gla_kernel(u_ref, m_ref, aup_ref, ab_ref, ng_ref, o_ref,
                st_ref, q_s, k_s, b_s, v_s, oi_s, ox_s):
    ct = u_ref.shape[1]
    qw, gw = GLA_QK_WIDTH, GROUP_WIDTH

    @pl.when(pl.program_id(1) == 0)
    def _():
        st_ref[...] = jnp.zeros_like(st_ref)

    u = u_ref[0]
    q_s[...] = u[:, 0:qw] * (GLA_KEY_DIM ** -0.5)
    k_s[...] = u[:, qw:2 * qw]
    v_s[...] = u[:, 2 * qw:2 * qw + gw]
    gate = u[:, 2 * qw + gw:]
    log_a = -_softplus(-(_dot(m_ref[0], aup_ref[...]) + ab_ref[...])) * (1.0 / GLA_TAU)
    head_mask = _head_mask(qw, gw, 5, 6)

    b_s[...] = log_a

    def chunk_body(ci, carry):
        sl = pl.ds(pl.multiple_of(ci * CHUNK, CHUNK), CHUNK)
        la_c = b_s[sl, :]
        q_c, k_c, v_c = q_s[sl, :], k_s[sl, :], v_s[sl, :]
        st = st_ref[...]
        b = _sel_dot(_tril_ones(CHUNK), la_c)
        b_s[sl, :] = b
        ox_s[sl, :] = _dot(q_c * jnp.exp(b), st)
        tail = jnp.exp(b[CHUNK - 1:CHUNK, :] - b)
        g_col = jnp.exp(_dot_tn_sel(la_c, jnp.ones((CHUNK, gw), BF16)))
        st_ref[...] = g_col * st + jnp.where(head_mask, _dot_tn(k_c * tail, v_c), 0.0)
        return carry

    lax.fori_loop(0, ct // CHUNK, chunk_body, 0)

    seg_bcast = jnp.where(head_mask, 1.0, 0.0).astype(BF16)

    def row_body(i, carry):
        c0 = pl.multiple_of(lax.shift_left(_shr(i, 6), jnp.int32(6)), CHUNK)
        sl = pl.ds(c0, CHUNK)
        k_c, b_c, v_c = k_s[sl, :], b_s[sl, :], v_s[sl, :]
        q_i, b_i = q_s[pl.ds(i, 1), :], b_s[pl.ds(i, 1), :]
        valid = (_iota((CHUNK, qw), 0) + c0) <= i
        p = jnp.where(valid, k_c * q_i * jnp.exp(jnp.minimum(b_i - b_c, 0.0)), 0.0)
        sx = jnp.dot(p.astype(BF16), seg_bcast, preferred_element_type=F32)
        oi_s[pl.ds(i, 1), :] = jnp.sum(sx * v_c, axis=0, keepdims=True)
        return carry

    lax.fori_loop(0, ct, row_body, 0, unroll=8)

    o = oi_s[...] + ox_s[...]
    ms = _dot_sel(o * o, _head_ones(gw, gw, 6, 6)) * (1.0 / HEAD_DIM)
    on = o * lax.rsqrt(ms + NORM_EPS) * ng_ref[...]
    o_ref[0] = (on * (gate * _sigmoid(gate))).astype(o_ref.dtype)


def _gla(u_b, misc, prm, ct):
    bsz, s, wb = u_b.shape
    qw, gw = GLA_QK_WIDTH, GROUP_WIDTH
    return pl.pallas_call(
        _gla_kernel,
        grid=(bsz, s // ct),
        in_specs=[pl.BlockSpec((1, ct, wb), lambda b, c: (b, c, 0)),
                  pl.BlockSpec((1, ct, MISC), lambda b, c: (b, c, 0)),
                  pl.BlockSpec((MISC, qw), lambda b, c: (0, 0)),
                  pl.BlockSpec((1, qw), lambda b, c: (0, 0)),
                  pl.BlockSpec((1, gw), lambda b, c: (0, 0))],
        out_specs=pl.BlockSpec((1, ct, gw), lambda b, c: (b, c, 0)),
        out_shape=jax.ShapeDtypeStruct((bsz, s, gw), BF16),
        scratch_shapes=[pltpu.VMEM((qw, gw), F32), pltpu.VMEM((ct, qw), F32), pltpu.VMEM((ct, qw), F32),
                        pltpu.VMEM((ct, qw), F32), pltpu.VMEM((ct, gw), F32), pltpu.VMEM((ct, gw), F32),
                        pltpu.VMEM((ct, gw), F32)],
        compiler_params=_cparams(2),
        name="gla",
    )(u_b, misc, *prm)


def _mlstm_kernel(u_ref, m_ref, cw_ref, cb_ref, eig_ref, efg_ref, ib_ref, fb_ref, ng_ref, o_ref,
                  mem_ref, nb_ref, mst_ref, prev_ref, q_s, k_s, v_s, li_s, lf_s, h_s):
    ct = u_ref.shape[1]
    gw = GROUP_WIDTH

    @pl.when(pl.program_id(1) == 0)
    def _():
        mem_ref[...] = jnp.zeros_like(mem_ref)
        nb_ref[...] = jnp.zeros_like(nb_ref)
        mst_ref[...] = jnp.zeros_like(mst_ref)
        prev_ref[...] = jnp.zeros_like(prev_ref)

    u = u_ref[0]
    qk = u[:, 0:2 * gw]
    row = _iota(qk.shape, 0)
    prev8 = prev_ref[...]
    conv = cb_ref[...] + jnp.zeros_like(qk)
    for j in range(MLSTM_CONV):
        conv = conv + cw_ref[j:j + 1, :] * _shift_rows(qk, prev8, MLSTM_CONV - 1 - j, row)
    prev_ref[...] = u_ref[0, ct - 8:ct, 0:2 * gw]
    qk_act = conv * _sigmoid(conv)
    q_s[...] = qk_act[:, 0:gw]
    k_s[...] = qk_act[:, gw:] * (HEAD_DIM ** -0.5)
    v_s[...] = u[:, 2 * gw:3 * gw]
    o_gate = u[:, 3 * gw:]
    misc = m_ref[0]
    li_s[...] = _dot_sel(misc, eig_ref[...]) + ib_ref[...]
    lf_s[...] = -_softplus(-(_dot_sel(misc, efg_ref[...]) + fb_ref[...]))
    neg_inf = -jnp.inf

    def chunk_body(ci, carry):
        sl = pl.ds(pl.multiple_of(ci * CHUNK, CHUNK), CHUNK)
        q_c, k_c, v_c, li_c, lf_c = q_s[sl, :], k_s[sl, :], v_s[sl, :], li_s[sl, :], lf_s[sl, :]
        mem, nb, m_e = mem_ref[...], nb_ref[...], mst_ref[0:1, :]
        shape = (CHUNK, gw)
        i_idx, j_idx = _iota(shape, 0), _lane_j(shape)
        incl = j_idx <= i_idx
        head_mask = _head_mask(gw, gw, 6, 6)
        fc = _sel_dot(_tril_ones(CHUNK), lf_c)
        g_row = li_c - fc
        ones_avg = jnp.full(shape, 1.0 / HEAD_DIM, BF16)
        g_hi, g_lo = _split(_bd(g_row))
        dn = (((1,), (1,)), ((), ()))
        g_t = (lax.dot_general(ones_avg, g_hi, dn, preferred_element_type=F32)
               + lax.dot_general(ones_avg, g_lo, dn, preferred_element_type=F32))
        log_w = jnp.where(incl, fc + g_t, neg_inf)
        cm = g_row
        for sh in (1, 2, 4, 8, 16, 32):
            cm = jnp.maximum(cm, jnp.where(i_idx < sh, neg_inf, pltpu.roll(cm, sh, 0)))
        log_carry = fc + m_e
        m_row = jnp.maximum(fc + cm, log_carry)
        sc = _dot_nt(q_c, _bd(k_c)) * jnp.exp(log_w - m_row)
        w_c = jnp.exp(log_carry - m_row)
        scb = sc.astype(BF16)
        num = jnp.dot(scb, _bd(v_c).astype(BF16), preferred_element_type=F32) + w_c * _dot(q_c, mem)
        den = (jnp.dot(scb, _head_ones(gw, gw, 6, 6), preferred_element_type=F32) + w_c * _dot(q_c, nb))
        h_s[sl, :] = num / jnp.maximum(jnp.abs(den), jnp.exp(-m_row))
        f_end = fc[CHUNK - 1:CHUNK, :]
        log_kv = f_end - fc + li_c
        m_new = jnp.maximum(f_end + m_e, jnp.max(log_kv, axis=0, keepdims=True))
        kwk = jnp.exp(log_kv - m_new) * k_c
        cd = jnp.exp(f_end + m_e - m_new)
        cd_rows = jnp.where(_iota((16, gw), 0) == 0, cd, 0.0)
        cd_col = _dot_tn_sel(cd_rows, jnp.ones((16, gw), BF16))
        upd = _dot_tn(kwk, jnp.concatenate([v_c, jnp.ones(shape, F32)], axis=1))
        mem_ref[...] = cd_col * mem + jnp.where(head_mask, upd[:, :gw], 0.0)
        nb_ref[...] = cd_col * nb + jnp.where(head_mask, upd[:, gw:], 0.0)
        mst_ref[0:1, :] = m_new
        return carry

    lax.fori_loop(0, ct // CHUNK, chunk_body, 0)

    h = h_s[...]
    ones_bd = _head_ones(gw, gw, 6, 6)
    mean = _dot_sel(h, ones_bd) * (1.0 / HEAD_DIM)
    hc = h - mean
    var = _dot_sel(hc * hc, ones_bd) * (1.0 / HEAD_DIM)
    hn = hc * lax.rsqrt(var + LN_EPS) * ng_ref[...]
    o_ref[0] = (hn * _sigmoid(o_gate)).astype(o_ref.dtype)


def _mlstm(u_c, misc, prm, ct):
    bsz, s, wc = u_c.shape
    gw = GROUP_WIDTH
    vec = lambda n: pl.BlockSpec((1, n), lambda b, c: (0, 0))
    mat = lambda m, n: pl.BlockSpec((m, n), lambda b, c: (0, 0))
    tile = pltpu.VMEM((ct, gw), F32)
    return pl.pallas_call(
        _mlstm_kernel,
        grid=(bsz, s // ct),
        in_specs=[pl.BlockSpec((1, ct, wc), lambda b, c: (b, c, 0)),
                  pl.BlockSpec((1, ct, MISC), lambda b, c: (b, c, 0)),
                  mat(MLSTM_CONV, 2 * gw), vec(2 * gw), mat(MISC, gw), mat(MISC, gw), vec(gw), vec(gw), vec(gw)],
        out_specs=pl.BlockSpec((1, ct, gw), lambda b, c: (b, c, 0)),
        out_shape=jax.ShapeDtypeStruct((bsz, s, gw), BF16),
        scratch_shapes=[pltpu.VMEM((gw, gw), F32), pltpu.VMEM((gw, gw), F32), pltpu.VMEM((8, gw), F32),
                        pltpu.VMEM((8, 2 * gw), F32)] + [tile] * 6,
        compiler_params=_cparams(2),
        name="mlstm",
    )(u_c, misc, *prm)


def _mla_prep_kernel(u_ref, m_ref, cq_ref, sq_ref, csk_ref, qg_ref, kvg_ref, wq_ref, wqr_ref, wk_ref, wv_ref,
                     selk_ref, q_ref, k_ref, v_ref):
    u = u_ref[0]
    cq = u[:, 0:MLA_Q_RANK]
    ckv = u[:, MLA_Q_RANK:]
    cqn = cq * lax.rsqrt(jnp.mean(cq * cq, axis=-1, keepdims=True) + NORM_EPS) * qg_ref[...]
    ckvn = ckv * lax.rsqrt(jnp.mean(ckv * ckv, axis=-1, keepdims=True) + NORM_EPS) * kvg_ref[...]
    cqb, ckvb = cqn.astype(BF16), ckvn.astype(BF16)
    k_rope = _dot(m_ref[0] * csk_ref[0], selk_ref[...])
    cos_q, sin_q = cq_ref[0], sq_ref[0]
    scale = MLA_QK ** -0.5
    for h in range(N_HEADS):
        qh = (jnp.dot(cqb, wq_ref[h], preferred_element_type=F32) * cos_q
              + jnp.dot(cqb, wqr_ref[h], preferred_element_type=F32) * sin_q) * scale
        q_ref[0, h] = qh.astype(BF16)
        k_ref[0, h] = (jnp.dot(ckvb, wk_ref[h], preferred_element_type=F32) + k_rope).astype(BF16)
        v_ref[0, h] = jnp.dot(ckvb, wv_ref[h], preferred_element_type=F32).astype(BF16)


def _mla_prep(u_d, misc, tabs, prm, tm):
    bsz, s, wd = u_d.shape
    full = lambda a: pl.BlockSpec(a.shape, lambda b, c: (0,) * a.ndim)
    tok = lambda n: pl.BlockSpec((1, tm, n), lambda b, c: (b, c, 0))
    head = lambda n: pl.BlockSpec((1, N_HEADS, tm, n), lambda b, c: (b, 0, c, 0))
    return pl.pallas_call(
        _mla_prep_kernel,
        grid=(bsz, s // tm),
        in_specs=[tok(wd), tok(MISC), tok(MLA_QK), tok(MLA_QK), tok(MISC)] + [full(a) for a in prm],
        out_specs=[head(MLA_QK), head(MLA_QK), head(HEAD_DIM)],
        out_shape=[jax.ShapeDtypeStruct((bsz, N_HEADS, s, MLA_QK), BF16),
                   jax.ShapeDtypeStruct((bsz, N_HEADS, s, MLA_QK), BF16),
                   jax.ShapeDtypeStruct((bsz, N_HEADS, s, HEAD_DIM), BF16)],
        compiler_params=_cparams(2),
        name="mla_prep",
    )(u_d, misc, *tabs, *prm)


def _attn_kernel(q_ref, k_ref, v_ref, o_ref):
    tq = q_ref.shape[2]
    qi = pl.program_id(2)
    q = q_ref[0, 0]

    def step(j, carry, diagonal):
        m, l, acc = carry
        sl = pl.ds(pl.multiple_of(j * tq, tq), tq)
        kb, vb = k_ref[0, 0, sl, :], v_ref[0, 0, sl, :]
        s = lax.dot_general(q, kb, (((1,), (1,)), ((), ())), preferred_element_type=F32)
        if diagonal:
            s = jnp.where(_iota((tq, tq), 1) <= _iota((tq, tq), 0), s, -jnp.inf)
        m_new = jnp.maximum(m, jnp.max(s, axis=-1, keepdims=True))
        p = jnp.exp(s - m_new)
        alpha = jnp.exp(m - m_new)
        l = alpha * l + jnp.sum(p, axis=-1, keepdims=True)
        acc = alpha * acc + jnp.dot(p.astype(BF16), vb, preferred_element_type=F32)
        return m_new, l, acc

    init = (jnp.full((tq, 1), -jnp.inf, F32), jnp.zeros((tq, 1), F32), jnp.zeros((tq, HEAD_DIM), F32))
    carry = lax.fori_loop(0, qi, lambda j, c: step(j, c, False), init)
    _, l, acc = step(qi, carry, True)
    o_ref[0, 0] = (acc / l).astype(o_ref.dtype)


def _attention(q, k, v, tq):
    bsz, nh, s, _ = q.shape
    return pl.pallas_call(
        _attn_kernel,
        grid=(bsz, nh, s // tq),
        in_specs=[pl.BlockSpec((1, 1, tq, MLA_QK), lambda b, h, i: (b, h, i, 0)),
                  pl.BlockSpec((1, 1, s, MLA_QK), lambda b, h, i: (b, h, 0, 0)),
                  pl.BlockSpec((1, 1, s, HEAD_DIM), lambda b, h, i: (b, h, 0, 0))],
        out_specs=pl.BlockSpec((1, 1, tq, HEAD_DIM), lambda b, h, i: (b, h, i, 0)),
        out_shape=jax.ShapeDtypeStruct((bsz, nh, s, HEAD_DIM), BF16),
        compiler_params=_cparams(3),
        name="mla_attn",
    )(q, k, v)


def _layer_norm(z, g, b):
    zc = z - jnp.mean(z, axis=-1, keepdims=True)
    return zc * lax.rsqrt(jnp.mean(zc * zc, axis=-1, keepdims=True) + LN_EPS) * g + b


def _outproj_kernel(x_ref, ya_ref, yb_ref, yc_ref, yd_ref, wo_ref, wod_ref, g_ref, b_ref, o_ref):
    gw = GROUP_WIDTH
    mix = jnp.dot(ya_ref[0], wo_ref[0:gw, :], preferred_element_type=F32)
    mix = mix + jnp.dot(yb_ref[0], wo_ref[gw:2 * gw, :], preferred_element_type=F32)
    mix = mix + jnp.dot(yc_ref[0], wo_ref[2 * gw:3 * gw, :], preferred_element_type=F32)
    for h in range(N_HEADS):
        mix = mix + jnp.dot(yd_ref[0, h], wod_ref[h], preferred_element_type=F32)
    o_ref[0] = _layer_norm(DN_ALPHA * x_ref[0] + mix, g_ref[...], b_ref[...])


def _outproj(x, ya, yb, yc, yd, wo, wod, g, b, tm):
    bsz, s, d = x.shape
    gw = GROUP_WIDTH
    tok = lambda n: pl.BlockSpec((1, tm, n), lambda bb, c: (bb, c, 0))
    full = lambda a: pl.BlockSpec(a.shape, lambda bb, c: (0,) * a.ndim)
    return pl.pallas_call(
        _outproj_kernel,
        grid=(bsz, s // tm),
        in_specs=[tok(d), tok(gw), tok(gw), tok(gw),
                  pl.BlockSpec((1, N_HEADS, tm, HEAD_DIM), lambda bb, c: (bb, 0, c, 0)),
                  full(wo), full(wod), full(g), full(b)],
        out_specs=tok(d),
        out_shape=jax.ShapeDtypeStruct((bsz, s, d), F32),
        compiler_params=_cparams(2),
        name="outproj_ln1",
    )(x, ya, yb, yc, yd, wo, wod, g, b)


def _router_kernel(x_ref, w_ref, b_ref, idx_ref, gate_ref):
    x = x_ref[...]
    xh, xl = _split(x)
    wh, wl = _split(w_ref[...])
    dn = (((1,), (1,)), ((), ()))
    logits = (lax.dot_general(wh, xh, dn, preferred_element_type=F32)
              + lax.dot_general(wh, xl, dn, preferred_element_type=F32)
              + lax.dot_general(wl, xh, dn, preferred_element_type=F32)) + b_ref[...]
    tm = x.shape[0]
    ng, ne = N_EXPERT_GROUPS, EXPERTS_PER_GROUP
    gl = [logits[i:i + 1, :] for i in range(ng)]
    gmax = functools.reduce(jnp.maximum, gl)
    gsum = functools.reduce(jnp.add, [jnp.exp(g - gmax) for g in gl])
    group_p = 1.0 / gsum
    gidx = jnp.full((1, tm), ng - 1, jnp.int32)
    for i in range(ng - 2, -1, -1):
        gidx = jnp.where(gl[i] == gmax, i, gidx)
    el = logits[8 + (ng - 1) * ne:8 + ng * ne, :]
    for i in range(ng - 2, -1, -1):
        el = jnp.where(gidx == i, logits[8 + i * ne:8 + (i + 1) * ne, :], el)
    rows = _iota((ne, tm), 0)
    m1 = jnp.max(el, axis=0, keepdims=True)
    i1 = jnp.min(jnp.where(el == m1, rows, ne), axis=0, keepdims=True)
    el2 = jnp.where(rows == i1, -jnp.inf, el)
    m2 = jnp.max(el2, axis=0, keepdims=True)
    i2 = jnp.min(jnp.where(el2 == m2, rows, ne), axis=0, keepdims=True)
    z = jnp.sum(jnp.exp(el - m1), axis=0, keepdims=True)
    p1 = 1.0 / z
    p2 = jnp.exp(m2 - m1) / z
    base = gidx * ne
    idx_ref[...] = jnp.concatenate([base + i1, base + i2], axis=0)
    gate_ref[...] = jnp.concatenate([group_p * p1 / (p1 + p2), group_p * p2 / (p1 + p2)], axis=0)


def _router(x2d, w_t, b_col, tm):
    t_tok, d = x2d.shape
    return pl.pallas_call(
        _router_kernel,
        grid=(t_tok // tm,),
        in_specs=[pl.BlockSpec((tm, d), lambda i: (i, 0)), pl.BlockSpec(w_t.shape, lambda i: (0, 0)),
                  pl.BlockSpec(b_col.shape, lambda i: (0, 0))],
        out_specs=[pl.BlockSpec((TOP_K, tm), lambda i: (0, i)), pl.BlockSpec((TOP_K, tm), lambda i: (0, i))],
        out_shape=[jax.ShapeDtypeStruct((TOP_K, t_tok), jnp.int32), jax.ShapeDtypeStruct((TOP_K, t_tok), F32)],
        compiler_params=_cparams(1),
        name="router",
    )(x2d, w_t, b_col)


def _moe_kernel(be_ref, x_ref, gate_ref, wg_ref, wu_ref, wd_ref, o_ref, wg_s, wu_s, wd_s):
    i = pl.program_id(0)

    @pl.when(jnp.logical_or(i == 0, be_ref[i] != be_ref[jnp.maximum(i - 1, 0)]))
    def _():
        wg_s[...] = wg_ref[0].astype(BF16)
        wu_s[...] = wu_ref[0].astype(BF16)
        wd_s[...] = wd_ref[0].astype(BF16)

    xb = x_ref[...]
    hg = jnp.dot(xb, wg_s[...], preferred_element_type=F32)
    hu = jnp.dot(xb, wu_s[...], preferred_element_type=F32)
    hid = (hg * _sigmoid(hg)) * hu
    y = jnp.dot(hid.astype(BF16), wd_s[...], preferred_element_type=F32) * gate_ref[...]
    o_ref[...] = y.astype(o_ref.dtype)


def _moe(x_rows, block_expert, row_gate, wg, wu, wd, layer):
    n_rows, d = x_rows.shape
    n_blocks = n_rows // MOE_BLOCK
    grid_spec = pltpu.PrefetchScalarGridSpec(
        num_scalar_prefetch=1,
        grid=(n_blocks,),
        in_specs=[pl.BlockSpec((MOE_BLOCK, d), lambda i, be: (i, 0)),
                  pl.BlockSpec((MOE_BLOCK, 1), lambda i, be: (i, 0)),
                  pl.BlockSpec((None, 1, d, D_EXPERT), lambda i, be: (layer, be[i], 0, 0)),
                  pl.BlockSpec((None, 1, d, D_EXPERT), lambda i, be: (layer, be[i], 0, 0)),
                  pl.BlockSpec((None, 1, D_EXPERT, d), lambda i, be: (layer, be[i], 0, 0))],
        out_specs=pl.BlockSpec((MOE_BLOCK, d), lambda i, be: (i, 0)),
        scratch_shapes=[pltpu.VMEM((d, D_EXPERT), BF16), pltpu.VMEM((d, D_EXPERT), BF16),
                        pltpu.VMEM((D_EXPERT, d), BF16)],
    )
    return pl.pallas_call(
        _moe_kernel,
        grid_spec=grid_spec,
        out_shape=jax.ShapeDtypeStruct((n_rows, d), BF16),
        compiler_params=_cparams(1),
        name="moe_experts",
    )(block_expert, x_rows, row_gate.reshape(n_rows, 1), wg, wu, wd)


def _dispatch(idx, gate, t_tok):
    n_asg = t_tok * TOP_K
    flat_e = idx.reshape(n_asg)
    flat_g = gate.reshape(n_asg)
    order = jnp.argsort(flat_e)
    inverse = jnp.argsort(order).astype(jnp.int32)
    starts = jnp.searchsorted(flat_e[order], jnp.arange(N_EXPERTS + 1, dtype=jnp.int32), side='left')
    starts = starts.astype(jnp.int32)
    counts = starts[1:] - starts[:-1]
    starts = starts[:-1]
    padded = (counts + MOE_BLOCK - 1) // MOE_BLOCK * MOE_BLOCK
    pad_ends = jnp.cumsum(padded)
    pad_starts = pad_ends - padded
    n_rows = -(-(n_asg + N_EXPERTS * (MOE_BLOCK - 1)) // MOE_BLOCK) * MOE_BLOCK
    n_blocks = n_rows // MOE_BLOCK
    block_expert = jnp.minimum(jnp.searchsorted(pad_ends, jnp.arange(n_blocks) * MOE_BLOCK, side='right'),
                               N_EXPERTS - 1).astype(jnp.int32)
    pos = pad_starts[flat_e] + inverse - starts[flat_e]
    row_e = jnp.repeat(block_expert, MOE_BLOCK)
    row_rank = jnp.arange(n_rows, dtype=jnp.int32) - pad_starts[row_e]
    valid = row_rank < counts[row_e]
    asg = order[jnp.where(valid, starts[row_e] + row_rank, 0)]
    row_tok = jnp.where(valid, asg % t_tok, 0).astype(jnp.int32)
    row_gate = jnp.where(valid, flat_g[asg], 0.0)
    return block_expert, row_tok, pos.reshape(TOP_K, t_tok), row_gate


def _final_kernel(x_ref, p_ref, f0_ref, f1_ref, wg_ref, bg_ref, wp_ref, g_ref, b_ref, o_ref):
    x = x_ref[...]
    gate = _sigmoid(_dot(x, wg_ref[...]) + bg_ref[...])
    ple = gate * _dot(p_ref[...], wp_ref[...])
    ffn = f0_ref[...].astype(F32) + f1_ref[...].astype(F32)
    o_ref[...] = _layer_norm(DN_ALPHA * x + ffn + ple, g_ref[...], b_ref[...])


def _final(x2d, p2d, ffn0, ffn1, wg, bg, wp, g, b, tm):
    t_tok, d = x2d.shape
    full = lambda a: pl.BlockSpec(a.shape, lambda i: (0,) * a.ndim)
    return pl.pallas_call(
        _final_kernel,
        grid=(t_tok // tm,),
        in_specs=[pl.BlockSpec((tm, d), lambda i: (i, 0)), pl.BlockSpec((tm, p2d.shape[1]), lambda i: (i, 0)),
                  pl.BlockSpec((tm, d), lambda i: (i, 0)), pl.BlockSpec((tm, d), lambda i: (i, 0)),
                  full(wg), full(bg), full(wp), full(g), full(b)],
        out_specs=pl.BlockSpec((tm, d), lambda i: (i, 0)),
        out_shape=jax.ShapeDtypeStruct((t_tok, d), F32),
        compiler_params=_cparams(1),
        name="ple_ln2",
    )(x2d, p2d, ffn0, ffn1, wg, bg, wp, g, b)


def _pad_rows(w, offset, total=MISC):
    return jnp.zeros((total, w.shape[1]), w.dtype).at[offset:offset + w.shape[0]].set(w)


def _rot_cols(w):
    half = w.shape[1] // 2
    return jnp.concatenate([-w[:, half:], w[:, :half]], axis=1)


def _row(v):
    return v.reshape(1, -1)


def _expand_heads(v):
    return jnp.repeat(v, HEAD_DIM).reshape(1, GROUP_WIDTH)


def _head_select(offset):
    sel = np.zeros((MISC, GROUP_WIDTH), np.float32)
    for h in range(N_HEADS):
        sel[offset + h, h * HEAD_DIM:(h + 1) * HEAD_DIM] = 1.0
    return jnp.asarray(sel, BF16)


def _rope_tables(positions):
    inv_freq = ROPE_THETA ** (-jnp.arange(0, MLA_ROPE, 2, dtype=F32) / MLA_ROPE)
    ang = positions.astype(F32)[..., None] * inv_freq
    cos, sin = jnp.cos(ang), jnp.sin(ang)
    cos2 = jnp.concatenate([cos, cos], axis=-1)
    sin2 = jnp.concatenate([sin, sin], axis=-1)
    lead = cos.shape[:-1]
    cos_q = jnp.concatenate([jnp.ones(lead + (MLA_NOPE,), F32), cos2], axis=-1)
    sin_q = jnp.concatenate([jnp.zeros(lead + (MLA_NOPE,), F32), sin2], axis=-1)
    cs_k = jnp.concatenate([jnp.zeros(lead + (MISC_KR,), F32), cos2, sin2,
                            jnp.zeros(lead + (MISC - MISC_KR_ROT - MLA_ROPE,), F32)], axis=-1)
    return cos_q, sin_q, cs_k


def _rope_key_select():
    sel = np.zeros((MISC, MLA_QK), np.float32)
    for c in range(MLA_ROPE):
        sel[MISC_KR + c, MLA_NOPE + c] = 1.0
        sel[MISC_KR_ROT + c, MLA_NOPE + c] = 1.0
    return jnp.asarray(sel, BF16)


def _split_w_in(w):
    a0, b0 = 0, D_IN_A
    c0, d0 = b0 + D_IN_B, b0 + D_IN_B + D_IN_C
    qw, gw = GLA_QK_WIDTH, GROUP_WIDTH
    w_a = w[:, a0:b0]
    gla_ad = w[:, b0 + 2 * qw + gw:b0 + 2 * qw + gw + GLA_GATE_RANK]
    w_b = jnp.concatenate([w[:, b0:b0 + 2 * qw + gw], w[:, b0 + 2 * qw + gw + GLA_GATE_RANK:c0]], axis=1)
    w_c = w[:, c0:c0 + 4 * gw]
    gates = w[:, c0 + 4 * gw:d0]
    w_d = w[:, d0:d0 + MLA_Q_RANK + MLA_KV_RANK]
    kr = w[:, d0 + MLA_Q_RANK + MLA_KV_RANK:]
    z = lambda n: jnp.zeros((w.shape[0], n), w.dtype)
    w_m = jnp.concatenate([gla_ad, gates, z(MISC_KR - MISC_FG - N_HEADS), kr, _rot_cols(kr),
                           z(MISC - MISC_KR_ROT - MLA_ROPE)], axis=1)
    return [t.astype(BF16) for t in (w_a, w_b, w_c, w_d, w_m)]


def _tiles(s):
    return min(512, s), min(512, s)


def kernel(x, p, positions, w_in, rwkv_mu, rwkv_w0, rwkv_w_up, rwkv_a0, rwkv_a_up, rwkv_g_up, rwkv_k_k, rwkv_k_a, rwkv_r_k, rwkv_gn_g, rwkv_gn_b, gla_alpha_up, gla_alpha_b, gla_norm_g, mlstm_conv_w, mlstm_conv_b, mlstm_i_b, mlstm_f_b, mlstm_norm_g, mla_q_norm_g, mla_w_uq, mla_kv_norm_g, mla_w_ukv, w_out, ln1_g, ln1_b, moe_w_rg, moe_b_rg, moe_w_re, moe_b_re, moe_w_gate, moe_w_up, moe_w_down, ple_w_gate, ple_b_gate, ple_w, ln2_g, ln2_b):
    bsz, s, d = x.shape
    t_tok = bsz * s
    depth = w_in.shape[0]
    ct, tq = _tiles(s)
    tm = min(512, t_tok)
    rope_tabs = _rope_tables(positions)
    sel_ig, sel_fg, sel_k = _head_select(MISC_IG), _head_select(MISC_FG), _rope_key_select()
    for i in range(depth):
        u_a, u_b, u_c, u_d, u_m = _project(x.reshape(t_tok, d), _split_w_in(w_in[i]), tm)
        u_a, u_b, u_c, u_d, u_m = (t.reshape(bsz, s, -1) for t in (u_a, u_b, u_c, u_d, u_m))

        rk = RWKV_DECAY_RANK
        ya = _rwkv(u_a, (_row(rwkv_mu[i]), _row(rwkv_w0[i]), _pad_rows(rwkv_w_up[i], 0).astype(BF16),
                         _row(rwkv_a0[i]), _pad_rows(rwkv_a_up[i], rk).astype(BF16),
                         _pad_rows(rwkv_g_up[i], rk + RWKV_ICL_RANK).astype(BF16),
                         _row(rwkv_k_k[i]), _row(rwkv_k_a[i]), _row(rwkv_r_k[i]),
                         _row(rwkv_gn_g[i]), _row(rwkv_gn_b[i])), ct)
        yb = _gla(u_b, u_m, (_pad_rows(gla_alpha_up[i], MISC_GLA_AD).astype(BF16), _row(gla_alpha_b[i]),
                             _row(gla_norm_g[i])), ct)
        yc = _mlstm(u_c, u_m, (mlstm_conv_w[i], _row(mlstm_conv_b[i]), sel_ig, sel_fg,
                               _expand_heads(mlstm_i_b[i]), _expand_heads(mlstm_f_b[i]),
                               _row(mlstm_norm_g[i])), ct)

        wq = mla_w_uq[i].reshape(MLA_Q_RANK, N_HEADS, MLA_QK).transpose(1, 0, 2)
        wq_rot = jnp.concatenate([jnp.zeros((N_HEADS, MLA_Q_RANK, MLA_NOPE), F32),
                                  jax.vmap(_rot_cols)(wq[:, :, MLA_NOPE:])], axis=2)
        wkv = mla_w_ukv[i].reshape(MLA_KV_RANK, N_HEADS, MLA_NOPE + HEAD_DIM).transpose(1, 0, 2)
        wk = jnp.concatenate([wkv[:, :, :MLA_NOPE], jnp.zeros((N_HEADS, MLA_KV_RANK, MLA_ROPE), F32)], axis=2)
        q, k, v = _mla_prep(u_d, u_m, rope_tabs,
                            (_row(mla_q_norm_g[i]), _row(mla_kv_norm_g[i]), wq.astype(BF16), wq_rot.astype(BF16),
                             wk.astype(BF16), wkv[:, :, MLA_NOPE:].astype(BF16), sel_k), ct)
        yd = _attention(q, k, v, tq)

        wo = w_out[i].astype(BF16)
        x = _outproj(x, ya, yb, yc, yd, wo[:3 * GROUP_WIDTH], wo[3 * GROUP_WIDTH:].reshape(N_HEADS, HEAD_DIM, d),
                     _row(ln1_g[i]), _row(ln1_b[i]), ct)

        x2d = x.reshape(t_tok, d)
        w_route = jnp.concatenate([moe_w_rg[i].T, jnp.zeros((8 - N_EXPERT_GROUPS, d), F32), moe_w_re[i].T], axis=0)
        b_route = jnp.concatenate([moe_b_rg[i], jnp.zeros((8 - N_EXPERT_GROUPS,), F32), moe_b_re[i]]).reshape(-1, 1)
        idx, gate = _router(x2d, w_route, b_route, tm)
        block_expert, row_tok, pos, row_gate = _dispatch(idx, gate, t_tok)
        y_rows = _moe(x2d.astype(BF16)[row_tok], block_expert, row_gate, moe_w_gate, moe_w_up, moe_w_down, i)
        x = _final(x2d, p[i].reshape(t_tok, -1), y_rows[pos[0]], y_rows[pos[1]], ple_w_gate[i].astype(BF16),
                   _row(ple_b_gate[i]), ple_w[i].astype(BF16), _row(ln2_g[i]), _row(ln2_b[i]), tm).reshape(bsz, s, d)
    return x
```

```python
import functools

import jax
import jax.numpy as jnp
import numpy as np
from jax import lax
from jax.experimental import pallas as pl
from jax.experimental.pallas import tpu as pltpu

F32 = jnp.float32
BF16 = jnp.bfloat16

D_MODEL = 1024
DEPTH = 4
HEAD_DIM = 64
N_HEADS = 4
GROUP_WIDTH = N_HEADS * HEAD_DIM
D_PLE = 256

RWKV_DECAY_RANK = 32
RWKV_ICL_RANK = 32
RWKV_GATE_RANK = 64
RWKV_GN_EPS = 64e-5
D_IN_A = 3 * GROUP_WIDTH + RWKV_DECAY_RANK + RWKV_ICL_RANK + RWKV_GATE_RANK

GLA_KEY_DIM = 32
GLA_QK_WIDTH = N_HEADS * GLA_KEY_DIM
GLA_GATE_RANK = 16
GLA_TAU = 16.0
D_IN_B = 2 * GLA_QK_WIDTH + 2 * GROUP_WIDTH + GLA_GATE_RANK

MLSTM_CONV = 4
D_IN_C = 4 * GROUP_WIDTH + 2 * N_HEADS

MLA_Q_RANK = 256
MLA_KV_RANK = 128
MLA_NOPE = 64
MLA_ROPE = 32
MLA_QK = MLA_NOPE + MLA_ROPE
D_IN_D = MLA_Q_RANK + MLA_KV_RANK + MLA_ROPE
ROPE_THETA = 10000.0

N_EXPERT_GROUPS = 4
EXPERTS_PER_GROUP = 8
N_EXPERTS = N_EXPERT_GROUPS * EXPERTS_PER_GROUP
TOP_K = 2
D_EXPERT = 512
MOE_BLOCK = 256

DN_ALPHA = (2.0 * DEPTH) ** 0.25
LN_EPS = 1e-5
NORM_EPS = 1e-6

CHUNK = 64
GLA_SUB = 16
MISC = 128
MISC_GLA_AD = 0
MISC_IG = 16
MISC_FG = 20
MISC_KR = 32
MISC_KR_ROT = 64
VMEM_LIMIT_BYTES = 56 * 1024 * 1024


def _cparams(n_axes):
    return pltpu.CompilerParams(dimension_semantics=("arbitrary",) * n_axes,
                                vmem_limit_bytes=VMEM_LIMIT_BYTES)


def _dot(a, b):
    return jnp.dot(a.astype(BF16), b.astype(BF16), preferred_element_type=F32)


def _dot_nt(a, b):
    return lax.dot_general(a.astype(BF16), b.astype(BF16), (((1,), (1,)), ((), ())),
                           preferred_element_type=F32)


def _dot_tn(a, b):
    return lax.dot_general(a.astype(BF16), b.astype(BF16), (((0,), (0,)), ((), ())),
                           preferred_element_type=F32)


def _split(x):
    hi = x.astype(BF16)
    lo = (x - hi.astype(F32)).astype(BF16)
    return hi, lo


def _dot_sel(x, sel):
    hi, lo = _split(x)
    return (jnp.dot(hi, sel, preferred_element_type=F32) + jnp.dot(lo, sel, preferred_element_type=F32))


def _sel_dot(sel, x):
    hi, lo = _split(x)
    return (jnp.dot(sel, hi, preferred_element_type=F32) + jnp.dot(sel, lo, preferred_element_type=F32))


def _dot_tn_sel(x, sel):
    hi, lo = _split(x)
    dn = (((0,), (0,)), ((), ()))
    return (lax.dot_general(hi, sel, dn, preferred_element_type=F32)
            + lax.dot_general(lo, sel, dn, preferred_element_type=F32))


def _iota(shape, dim):
    return lax.broadcasted_iota(jnp.int32, shape, dim)


def _shr(x, n):
    return lax.shift_right_logical(x, jnp.int32(n))


def _head_mask(rows, cols, row_shift, col_shift):
    return _shr(_iota((rows, cols), 0), row_shift) == _shr(_iota((rows, cols), 1), col_shift)


def _head_ones(rows, cols, row_shift, col_shift):
    return jnp.where(_head_mask(rows, cols, row_shift, col_shift), 1.0, 0.0).astype(BF16)


def _tril_ones(n):
    return jnp.where(_iota((n, n), 1) <= _iota((n, n), 0), 1.0, 0.0).astype(BF16)


def _bd(x, row_shift=6, col_shift=6):
    t = jnp.concatenate([x] * N_HEADS, axis=0)
    return jnp.where(_head_mask(t.shape[0], t.shape[1], row_shift, col_shift), t, 0.0)


def _lane_j(shape):
    return jnp.bitwise_and(_iota(shape, 1), CHUNK - 1)


def _sigmoid(x):
    return 1.0 / (1.0 + jnp.exp(-x))


def _softplus(x):
    return jnp.maximum(x, 0.0) + jnp.log(1.0 + jnp.exp(-jnp.abs(x)))


def _shift_rows(x, prev8, s, row):
    if s == 0:
        return x
    rp = pltpu.roll(prev8, s, 0)
    rp_t = jnp.concatenate([rp] * (x.shape[0] // 8), axis=0)
    return jnp.where(row < s, rp_t, pltpu.roll(x, s, 0))


def _proj_kernel(x_ref, *refs):
    n = len(refs) // 2
    xb = x_ref[...].astype(BF16)
    for w_ref, o_ref in zip(refs[:n], refs[n:]):
        o_ref[...] = jnp.dot(xb, w_ref[...], preferred_element_type=F32)


def _project(x2d, ws, tm):
    t_tok, d = x2d.shape
    return pl.pallas_call(
        _proj_kernel,
        grid=(t_tok // tm,),
        in_specs=[pl.BlockSpec((tm, d), lambda i: (i, 0))]
        + [pl.BlockSpec(w.shape, lambda i: (0, 0)) for w in ws],
        out_specs=[pl.BlockSpec((tm, w.shape[1]), lambda i: (i, 0)) for w in ws],
        out_shape=[jax.ShapeDtypeStruct((t_tok, w.shape[1]), F32) for w in ws],
        compiler_params=_cparams(1),
        name="proj",
    )(x2d, *ws)


def _rwkv_kernel(u_ref, mu_ref, w0_ref, wup_ref, a0_ref, aup_ref, gup_ref, kk_ref, ka_ref, rk_ref,
                 gng_ref, gnb_ref, o_ref,
                 st_ref, prev_ref, r_s, lw_s, k_s, v_s, a_s, b_s, g_s, y_s):
    ct = u_ref.shape[1]
    gw = GROUP_WIDTH

    @pl.when(pl.program_id(1) == 0)
    def _():
        st_ref[...] = jnp.zeros_like(st_ref)
        prev_ref[...] = jnp.zeros_like(prev_ref)

    u = u_ref[0]
    row = _iota(u.shape, 0)
    prev = jnp.where(row == 0, prev_ref[0:1, :], pltpu.roll(u, 1, 0))
    prev_ref[0:1, :] = u_ref[0, ct - 1:ct, :]
    us = u + (prev - u) * mu_ref[...]
    r = us[:, 0:gw]
    k = us[:, gw:2 * gw]
    v = us[:, 2 * gw:3 * gw]
    misc = us[:, 3 * gw:]
    lnl = -_softplus(-(w0_ref[...] + _dot(jnp.tanh(misc), wup_ref[...]))) - 0.5
    a = _sigmoid(a0_ref[...] + _dot(misc, aup_ref[...]))
    ones_bd = _head_ones(gw, gw, 6, 6)
    kkp = k * kk_ref[...]
    kk = kkp / jnp.maximum(jnp.sqrt(_dot_sel(kkp * kkp, ones_bd)), 1e-12)
    k2 = k * (1.0 + (a - 1.0) * ka_ref[...])
    r_s[...] = r
    lw_s[...] = -jnp.exp(lnl)
    k_s[...] = k2
    v_s[...] = v
    a_s[...] = -kk
    b_s[...] = kk * a
    g_s[...] = _dot(_sigmoid(misc), gup_ref[...])

    def chunk_body(ci, carry):
        sl = pl.ds(pl.multiple_of(ci * CHUNK, CHUNK), CHUNK)
        r_c, lw_c, k_c, v_c, a_c, b_c = r_s[sl, :], lw_s[sl, :], k_s[sl, :], v_s[sl, :], a_s[sl, :], b_s[sl, :]
        st = st_ref[...]
        cs = _sel_dot(_tril_ones(CHUNK), lw_c)
        g_incl = jnp.exp(cs)
        g_prev = jnp.exp(cs - lw_c)
        g_inv = jnp.exp(-cs)
        lhs = jnp.concatenate([a_c * g_prev, r_c * g_incl], axis=0)
        sb = _dot_nt(lhs, _bd(b_c * g_inv))
        sk = _dot_nt(lhs, _bd(k_c * g_inv))
        shape = (CHUNK, gw)
        i_idx, j_idx = _iota(shape, 0), _lane_j(shape)
        strict, incl = j_idx < i_idx, j_idx <= i_idx
        a_ab = jnp.where(strict, sb[:CHUNK], 0.0)
        a_rb = jnp.where(incl, sb[CHUNK:], 0.0)
        a_k = jnp.concatenate([jnp.where(strict, sk[:CHUNK], 0.0), jnp.where(incl, sk[CHUNK:], 0.0)], axis=0)
        p = jnp.where(j_idx == i_idx, 1.0, 0.0) + a_ab
        ap = a_ab
        for _ in range(5):
            ap = _dot(ap, _bd(ap))
            p = p + _dot(p, _bd(ap))
        from_state = _dot(lhs, st)
        from_v = _dot(a_k, _bd(v_c))
        uu = _dot(p, _bd(from_state[:CHUNK] + from_v[:CHUNK]))
        y_s[sl, :] = from_state[CHUNK:] + from_v[CHUNK:] + _dot(a_rb, _bd(uu))
        tail = jnp.exp(cs[CHUNK - 1:CHUNK, :] - cs)
        upd = _dot_tn(jnp.concatenate([b_c * tail, k_c * tail], axis=0), jnp.concatenate([uu, v_c], axis=0))
        g_col = jnp.exp(_dot_tn_sel(lw_c, jnp.ones((CHUNK, gw), BF16)))
        st_ref[...] = g_col * st + jnp.where(_head_mask(gw, gw, 6, 6), upd, 0.0)
        return carry

    lax.fori_loop(0, ct // CHUNK, chunk_body, 0)

    y = y_s[...]
    mean = _dot_sel(y, ones_bd) * (1.0 / HEAD_DIM)
    yc = y - mean
    var = _dot_sel(yc * yc, ones_bd) * (1.0 / HEAD_DIM)
    yn = yc * lax.rsqrt(var + RWKV_GN_EPS) * gng_ref[...] + gnb_ref[...]
    bonus = _dot_sel(r_s[...] * k_s[...] * rk_ref[...], ones_bd) * v_s[...]
    o_ref[0] = ((yn + bonus) * g_s[...]).astype(o_ref.dtype)


def _rwkv(u_a, prm, ct):
    bsz, s, _ = u_a.shape
    gw = GROUP_WIDTH
    vec = lambda n: pl.BlockSpec((1, n), lambda b, c: (0, 0))
    mat = lambda m, n: pl.BlockSpec((m, n), lambda b, c: (0, 0))
    tile = lambda n: pltpu.VMEM((ct, n), F32)
    return pl.pallas_call(
        _rwkv_kernel,
        grid=(bsz, s // ct),
        in_specs=[pl.BlockSpec((1, ct, D_IN_A), lambda b, c: (b, c, 0)), vec(D_IN_A), vec(gw), mat(MISC, gw),
                  vec(gw), mat(MISC, gw), mat(MISC, gw), vec(gw), vec(gw), vec(gw), vec(gw), vec(gw)],
        out_specs=pl.BlockSpec((1, ct, gw), lambda b, c: (b, c, 0)),
        out_shape=jax.ShapeDtypeStruct((bsz, s, gw), BF16),
        scratch_shapes=[pltpu.VMEM((gw, gw), F32), pltpu.VMEM((8, D_IN_A), F32)] + [tile(gw)] * 8,
        compiler_params=_cparams(2),
        name="rwkv7",
    )(u_a, *prm)


def _gla_kernel(u_ref, m_ref, aup_ref, ab_ref, ng_ref, o_ref,
                st_ref, q_s, k_s, b_s, v_s, oi_s, ox_s):
    ct = u_ref.shape[1]
    qw, gw = GLA_QK_WIDTH, GROUP_WIDTH

    @pl.when(pl.program_id(1) == 0)
    def _():
        st_ref[...] = jnp.zeros_like(st_ref)

    u = u_ref[0]
    q_s[...] = u[:, 0:qw] * (GLA_KEY_DIM ** -0.5)
    k_s[...] = u[:, qw:2 * qw]
    v_s[...] = u[:, 2 * qw:2 * qw + gw]
    gate = u[:, 2 * qw + gw:]
    log_a = -_softplus(-(_dot(m_ref[0], aup_ref[...]) + ab_ref[...])) * (1.0 / GLA_TAU)
    head_mask = _head_mask(qw, gw, 5, 6)

    b_s[...] = log_a

    def chunk_body(ci, carry):
        sl = pl.ds(pl.multiple_of(ci * CHUNK, CHUNK), CHUNK)
        la_c = b_s[sl, :]
        q_c, k_c, v_c = q_s[sl, :], k_s[sl, :], v_s[sl, :]
        st = st_ref[...]
        b = _sel_dot(_tril_ones(CHUNK), la_c)
        b_s[sl, :] = b
        v_bd = _bd(v_c)
        j_idx = _lane_j((GLA_SUB, gw))
        parts = [jnp.zeros((GLA_SUB, gw), F32)]
        for blk in range(1, CHUNK // GLA_SUB):
            lo = blk * GLA_SUB
            beta = b[lo - 1:lo, :]
            q_blk = q_c[lo:lo + GLA_SUB, :] * jnp.exp(b[lo:lo + GLA_SUB, :] - beta)
            k_sc = k_c * jnp.exp(jnp.minimum(beta - b, 0.0))
            sc = jnp.where(j_idx < lo, _dot_nt(q_blk, _bd(k_sc, 6, 5)), 0.0)
            parts.append(_dot(sc, v_bd))
        ox_s[sl, :] = _dot(q_c * jnp.exp(b), st) + jnp.concatenate(parts, axis=0)
        tail = jnp.exp(b[CHUNK - 1:CHUNK, :] - b)
        g_col = jnp.exp(_dot_tn_sel(la_c, jnp.ones((CHUNK, gw), BF16)))
        st_ref[...] = g_col * st + jnp.where(head_mask, _dot_tn(k_c * tail, v_c), 0.0)
        return carry

    lax.fori_loop(0, ct // CHUNK, chunk_body, 0)

    seg_bcast = jnp.where(head_mask, 1.0, 0.0).astype(BF16)

    def row_body(i, carry):
        c0 = pl.multiple_of(lax.shift_left(_shr(i, 4), jnp.int32(4)), GLA_SUB)
        sl = pl.ds(c0, GLA_SUB)
        k_c, b_c, v_c = k_s[sl, :], b_s[sl, :], v_s[sl, :]
        q_i, b_i = q_s[pl.ds(i, 1), :], b_s[pl.ds(i, 1), :]
        valid = (_iota((GLA_SUB, qw), 0) + c0) <= i
        p = jnp.where(valid, k_c * q_i * jnp.exp(jnp.minimum(b_i - b_c, 0.0)), 0.0)
        sx = jnp.dot(p.astype(BF16), seg_bcast, preferred_element_type=F32)
        oi_s[pl.ds(i, 1), :] = jnp.sum(sx * v_c, axis=0, keepdims=True)
        return carry

    lax.fori_loop(0, ct, row_body, 0, unroll=8)

    o = oi_s[...] + ox_s[...]
    ms = _dot_sel(o * o, _head_ones(gw, gw, 6, 6)) * (1.0 / HEAD_DIM)
    on = o * lax.rsqrt(ms + NORM_EPS) * ng_ref[...]
    o_ref[0] = (on * (gate * _sigmoid(gate))).astype(o_ref.dtype)


def _gla(u_b, misc, prm, ct):
    bsz, s, wb = u_b.shape
    qw, gw = GLA_QK_WIDTH, GROUP_WIDTH
    return pl.pallas_call(
        _gla_kernel,
        grid=(bsz, s // ct),
        in_specs=[pl.BlockSpec((1, ct, wb), lambda b, c: (b, c, 0)),
                  pl.BlockSpec((1, ct, MISC), lambda b, c: (b, c, 0)),
                  pl.BlockSpec((MISC, qw), lambda b, c: (0, 0)),
                  pl.BlockSpec((1, qw), lambda b, c: (0, 0)),
                  pl.BlockSpec((1, gw), lambda b, c: (0, 0))],
        out_specs=pl.BlockSpec((1, ct, gw), lambda b, c: (b, c, 0)),
        out_shape=jax.ShapeDtypeStruct((bsz, s, gw), BF16),
        scratch_shapes=[pltpu.VMEM((qw, gw), F32), pltpu.VMEM((ct, qw), F32), pltpu.VMEM((ct, qw), F32),
                        pltpu.VMEM((ct, qw), F32), pltpu.VMEM((ct, gw), F32), pltpu.VMEM((ct, gw), F32),
                        pltpu.VMEM((ct, gw), F32)],
        compiler_params=_cparams(2),
        name="gla",
    )(u_b, misc, *prm)


def _mlstm_kernel(u_ref, m_ref, cw_ref, cb_ref, eig_ref, efg_ref, ib_ref, fb_ref, ng_ref, o_ref,
                  mem_ref, nb_ref, mst_ref, prev_ref, q_s, k_s, v_s, li_s, lf_s, h_s):
    ct = u_ref.shape[1]
    gw = GROUP_WIDTH

    @pl.when(pl.program_id(1) == 0)
    def _():
        mem_ref[...] = jnp.zeros_like(mem_ref)
        nb_ref[...] = jnp.zeros_like(nb_ref)
        mst_ref[...] = jnp.zeros_like(mst_ref)
        prev_ref[...] = jnp.zeros_like(prev_ref)

    u = u_ref[0]
    qk = u[:, 0:2 * gw]
    row = _iota(qk.shape, 0)
    prev8 = prev_ref[...]
    conv = cb_ref[...] + jnp.zeros_like(qk)
    for j in range(MLSTM_CONV):
        conv = conv + cw_ref[j:j + 1, :] * _shift_rows(qk, prev8, MLSTM_CONV - 1 - j, row)
    prev_ref[...] = u_ref[0, ct - 8:ct, 0:2 * gw]
    qk_act = conv * _sigmoid(conv)
    q_s[...] = qk_act[:, 0:gw]
    k_s[...] = qk_act[:, gw:] * (HEAD_DIM ** -0.5)
    v_s[...] = u[:, 2 * gw:3 * gw]
    o_gate = u[:, 3 * gw:]
    misc = m_ref[0]
    li_s[...] = _dot_sel(misc, eig_ref[...]) + ib_ref[...]
    lf_s[...] = -_softplus(-(_dot_sel(misc, efg_ref[...]) + fb_ref[...]))
    neg_inf = -jnp.inf

    def chunk_body(ci, carry):
        sl = pl.ds(pl.multiple_of(ci * CHUNK, CHUNK), CHUNK)
        q_c, k_c, v_c, li_c, lf_c = q_s[sl, :], k_s[sl, :], v_s[sl, :], li_s[sl, :], lf_s[sl, :]
        mem, nb, m_e = mem_ref[...], nb_ref[...], mst_ref[0:1, :]
        shape = (CHUNK, gw)
        i_idx, j_idx = _iota(shape, 0), _lane_j(shape)
        incl = j_idx <= i_idx
        head_mask = _head_mask(gw, gw, 6, 6)
        fc = _sel_dot(_tril_ones(CHUNK), lf_c)
        g_row = li_c - fc
        ones_avg = jnp.full(shape, 1.0 / HEAD_DIM, BF16)
        g_hi, g_lo = _split(_bd(g_row))
        dn = (((1,), (1,)), ((), ()))
        g_t = (lax.dot_general(ones_avg, g_hi, dn, preferred_element_type=F32)
               + lax.dot_general(ones_avg, g_lo, dn, preferred_element_type=F32))
        log_w = jnp.where(incl, fc + g_t, neg_inf)
        cm = g_row
        for sh in (1, 2, 4, 8, 16, 32):
            cm = jnp.maximum(cm, jnp.where(i_idx < sh, neg_inf, pltpu.roll(cm, sh, 0)))
        log_carry = fc + m_e
        m_row = jnp.maximum(fc + cm, log_carry)
        sc = _dot_nt(q_c, _bd(k_c)) * jnp.exp(log_w - m_row)
        w_c = jnp.exp(log_carry - m_row)
        scb = sc.astype(BF16)
        num = jnp.dot(scb, _bd(v_c).astype(BF16), preferred_element_type=F32) + w_c * _dot(q_c, mem)
        den = (jnp.dot(scb, _head_ones(gw, gw, 6, 6), preferred_element_type=F32) + w_c * _dot(q_c, nb))
        h_s[sl, :] = num / jnp.maximum(jnp.abs(den), jnp.exp(-m_row))
        f_end = fc[CHUNK - 1:CHUNK, :]
        log_kv = f_end - fc + li_c
        m_new = jnp.maximum(f_end + m_e, jnp.max(log_kv, axis=0, keepdims=True))
        kwk = jnp.exp(log_kv - m_new) * k_c
        cd = jnp.exp(f_end + m_e - m_new)
        cd_rows = jnp.where(_iota((16, gw), 0) == 0, cd, 0.0)
        cd_col = _dot_tn_sel(cd_rows, jnp.ones((16, gw), BF16))
        upd = _dot_tn(kwk, jnp.concatenate([v_c, jnp.ones(shape, F32)], axis=1))
        mem_ref[...] = cd_col * mem + jnp.where(head_mask, upd[:, :gw], 0.0)
        nb_ref[...] = cd_col * nb + jnp.where(head_mask, upd[:, gw:], 0.0)
        mst_ref[0:1, :] = m_new
        return carry

    lax.fori_loop(0, ct // CHUNK, chunk_body, 0)

    h = h_s[...]
    ones_bd = _head_ones(gw, gw, 6, 6)
    mean = _dot_sel(h, ones_bd) * (1.0 / HEAD_DIM)
    hc = h - mean
    var = _dot_sel(hc * hc, ones_bd) * (1.0 / HEAD_DIM)
    hn = hc * lax.rsqrt(var + LN_EPS) * ng_ref[...]
    o_ref[0] = (hn * _sigmoid(o_gate)).astype(o_ref.dtype)


def _mlstm(u_c, misc, prm, ct):
    bsz, s, wc = u_c.shape
    gw = GROUP_WIDTH
    vec = lambda n: pl.BlockSpec((1, n), lambda b, c: (0, 0))
    mat = lambda m, n: pl.BlockSpec((m, n), lambda b, c: (0, 0))
    tile = pltpu.VMEM((ct, gw), F32)
    return pl.pallas_call(
        _mlstm_kernel,
        grid=(bsz, s // ct),
        in_specs=[pl.BlockSpec((1, ct, wc), lambda b, c: (b, c, 0)),
                  pl.BlockSpec((1, ct, MISC), lambda b, c: (b, c, 0)),
                  mat(MLSTM_CONV, 2 * gw), vec(2 * gw), mat(MISC, gw), mat(MISC, gw), vec(gw), vec(gw), vec(gw)],
        out_specs=pl.BlockSpec((1, ct, gw), lambda b, c: (b, c, 0)),
        out_shape=jax.ShapeDtypeStruct((bsz, s, gw), BF16),
        scratch_shapes=[pltpu.VMEM((gw, gw), F32), pltpu.VMEM((gw, gw), F32), pltpu.VMEM((8, gw), F32),
                        pltpu.VMEM((8, 2 * gw), F32)] + [tile] * 6,
        compiler_params=_cparams(2),
        name="mlstm",
    )(u_c, misc, *prm)


def _mla_prep_kernel(u_ref, m_ref, cq_ref, sq_ref, csk_ref, qg_ref, kvg_ref, wq_ref, wqr_ref, wk_ref, wv_ref,
                     selk_ref, q_ref, k_ref, v_ref):
    u = u_ref[0]
    cq = u[:, 0:MLA_Q_RANK]
    ckv = u[:, MLA_Q_RANK:]
    cqn = cq * lax.rsqrt(jnp.mean(cq * cq, axis=-1, keepdims=True) + NORM_EPS) * qg_ref[...]
    ckvn = ckv * lax.rsqrt(jnp.mean(ckv * ckv, axis=-1, keepdims=True) + NORM_EPS) * kvg_ref[...]
    cqb, ckvb = cqn.astype(BF16), ckvn.astype(BF16)
    k_rope = _dot(m_ref[0] * csk_ref[0], selk_ref[...])
    cos_q, sin_q = cq_ref[0], sq_ref[0]
    scale = MLA_QK ** -0.5
    for h in range(N_HEADS):
        qh = (jnp.dot(cqb, wq_ref[h], preferred_element_type=F32) * cos_q
              + jnp.dot(cqb, wqr_ref[h], preferred_element_type=F32) * sin_q) * scale
        q_ref[0, h] = qh.astype(BF16)
        k_ref[0, h] = (jnp.dot(ckvb, wk_ref[h], preferred_element_type=F32) + k_rope).astype(BF16)
        v_ref[0, h] = jnp.dot(ckvb, wv_ref[h], preferred_element_type=F32).astype(BF16)


def _mla_prep(u_d, misc, tabs, prm, tm):
    bsz, s, wd = u_d.shape
    full = lambda a: pl.BlockSpec(a.shape, lambda b, c: (0,) * a.ndim)
    tok = lambda n: pl.BlockSpec((1, tm, n), lambda b, c: (b, c, 0))
    head = lambda n: pl.BlockSpec((1, N_HEADS, tm, n), lambda b, c: (b, 0, c, 0))
    return pl.pallas_call(
        _mla_prep_kernel,
        grid=(bsz, s // tm),
        in_specs=[tok(wd), tok(MISC), tok(MLA_QK), tok(MLA_QK), tok(MISC)] + [full(a) for a in prm],
        out_specs=[head(MLA_QK), head(MLA_QK), head(HEAD_DIM)],
        out_shape=[jax.ShapeDtypeStruct((bsz, N_HEADS, s, MLA_QK), BF16),
                   jax.ShapeDtypeStruct((bsz, N_HEADS, s, MLA_QK), BF16),
                   jax.ShapeDtypeStruct((bsz, N_HEADS, s, HEAD_DIM), BF16)],
        compiler_params=_cparams(2),
        name="mla_prep",
    )(u_d, misc, *tabs, *prm)


def _attn_kernel(q_ref, k_ref, v_ref, o_ref):
    tq = q_ref.shape[2]
    qi = pl.program_id(2)
    q = q_ref[0, 0]

    def step(j, carry, diagonal):
        m, l, acc = carry
        sl = pl.ds(pl.multiple_of(j * tq, tq), tq)
        kb, vb = k_ref[0, 0, sl, :], v_ref[0, 0, sl, :]
        s = lax.dot_general(q, kb, (((1,), (1,)), ((), ())), preferred_element_type=F32)
        if diagonal:
            s = jnp.where(_iota((tq, tq), 1) <= _iota((tq, tq), 0), s, -jnp.inf)
        m_new = jnp.maximum(m, jnp.max(s, axis=-1, keepdims=True))
        p = jnp.exp(s - m_new)
        alpha = jnp.exp(m - m_new)
        l = alpha * l + jnp.sum(p, axis=-1, keepdims=True)
        acc = alpha * acc + jnp.dot(p.astype(BF16), vb, preferred_element_type=F32)
        return m_new, l, acc

    init = (jnp.full((tq, 1), -jnp.inf, F32), jnp.zeros((tq, 1), F32), jnp.zeros((tq, HEAD_DIM), F32))
    carry = lax.fori_loop(0, qi, lambda j, c: step(j, c, False), init)
    _, l, acc = step(qi, carry, True)
    o_ref[0, 0] = (acc / l).astype(o_ref.dtype)


def _attention(q, k, v, tq):
    bsz, nh, s, _ = q.shape
    return pl.pallas_call(
        _attn_kernel,
        grid=(bsz, nh, s // tq),
        in_specs=[pl.BlockSpec((1, 1, tq, MLA_QK), lambda b, h, i: (b, h, i, 0)),
                  pl.BlockSpec((1, 1, s, MLA_QK), lambda b, h, i: (b, h, 0, 0)),
                  pl.BlockSpec((1, 1, s, HEAD_DIM), lambda b, h, i: (b, h, 0, 0))],
        out_specs=pl.BlockSpec((1, 1, tq, HEAD_DIM), lambda b, h, i: (b, h, i, 0)),
        out_shape=jax.ShapeDtypeStruct((bsz, nh, s, HEAD_DIM), BF16),
        compiler_params=_cparams(3),
        name="mla_attn",
    )(q, k, v)


def _layer_norm(z, g, b):
    zc = z - jnp.mean(z, axis=-1, keepdims=True)
    return zc * lax.rsqrt(jnp.mean(zc * zc, axis=-1, keepdims=True) + LN_EPS) * g + b


def _outproj_kernel(x_ref, ya_ref, yb_ref, yc_ref, yd_ref, wo_ref, wod_ref, g_ref, b_ref, o_ref):
    gw = GROUP_WIDTH
    mix = jnp.dot(ya_ref[0], wo_ref[0:gw, :], preferred_element_type=F32)
    mix = mix + jnp.dot(yb_ref[0], wo_ref[gw:2 * gw, :], preferred_element_type=F32)
    mix = mix + jnp.dot(yc_ref[0], wo_ref[2 * gw:3 * gw, :], preferred_element_type=F32)
    for h in range(N_HEADS):
        mix = mix + jnp.dot(yd_ref[0, h], wod_ref[h], preferred_element_type=F32)
    o_ref[0] = _layer_norm(DN_ALPHA * x_ref[0] + mix, g_ref[...], b_ref[...])


def _outproj(x, ya, yb, yc, yd, wo, wod, g, b, tm):
    bsz, s, d = x.shape
    gw = GROUP_WIDTH
    tok = lambda n: pl.BlockSpec((1, tm, n), lambda bb, c: (bb, c, 0))
    full = lambda a: pl.BlockSpec(a.shape, lambda bb, c: (0,) * a.ndim)
    return pl.pallas_call(
        _outproj_kernel,
        grid=(bsz, s // tm),
        in_specs=[tok(d), tok(gw), tok(gw), tok(gw),
                  pl.BlockSpec((1, N_HEADS, tm, HEAD_DIM), lambda bb, c: (bb, 0, c, 0)),
                  full(wo), full(wod), full(g), full(b)],
        out_specs=tok(d),
        out_shape=jax.ShapeDtypeStruct((bsz, s, d), F32),
        compiler_params=_cparams(2),
        name="outproj_ln1",
    )(x, ya, yb, yc, yd, wo, wod, g, b)


def _router_kernel(x_ref, w_ref, b_ref, idx_ref, gate_ref):
    x = x_ref[...]
    xh, xl = _split(x)
    wh, wl = _split(w_ref[...])
    dn = (((1,), (1,)), ((), ()))
    logits = (lax.dot_general(wh, xh, dn, preferred_element_type=F32)
              + lax.dot_general(wh, xl, dn, preferred_element_type=F32)
              + lax.dot_general(wl, xh, dn, preferred_element_type=F32)) + b_ref[...]
    tm = x.shape[0]
    ng, ne = N_EXPERT_GROUPS, EXPERTS_PER_GROUP
    gl = [logits[i:i + 1, :] for i in range(ng)]
    gmax = functools.reduce(jnp.maximum, gl)
    gsum = functools.reduce(jnp.add, [jnp.exp(g - gmax) for g in gl])
    group_p = 1.0 / gsum
    gidx = jnp.full((1, tm), ng - 1, jnp.int32)
    for i in range(ng - 2, -1, -1):
        gidx = jnp.where(gl[i] == gmax, i, gidx)
    el = logits[8 + (ng - 1) * ne:8 + ng * ne, :]
    for i in range(ng - 2, -1, -1):
        el = jnp.where(gidx == i, logits[8 + i * ne:8 + (i + 1) * ne, :], el)
    rows = _iota((ne, tm), 0)
    m1 = jnp.max(el, axis=0, keepdims=True)
    i1 = jnp.min(jnp.where(el == m1, rows, ne), axis=0, keepdims=True)
    el2 = jnp.where(rows == i1, -jnp.inf, el)
    m2 = jnp.max(el2, axis=0, keepdims=True)
    i2 = jnp.min(jnp.where(el2 == m2, rows, ne), axis=0, keepdims=True)
    z = jnp.sum(jnp.exp(el - m1), axis=0, keepdims=True)
    p1 = 1.0 / z
    p2 = jnp.exp(m2 - m1) / z
    base = gidx * ne
    idx_ref[...] = jnp.concatenate([base + i1, base + i2], axis=0)
    gate_ref[...] = jnp.concatenate([group_p * p1 / (p1 + p2), group_p * p2 / (p1 + p2)], axis=0)


def _router(x2d, w_t, b_col, tm):
    t_tok, d = x2d.shape
    return pl.pallas_call(
        _router_kernel,
        grid=(t_tok // tm,),
        in_specs=[pl.BlockSpec((tm, d), lambda i: (i, 0)), pl.BlockSpec(w_t.shape, lambda i: (0, 0)),
                  pl.BlockSpec(b_col.shape, lambda i: (0, 0))],
        out_specs=[pl.BlockSpec((TOP_K, tm), lambda i: (0, i)), pl.BlockSpec((TOP_K, tm), lambda i: (0, i))],
        out_shape=[jax.ShapeDtypeStruct((TOP_K, t_tok), jnp.int32), jax.ShapeDtypeStruct((TOP_K, t_tok), F32)],
        compiler_params=_cparams(1),
        name="router",
    )(x2d, w_t, b_col)


def _moe_kernel(be_ref, x_ref, gate_ref, wg_ref, wu_ref, wd_ref, o_ref, wg_s, wu_s, wd_s):
    i = pl.program_id(0)

    @pl.when(jnp.logical_or(i == 0, be_ref[i] != be_ref[jnp.maximum(i - 1, 0)]))
    def _():
        wg_s[...] = wg_ref[0].astype(BF16)
        wu_s[...] = wu_ref[0].astype(BF16)
        wd_s[...] = wd_ref[0].astype(BF16)

    xb = x_ref[...]
    hg = jnp.dot(xb, wg_s[...], preferred_element_type=F32)
    hu = jnp.dot(xb, wu_s[...], preferred_element_type=F32)
    hid = (hg * _sigmoid(hg)) * hu
    y = jnp.dot(hid.astype(BF16), wd_s[...], preferred_element_type=F32) * gate_ref[...]
    o_ref[...] = y.astype(o_ref.dtype)


def _moe(x_rows, block_expert, row_gate, wg, wu, wd, layer):
    n_rows, d = x_rows.shape
    n_blocks = n_rows // MOE_BLOCK
    grid_spec = pltpu.PrefetchScalarGridSpec(
        num_scalar_prefetch=1,
        grid=(n_blocks,),
        in_specs=[pl.BlockSpec((MOE_BLOCK, d), lambda i, be: (i, 0)),
                  pl.BlockSpec((MOE_BLOCK, 1), lambda i, be: (i, 0)),
                  pl.BlockSpec((None, 1, d, D_EXPERT), lambda i, be: (layer, be[i], 0, 0)),
                  pl.BlockSpec((None, 1, d, D_EXPERT), lambda i, be: (layer, be[i], 0, 0)),
                  pl.BlockSpec((None, 1, D_EXPERT, d), lambda i, be: (layer, be[i], 0, 0))],
        out_specs=pl.BlockSpec((MOE_BLOCK, d), lambda i, be: (i, 0)),
        scratch_shapes=[pltpu.VMEM((d, D_EXPERT), BF16), pltpu.VMEM((d, D_EXPERT), BF16),
                        pltpu.VMEM((D_EXPERT, d), BF16)],
    )
    return pl.pallas_call(
        _moe_kernel,
        grid_spec=grid_spec,
        out_shape=jax.ShapeDtypeStruct((n_rows, d), BF16),
        compiler_params=_cparams(1),
        name="moe_experts",
    )(block_expert, x_rows, row_gate.reshape(n_rows, 1), wg, wu, wd)


def _dispatch(idx, gate, t_tok):
    n_asg = t_tok * TOP_K
    flat_e = idx.reshape(n_asg)
    flat_g = gate.reshape(n_asg)
    order = jnp.argsort(flat_e)
    inverse = jnp.argsort(order).astype(jnp.int32)
    starts = jnp.searchsorted(flat_e[order], jnp.arange(N_EXPERTS + 1, dtype=jnp.int32), side='left')
    starts = starts.astype(jnp.int32)
    counts = starts[1:] - starts[:-1]
    starts = starts[:-1]
    padded = (counts + MOE_BLOCK - 1) // MOE_BLOCK * MOE_BLOCK
    pad_ends = jnp.cumsum(padded)
    pad_starts = pad_ends - padded
    n_rows = -(-(n_asg + N_EXPERTS * (MOE_BLOCK - 1)) // MOE_BLOCK) * MOE_BLOCK
    n_blocks = n_rows // MOE_BLOCK
    block_expert = jnp.minimum(jnp.searchsorted(pad_ends, jnp.arange(n_blocks) * MOE_BLOCK, side='right'),
                               N_EXPERTS - 1).astype(jnp.int32)
    pos = pad_starts[flat_e] + inverse - starts[flat_e]
    row_e = jnp.repeat(block_expert, MOE_BLOCK)
    row_rank = jnp.arange(n_rows, dtype=jnp.int32) - pad_starts[row_e]
    valid = row_rank < counts[row_e]
    asg = order[jnp.where(valid, starts[row_e] + row_rank, 0)]
    row_tok = jnp.where(valid, asg % t_tok, 0).astype(jnp.int32)
    row_gate = jnp.where(valid, flat_g[asg], 0.0)
    return block_expert, row_tok, pos.reshape(TOP_K, t_tok), row_gate


def _final_kernel(x_ref, p_ref, f0_ref, f1_ref, wg_ref, bg_ref, wp_ref, g_ref, b_ref, o_ref):
    x = x_ref[...]
    gate = _sigmoid(_dot(x, wg_ref[...]) + bg_ref[...])
    ple = gate * _dot(p_ref[...], wp_ref[...])
    ffn = f0_ref[...].astype(F32) + f1_ref[...].astype(F32)
    o_ref[...] = _layer_norm(DN_ALPHA * x + ffn + ple, g_ref[...], b_ref[...])


def _final(x2d, p2d, ffn0, ffn1, wg, bg, wp, g, b, tm):
    t_tok, d = x2d.shape
    full = lambda a: pl.BlockSpec(a.shape, lambda i: (0,) * a.ndim)
    return pl.pallas_call(
        _final_kernel,
        grid=(t_tok // tm,),
        in_specs=[pl.BlockSpec((tm, d), lambda i: (i, 0)), pl.BlockSpec((tm, p2d.shape[1]), lambda i: (i, 0)),
                  pl.BlockSpec((tm, d), lambda i: (i, 0)), pl.BlockSpec((tm, d), lambda i: (i, 0)),
                  full(wg), full(bg), full(wp), full(g), full(b)],
        out_specs=pl.BlockSpec((tm, d), lambda i: (i, 0)),
        out_shape=jax.ShapeDtypeStruct((t_tok, d), F32),
        compiler_params=_cparams(1),
        name="ple_ln2",
    )(x2d, p2d, ffn0, ffn1, wg, bg, wp, g, b)


def _pad_rows(w, offset, total=MISC):
    return jnp.zeros((total, w.shape[1]), w.dtype).at[offset:offset + w.shape[0]].set(w)


def _rot_cols(w):
    half = w.shape[1] // 2
    return jnp.concatenate([-w[:, half:], w[:, :half]], axis=1)


def _row(v):
    return v.reshape(1, -1)


def _expand_heads(v):
    return jnp.repeat(v, HEAD_DIM).reshape(1, GROUP_WIDTH)


def _head_select(offset):
    sel = np.zeros((MISC, GROUP_WIDTH), np.float32)
    for h in range(N_HEADS):
        sel[offset + h, h * HEAD_DIM:(h + 1) * HEAD_DIM] = 1.0
    return jnp.asarray(sel, BF16)


def _rope_tables(positions):
    inv_freq = ROPE_THETA ** (-jnp.arange(0, MLA_ROPE, 2, dtype=F32) / MLA_ROPE)
    ang = positions.astype(F32)[..., None] * inv_freq
    cos, sin = jnp.cos(ang), jnp.sin(ang)
    cos2 = jnp.concatenate([cos, cos], axis=-1)
    sin2 = jnp.concatenate([sin, sin], axis=-1)
    lead = cos.shape[:-1]
    cos_q = jnp.concatenate([jnp.ones(lead + (MLA_NOPE,), F32), cos2], axis=-1)
    sin_q = jnp.concatenate([jnp.zeros(lead + (MLA_NOPE,), F32), sin2], axis=-1)
    cs_k = jnp.concatenate([jnp.zeros(lead + (MISC_KR,), F32), cos2, sin2,
                            jnp.zeros(lead + (MISC - MISC_KR_ROT - MLA_ROPE,), F32)], axis=-1)
    return cos_q, sin_q, cs_k


def _rope_key_select():
    sel = np.zeros((MISC, MLA_QK), np.float32)
    for c in range(MLA_ROPE):
        sel[MISC_KR + c, MLA_NOPE + c] = 1.0
        sel[MISC_KR_ROT + c, MLA_NOPE + c] = 1.0
    return jnp.asarray(sel, BF16)


def _split_w_in(w):
    a0, b0 = 0, D_IN_A
    c0, d0 = b0 + D_IN_B, b0 + D_IN_B + D_IN_C
    qw, gw = GLA_QK_WIDTH, GROUP_WIDTH
    w_a = w[:, a0:b0]
    gla_ad = w[:, b0 + 2 * qw + gw:b0 + 2 * qw + gw + GLA_GATE_RANK]
    w_b = jnp.concatenate([w[:, b0:b0 + 2 * qw + gw], w[:, b0 + 2 * qw + gw + GLA_GATE_RANK:c0]], axis=1)
    w_c = w[:, c0:c0 + 4 * gw]
    gates = w[:, c0 + 4 * gw:d0]
    w_d = w[:, d0:d0 + MLA_Q_RANK + MLA_KV_RANK]
    kr = w[:, d0 + MLA_Q_RANK + MLA_KV_RANK:]
    z = lambda n: jnp.zeros((w.shape[0], n), w.dtype)
    w_m = jnp.concatenate([gla_ad, gates, z(MISC_KR - MISC_FG - N_HEADS), kr, _rot_cols(kr),
                           z(MISC - MISC_KR_ROT - MLA_ROPE)], axis=1)
    return [t.astype(BF16) for t in (w_a, w_b, w_c, w_d, w_m)]


def _tiles(s):
    return min(512, s), min(512, s)


def kernel(x, p, positions, w_in, rwkv_mu, rwkv_w0, rwkv_w_up, rwkv_a0, rwkv_a_up, rwkv_g_up, rwkv_k_k, rwkv_k_a, rwkv_r_k, rwkv_gn_g, rwkv_gn_b, gla_alpha_up, gla_alpha_b, gla_norm_g, mlstm_conv_w, mlstm_conv_b, mlstm_i_b, mlstm_f_b, mlstm_norm_g, mla_q_norm_g, mla_w_uq, mla_kv_norm_g, mla_w_ukv, w_out, ln1_g, ln1_b, moe_w_rg, moe_b_rg, moe_w_re, moe_b_re, moe_w_gate, moe_w_up, moe_w_down, ple_w_gate, ple_b_gate, ple_w, ln2_g, ln2_b):
    bsz, s, d = x.shape
    t_tok = bsz * s
    depth = w_in.shape[0]
    ct, tq = _tiles(s)
    tm = min(512, t_tok)
    rope_tabs = _rope_tables(positions)
    sel_ig, sel_fg, sel_k = _head_select(MISC_IG), _head_select(MISC_FG), _rope_key_select()
    for i in range(depth):
        u_a, u_b, u_c, u_d, u_m = _project(x.reshape(t_tok, d), _split_w_in(w_in[i]), tm)
        u_a, u_b, u_c, u_d, u_m = (t.reshape(bsz, s, -1) for t in (u_a, u_b, u_c, u_d, u_m))

        rk = RWKV_DECAY_RANK
        ya = _rwkv(u_a, (_row(rwkv_mu[i]), _row(rwkv_w0[i]), _pad_rows(rwkv_w_up[i], 0).astype(BF16),
                         _row(rwkv_a0[i]), _pad_rows(rwkv_a_up[i], rk).astype(BF16),
                         _pad_rows(rwkv_g_up[i], rk + RWKV_ICL_RANK).astype(BF16),
                         _row(rwkv_k_k[i]), _row(rwkv_k_a[i]), _row(rwkv_r_k[i]),
                         _row(rwkv_gn_g[i]), _row(rwkv_gn_b[i])), ct)
        yb = _gla(u_b, u_m, (_pad_rows(gla_alpha_up[i], MISC_GLA_AD).astype(BF16), _row(gla_alpha_b[i]),
                             _row(gla_norm_g[i])), ct)
        yc = _mlstm(u_c, u_m, (mlstm_conv_w[i], _row(mlstm_conv_b[i]), sel_ig, sel_fg,
                               _expand_heads(mlstm_i_b[i]), _expand_heads(mlstm_f_b[i]),
                               _row(mlstm_norm_g[i])), ct)

        wq = mla_w_uq[i].reshape(MLA_Q_RANK, N_HEADS, MLA_QK).transpose(1, 0, 2)
        wq_rot = jnp.concatenate([jnp.zeros((N_HEADS, MLA_Q_RANK, MLA_NOPE), F32),
                                  jax.vmap(_rot_cols)(wq[:, :, MLA_NOPE:])], axis=2)
        wkv = mla_w_ukv[i].reshape(MLA_KV_RANK, N_HEADS, MLA_NOPE + HEAD_DIM).transpose(1, 0, 2)
        wk = jnp.concatenate([wkv[:, :, :MLA_NOPE], jnp.zeros((N_HEADS, MLA_KV_RANK, MLA_ROPE), F32)], axis=2)
        q, k, v = _mla_prep(u_d, u_m, rope_tabs,
                            (_row(mla_q_norm_g[i]), _row(mla_kv_norm_g[i]), wq.astype(BF16), wq_rot.astype(BF16),
                             wk.astype(BF16), wkv[:, :, MLA_NOPE:].astype(BF16), sel_k), ct)
        yd = _attention(q, k, v, tq)

        wo = w_out[i].astype(BF16)
        x = _outproj(x, ya, yb, yc, yd, wo[:3 * GROUP_WIDTH], wo[3 * GROUP_WIDTH:].reshape(N_HEADS, HEAD_DIM, d),
                     _row(ln1_g[i]), _row(ln1_b[i]), ct)

        x2d = x.reshape(t_tok, d)
        w_route = jnp.concatenate([moe_w_rg[i].T, jnp.zeros((8 - N_EXPERT_GROUPS, d), F32), moe_w_re[i].T], axis=0)
        b_route = jnp.concatenate([moe_b_rg[i], jnp.zeros((8 - N_EXPERT_GROUPS,), F32), moe_b_re[i]]).reshape(-1, 1)
        idx, gate = _router(x2d, w_route, b_route, tm)
        block_expert, row_tok, pos, row_gate = _dispatch(idx, gate, t_tok)
        y_rows = _moe(x2d.astype(BF16)[row_tok], block_expert, row_gate, moe_w_gate, moe_w_up, moe_w_down, i)
        x = _final(x2d, p[i].reshape(t_tok, -1), y_rows[pos[0]], y_rows[pos[1]], ple_w_gate[i].astype(BF16),
                   _row(ple_b_gate[i]), ple_w[i].astype(BF16), _row(ln2_g[i]), _row(ln2_b[i]), tm).reshape(bsz, s, d)
    return x
```

```python
import functools

import jax
import jax.numpy as jnp
import numpy as np
from jax import lax
from jax.experimental import pallas as pl
from jax.experimental.pallas import tpu as pltpu

F32 = jnp.float32
BF16 = jnp.bfloat16

D_MODEL = 1024
DEPTH = 4
HEAD_DIM = 64
N_HEADS = 4
GROUP_WIDTH = N_HEADS * HEAD_DIM
D_PLE = 256

RWKV_DECAY_RANK = 32
RWKV_ICL_RANK = 32
RWKV_GATE_RANK = 64
RWKV_GN_EPS = 64e-5
D_IN_A = 3 * GROUP_WIDTH + RWKV_DECAY_RANK + RWKV_ICL_RANK + RWKV_GATE_RANK

GLA_KEY_DIM = 32
GLA_QK_WIDTH = N_HEADS * GLA_KEY_DIM
GLA_GATE_RANK = 16
GLA_TAU = 16.0
D_IN_B = 2 * GLA_QK_WIDTH + 2 * GROUP_WIDTH + GLA_GATE_RANK

MLSTM_CONV = 4
D_IN_C = 4 * GROUP_WIDTH + 2 * N_HEADS

MLA_Q_RANK = 256
MLA_KV_RANK = 128
MLA_NOPE = 64
MLA_ROPE = 32
MLA_QK = MLA_NOPE + MLA_ROPE
D_IN_D = MLA_Q_RANK + MLA_KV_RANK + MLA_ROPE
ROPE_THETA = 10000.0

N_EXPERT_GROUPS = 4
EXPERTS_PER_GROUP = 8
N_EXPERTS = N_EXPERT_GROUPS * EXPERTS_PER_GROUP
TOP_K = 2
D_EXPERT = 512
MOE_BLOCK = 256

DN_ALPHA = (2.0 * DEPTH) ** 0.25
LN_EPS = 1e-5
NORM_EPS = 1e-6

CHUNK = 64
GLA_SUB = 16
MISC = 128
MISC_GLA_AD = 0
MISC_IG = 16
MISC_FG = 20
MISC_KR = 32
MISC_KR_ROT = 64
VMEM_LIMIT_BYTES = 56 * 1024 * 1024


def _cparams(n_axes):
    return pltpu.CompilerParams(dimension_semantics=("arbitrary",) * n_axes,
                                vmem_limit_bytes=VMEM_LIMIT_BYTES)


def _dot(a, b):
    return jnp.dot(a.astype(BF16), b.astype(BF16), preferred_element_type=F32)


def _dot_nt(a, b):
    return lax.dot_general(a.astype(BF16), b.astype(BF16), (((1,), (1,)), ((), ())),
                           preferred_element_type=F32)


def _dot_tn(a, b):
    return lax.dot_general(a.astype(BF16), b.astype(BF16), (((0,), (0,)), ((), ())),
                           preferred_element_type=F32)


def _split(x):
    hi = x.astype(BF16)
    lo = (x - hi.astype(F32)).astype(BF16)
    return hi, lo


def _dot_sel(x, sel):
    hi, lo = _split(x)
    return (jnp.dot(hi, sel, preferred_element_type=F32) + jnp.dot(lo, sel, preferred_element_type=F32))


def _sel_dot(sel, x):
    hi, lo = _split(x)
    return (jnp.dot(sel, hi, preferred_element_type=F32) + jnp.dot(sel, lo, preferred_element_type=F32))


def _dot_tn_sel(x, sel):
    hi, lo = _split(x)
    dn = (((0,), (0,)), ((), ()))
    return (lax.dot_general(hi, sel, dn, preferred_element_type=F32)
            + lax.dot_general(lo, sel, dn, preferred_element_type=F32))


def _iota(shape, dim):
    return lax.broadcasted_iota(jnp.int32, shape, dim)


def _shr(x, n):
    return lax.shift_right_logical(x, jnp.int32(n))


def _head_mask(rows, cols, row_shift, col_shift):
    return _shr(_iota((rows, cols), 0), row_shift) == _shr(_iota((rows, cols), 1), col_shift)


def _head_ones(rows, cols, row_shift, col_shift):
    return jnp.where(_head_mask(rows, cols, row_shift, col_shift), 1.0, 0.0).astype(BF16)


def _tril_ones(n):
    return jnp.where(_iota((n, n), 1) <= _iota((n, n), 0), 1.0, 0.0).astype(BF16)


def _bd(x, row_shift=6, col_shift=6):
    t = jnp.concatenate([x] * N_HEADS, axis=0)
    return jnp.where(_head_mask(t.shape[0], t.shape[1], row_shift, col_shift), t, 0.0)


def _lane_j(shape):
    return jnp.bitwise_and(_iota(shape, 1), CHUNK - 1)


def _sigmoid(x):
    return 1.0 / (1.0 + jnp.exp(-x))


def _softplus(x):
    return jnp.maximum(x, 0.0) + jnp.log(1.0 + jnp.exp(-jnp.abs(x)))


def _shift_rows(x, prev8, s, row):
    if s == 0:
        return x
    rp = pltpu.roll(prev8, s, 0)
    rp_t = jnp.concatenate([rp] * (x.shape[0] // 8), axis=0)
    return jnp.where(row < s, rp_t, pltpu.roll(x, s, 0))


def _proj_kernel(x_ref, *refs):
    n = len(refs) // 2
    xb = x_ref[...].astype(BF16)
    for w_ref, o_ref in zip(refs[:n], refs[n:]):
        o_ref[...] = jnp.dot(xb, w_ref[...], preferred_element_type=F32)


def _project(x2d, ws, tm):
    t_tok, d = x2d.shape
    return pl.pallas_call(
        _proj_kernel,
        grid=(t_tok // tm,),
        in_specs=[pl.BlockSpec((tm, d), lambda i: (i, 0))]
        + [pl.BlockSpec(w.shape, lambda i: (0, 0)) for w in ws],
        out_specs=[pl.BlockSpec((tm, w.shape[1]), lambda i: (i, 0)) for w in ws],
        out_shape=[jax.ShapeDtypeStruct((t_tok, w.shape[1]), F32) for w in ws],
        compiler_params=_cparams(1),
        name="proj",
    )(x2d, *ws)


def _rwkv_kernel(u_ref, mu_ref, w0_ref, wup_ref, a0_ref, aup_ref, gup_ref, kk_ref, ka_ref, rk_ref,
                 gng_ref, gnb_ref, o_ref,
                 st_ref, prev_ref, r_s, lw_s, k_s, v_s, a_s, b_s, g_s, y_s):
    ct = u_ref.shape[1]
    gw = GROUP_WIDTH

    @pl.when(pl.program_id(1) == 0)
    def _():
        st_ref[...] = jnp.zeros_like(st_ref)
        prev_ref[...] = jnp.zeros_like(prev_ref)

    u = u_ref[0]
    row = _iota(u.shape, 0)
    prev = jnp.where(row == 0, prev_ref[0:1, :], pltpu.roll(u, 1, 0))
    prev_ref[0:1, :] = u_ref[0, ct - 1:ct, :]
    us = u + (prev - u) * mu_ref[...]
    r = us[:, 0:gw]
    k = us[:, gw:2 * gw]
    v = us[:, 2 * gw:3 * gw]
    misc = us[:, 3 * gw:]
    lnl = -_softplus(-(w0_ref[...] + _dot(jnp.tanh(misc), wup_ref[...]))) - 0.5
    a = _sigmoid(a0_ref[...] + _dot(misc, aup_ref[...]))
    ones_bd = _head_ones(gw, gw, 6, 6)
    kkp = k * kk_ref[...]
    kk = kkp / jnp.maximum(jnp.sqrt(_dot_sel(kkp * kkp, ones_bd)), 1e-12)
    k2 = k * (1.0 + (a - 1.0) * ka_ref[...])
    r_s[...] = r
    lw_s[...] = -jnp.exp(lnl)
    k_s[...] = k2
    v_s[...] = v
    a_s[...] = -kk
    b_s[...] = kk * a
    g_s[...] = _dot(_sigmoid(misc), gup_ref[...])

    def chunk_body(ci, carry):
        sl = pl.ds(pl.multiple_of(ci * CHUNK, CHUNK), CHUNK)
        r_c, lw_c, k_c, v_c, a_c, b_c = r_s[sl, :], lw_s[sl, :], k_s[sl, :], v_s[sl, :], a_s[sl, :], b_s[sl, :]
        st = st_ref[...]
        cs = _sel_dot(_tril_ones(CHUNK), lw_c)
        g_incl = jnp.exp(cs)
        g_prev = jnp.exp(cs - lw_c)
        g_inv = jnp.exp(-cs)
        lhs = jnp.concatenate([a_c * g_prev, r_c * g_incl], axis=0)
        sb = _dot_nt(lhs, _bd(b_c * g_inv))
        sk = _dot_nt(lhs, _bd(k_c * g_inv))
        shape = (CHUNK, gw)
        i_idx, j_idx = _iota(shape, 0), _lane_j(shape)
        strict, incl = j_idx < i_idx, j_idx <= i_idx
        a_ab = jnp.where(strict, sb[:CHUNK], 0.0)
        a_rb = jnp.where(incl, sb[CHUNK:], 0.0)
        a_k = jnp.concatenate([jnp.where(strict, sk[:CHUNK], 0.0), jnp.where(incl, sk[CHUNK:], 0.0)], axis=0)
        p = jnp.where(j_idx == i_idx, 1.0, 0.0) + a_ab
        ap = a_ab
        for _ in range(5):
            ap = _dot(ap, _bd(ap))
            p = p + _dot(p, _bd(ap))
        from_state = _dot(lhs, st)
        from_v = _dot(a_k, _bd(v_c))
        uu = _dot(p, _bd(from_state[:CHUNK] + from_v[:CHUNK]))
        y_s[sl, :] = from_state[CHUNK:] + from_v[CHUNK:] + _dot(a_rb, _bd(uu))
        tail = jnp.exp(cs[CHUNK - 1:CHUNK, :] - cs)
        upd = _dot_tn(jnp.concatenate([b_c * tail, k_c * tail], axis=0), jnp.concatenate([uu, v_c], axis=0))
        g_col = jnp.exp(_dot_tn_sel(lw_c, jnp.ones((CHUNK, gw), BF16)))
        st_ref[...] = g_col * st + jnp.where(_head_mask(gw, gw, 6, 6), upd, 0.0)
        return carry

    lax.fori_loop(0, ct // CHUNK, chunk_body, 0)

    y = y_s[...]
    mean = _dot_sel(y, ones_bd) * (1.0 / HEAD_DIM)
    yc = y - mean
    var = _dot_sel(yc * yc, ones_bd) * (1.0 / HEAD_DIM)
    yn = yc * lax.rsqrt(var + RWKV_GN_EPS) * gng_ref[...] + gnb_ref[...]
    bonus = _dot_sel(r_s[...] * k_s[...] * rk_ref[...], ones_bd) * v_s[...]
    o_ref[0] = ((yn + bonus) * g_s[...]).astype(o_ref.dtype)


def _rwkv(u_a, prm, ct):
    bsz, s, _ = u_a.shape
    gw = GROUP_WIDTH
    vec = lambda n: pl.BlockSpec((1, n), lambda b, c: (0, 0))
    mat = lambda m, n: pl.BlockSpec((m, n), lambda b, c: (0, 0))
    tile = lambda n: pltpu.VMEM((ct, n), F32)
    return pl.pallas_call(
        _rwkv_kernel,
        grid=(bsz, s // ct),
        in_specs=[pl.BlockSpec((1, ct, D_IN_A), lambda b, c: (b, c, 0)), vec(D_IN_A), vec(gw), mat(MISC, gw),
                  vec(gw), mat(MISC, gw), mat(MISC, gw), vec(gw), vec(gw), vec(gw), vec(gw), vec(gw)],
        out_specs=pl.BlockSpec((1, ct, gw), lambda b, c: (b, c, 0)),
        out_shape=jax.ShapeDtypeStruct((bsz, s, gw), BF16),
        scratch_shapes=[pltpu.VMEM((gw, gw), F32), pltpu.VMEM((8, D_IN_A), F32)] + [tile(gw)] * 8,
        compiler_params=_cparams(2),
        name="rwkv7",
    )(u_a, *prm)


def _gla_kernel(u_ref, m_ref, aup_ref, ab_ref, ng_ref, o_ref,
                st_ref, q_s, k_s, b_s, v_s, oi_s, ox_s):
    ct = u_ref.shape[1]
    qw, gw = GLA_QK_WIDTH, GROUP_WIDTH

    @pl.when(pl.program_id(1) == 0)
    def _():
        st_ref[...] = jnp.zeros_like(st_ref)

    u = u_ref[0]
    q_s[...] = u[:, 0:qw] * (GLA_KEY_DIM ** -0.5)
    k_s[...] = u[:, qw:2 * qw]
    v_s[...] = u[:, 2 * qw:2 * qw + gw]
    gate = u[:, 2 * qw + gw:]
    log_a = -_softplus(-(_dot(m_ref[0], aup_ref[...]) + ab_ref[...])) * (1.0 / GLA_TAU)
    head_mask = _head_mask(qw, gw, 5, 6)

    b_s[...] = log_a

    def chunk_body(ci, carry):
        sl = pl.ds(pl.multiple_of(ci * CHUNK, CHUNK), CHUNK)
        la_c = b_s[sl, :]
        q_c, k_c, v_c = q_s[sl, :], k_s[sl, :], v_s[sl, :]
        st = st_ref[...]
        b = _sel_dot(_tril_ones(CHUNK), la_c)
        b_s[sl, :] = b
        v_bd = _bd(v_c)
        j_idx = _lane_j((GLA_SUB, gw))
        parts = [jnp.zeros((GLA_SUB, gw), F32)]
        for blk in range(1, CHUNK // GLA_SUB):
            lo = blk * GLA_SUB
            beta = b[lo - 1:lo, :]
            q_blk = q_c[lo:lo + GLA_SUB, :] * jnp.exp(b[lo:lo + GLA_SUB, :] - beta)
            k_sc = k_c * jnp.exp(jnp.minimum(beta - b, 0.0))
            sc = jnp.where(j_idx < lo, _dot_nt(q_blk, _bd(k_sc, 6, 5)), 0.0)
            parts.append(_dot(sc, v_bd))
        ox_s[sl, :] = _dot(q_c * jnp.exp(b), st) + jnp.concatenate(parts, axis=0)
        tail = jnp.exp(b[CHUNK - 1:CHUNK, :] - b)
        g_col = jnp.exp(_dot_tn_sel(la_c, jnp.ones((CHUNK, gw), BF16)))
        st_ref[...] = g_col * st + jnp.where(head_mask, _dot_tn(k_c * tail, v_c), 0.0)
        return carry

    lax.fori_loop(0, ct // CHUNK, chunk_body, 0)

    seg_bcast = jnp.where(head_mask, 1.0, 0.0).astype(BF16)

    def sub_body(sb, carry):
        c0 = pl.multiple_of(sb * GLA_SUB, GLA_SUB)
        sl = pl.ds(c0, GLA_SUB)
        q_t = jnp.concatenate([q_s[sl, :]] * GLA_SUB, axis=0)
        b_t = jnp.concatenate([b_s[sl, :]] * GLA_SUB, axis=0)
        rep = lambda ref, w: jnp.concatenate(
            [jnp.broadcast_to(ref[pl.ds(c0 + j, 1), :], (GLA_SUB, w)) for j in range(GLA_SUB)], axis=0)
        k_r, b_r = rep(k_s, qw), rep(b_s, qw)
        rows = _iota((GLA_SUB * GLA_SUB, qw), 0)
        valid = jnp.bitwise_and(rows, GLA_SUB - 1) >= _shr(rows, 4)
        p = jnp.where(valid, q_t * k_r * jnp.exp(jnp.minimum(b_t - b_r, 0.0)), 0.0)
        sx = jnp.dot(p.astype(BF16), seg_bcast, preferred_element_type=F32)
        o = jnp.zeros((GLA_SUB, gw), F32)
        for j in range(GLA_SUB):
            o = o + sx[j * GLA_SUB:(j + 1) * GLA_SUB, :] * v_s[pl.ds(c0 + j, 1), :]
        oi_s[sl, :] = o
        return carry

    lax.fori_loop(0, ct // GLA_SUB, sub_body, 0, unroll=2)

    o = oi_s[...] + ox_s[...]
    ms = _dot_sel(o * o, _head_ones(gw, gw, 6, 6)) * (1.0 / HEAD_DIM)
    on = o * lax.rsqrt(ms + NORM_EPS) * ng_ref[...]
    o_ref[0] = (on * (gate * _sigmoid(gate))).astype(o_ref.dtype)


def _gla(u_b, misc, prm, ct):
    bsz, s, wb = u_b.shape
    qw, gw = GLA_QK_WIDTH, GROUP_WIDTH
    return pl.pallas_call(
        _gla_kernel,
        grid=(bsz, s // ct),
        in_specs=[pl.BlockSpec((1, ct, wb), lambda b, c: (b, c, 0)),
                  pl.BlockSpec((1, ct, MISC), lambda b, c: (b, c, 0)),
                  pl.BlockSpec((MISC, qw), lambda b, c: (0, 0)),
                  pl.BlockSpec((1, qw), lambda b, c: (0, 0)),
                  pl.BlockSpec((1, gw), lambda b, c: (0, 0))],
        out_specs=pl.BlockSpec((1, ct, gw), lambda b, c: (b, c, 0)),
        out_shape=jax.ShapeDtypeStruct((bsz, s, gw), BF16),
        scratch_shapes=[pltpu.VMEM((qw, gw), F32), pltpu.VMEM((ct, qw), F32), pltpu.VMEM((ct, qw), F32),
                        pltpu.VMEM((ct, qw), F32), pltpu.VMEM((ct, gw), F32), pltpu.VMEM((ct, gw), F32),
                        pltpu.VMEM((ct, gw), F32)],
        compiler_params=_cparams(2),
        name="gla",
    )(u_b, misc, *prm)


def _mlstm_kernel(u_ref, m_ref, cw_ref, cb_ref, eig_ref, efg_ref, ib_ref, fb_ref, ng_ref, o_ref,
                  mem_ref, nb_ref, mst_ref, prev_ref, q_s, k_s, v_s, li_s, lf_s, h_s):
    ct = u_ref.shape[1]
    gw = GROUP_WIDTH

    @pl.when(pl.program_id(1) == 0)
    def _():
        mem_ref[...] = jnp.zeros_like(mem_ref)
        nb_ref[...] = jnp.zeros_like(nb_ref)
        mst_ref[...] = jnp.zeros_like(mst_ref)
        prev_ref[...] = jnp.zeros_like(prev_ref)

    u = u_ref[0]
    qk = u[:, 0:2 * gw]
    row = _iota(qk.shape, 0)
    prev8 = prev_ref[...]
    conv = cb_ref[...] + jnp.zeros_like(qk)
    for j in range(MLSTM_CONV):
        conv = conv + cw_ref[j:j + 1, :] * _shift_rows(qk, prev8, MLSTM_CONV - 1 - j, row)
    prev_ref[...] = u_ref[0, ct - 8:ct, 0:2 * gw]
    qk_act = conv * _sigmoid(conv)
    q_s[...] = qk_act[:, 0:gw]
    k_s[...] = qk_act[:, gw:] * (HEAD_DIM ** -0.5)
    v_s[...] = u[:, 2 * gw:3 * gw]
    o_gate = u[:, 3 * gw:]
    misc = m_ref[0]
    li_s[...] = _dot_sel(misc, eig_ref[...]) + ib_ref[...]
    lf_s[...] = -_softplus(-(_dot_sel(misc, efg_ref[...]) + fb_ref[...]))
    neg_inf = -jnp.inf

    def chunk_body(ci, carry):
        sl = pl.ds(pl.multiple_of(ci * CHUNK, CHUNK), CHUNK)
        q_c, k_c, v_c, li_c, lf_c = q_s[sl, :], k_s[sl, :], v_s[sl, :], li_s[sl, :], lf_s[sl, :]
        mem, nb, m_e = mem_ref[...], nb_ref[...], mst_ref[0:1, :]
        shape = (CHUNK, gw)
        i_idx, j_idx = _iota(shape, 0), _lane_j(shape)
        incl = j_idx <= i_idx
        head_mask = _head_mask(gw, gw, 6, 6)
        fc = _sel_dot(_tril_ones(CHUNK), lf_c)
        g_row = li_c - fc
        ones_avg = jnp.full(shape, 1.0 / HEAD_DIM, BF16)
        g_hi, g_lo = _split(_bd(g_row))
        dn = (((1,), (1,)), ((), ()))
        g_t = (lax.dot_general(ones_avg, g_hi, dn, preferred_element_type=F32)
               + lax.dot_general(ones_avg, g_lo, dn, preferred_element_type=F32))
        log_w = jnp.where(incl, fc + g_t, neg_inf)
        cm = g_row
        for sh in (1, 2, 4, 8, 16, 32):
            cm = jnp.maximum(cm, jnp.where(i_idx < sh, neg_inf, pltpu.roll(cm, sh, 0)))
        log_carry = fc + m_e
        m_row = jnp.maximum(fc + cm, log_carry)
        sc = _dot_nt(q_c, _bd(k_c)) * jnp.exp(log_w - m_row)
        w_c = jnp.exp(log_carry - m_row)
        scb = sc.astype(BF16)
        num = jnp.dot(scb, _bd(v_c).astype(BF16), preferred_element_type=F32) + w_c * _dot(q_c, mem)
        den = (jnp.dot(scb, _head_ones(gw, gw, 6, 6), preferred_element_type=F32) + w_c * _dot(q_c, nb))
        h_s[sl, :] = num / jnp.maximum(jnp.abs(den), jnp.exp(-m_row))
        f_end = fc[CHUNK - 1:CHUNK, :]
        log_kv = f_end - fc + li_c
        m_new = jnp.maximum(f_end + m_e, jnp.max(log_kv, axis=0, keepdims=True))
        kwk = jnp.exp(log_kv - m_new) * k_c
        cd = jnp.exp(f_end + m_e - m_new)
        cd_rows = jnp.where(_iota((16, gw), 0) == 0, cd, 0.0)
        cd_col = _dot_tn_sel(cd_rows, jnp.ones((16, gw), BF16))
        upd = _dot_tn(kwk, jnp.concatenate([v_c, jnp.ones(shape, F32)], axis=1))
        mem_ref[...] = cd_col * mem + jnp.where(head_mask, upd[:, :gw], 0.0)
        nb_ref[...] = cd_col * nb + jnp.where(head_mask, upd[:, gw:], 0.0)
        mst_ref[0:1, :] = m_new
        return carry

    lax.fori_loop(0, ct // CHUNK, chunk_body, 0)

    h = h_s[...]
    ones_bd = _head_ones(gw, gw, 6, 6)
    mean = _dot_sel(h, ones_bd) * (1.0 / HEAD_DIM)
    hc = h - mean
    var = _dot_sel(hc * hc, ones_bd) * (1.0 / HEAD_DIM)
    hn = hc * lax.rsqrt(var + LN_EPS) * ng_ref[...]
    o_ref[0] = (hn * _sigmoid(o_gate)).astype(o_ref.dtype)


def _mlstm(u_c, misc, prm, ct):
    bsz, s, wc = u_c.shape
    gw = GROUP_WIDTH
    vec = lambda n: pl.BlockSpec((1, n), lambda b, c: (0, 0))
    mat = lambda m, n: pl.BlockSpec((m, n), lambda b, c: (0, 0))
    tile = pltpu.VMEM((ct, gw), F32)
    return pl.pallas_call(
        _mlstm_kernel,
        grid=(bsz, s // ct),
        in_specs=[pl.BlockSpec((1, ct, wc), lambda b, c: (b, c, 0)),
                  pl.BlockSpec((1, ct, MISC), lambda b, c: (b, c, 0)),
                  mat(MLSTM_CONV, 2 * gw), vec(2 * gw), mat(MISC, gw), mat(MISC, gw), vec(gw), vec(gw), vec(gw)],
        out_specs=pl.BlockSpec((1, ct, gw), lambda b, c: (b, c, 0)),
        out_shape=jax.ShapeDtypeStruct((bsz, s, gw), BF16),
        scratch_shapes=[pltpu.VMEM((gw, gw), F32), pltpu.VMEM((gw, gw), F32), pltpu.VMEM((8, gw), F32),
                        pltpu.VMEM((8, 2 * gw), F32)] + [tile] * 6,
        compiler_params=_cparams(2),
        name="mlstm",
    )(u_c, misc, *prm)


def _mla_prep_kernel(u_ref, m_ref, cq_ref, sq_ref, csk_ref, qg_ref, kvg_ref, wq_ref, wqr_ref, wk_ref, wv_ref,
                     selk_ref, q_ref, k_ref, v_ref):
    u = u_ref[0]
    cq = u[:, 0:MLA_Q_RANK]
    ckv = u[:, MLA_Q_RANK:]
    cqn = cq * lax.rsqrt(jnp.mean(cq * cq, axis=-1, keepdims=True) + NORM_EPS) * qg_ref[...]
    ckvn = ckv * lax.rsqrt(jnp.mean(ckv * ckv, axis=-1, keepdims=True) + NORM_EPS) * kvg_ref[...]
    cqb, ckvb = cqn.astype(BF16), ckvn.astype(BF16)
    k_rope = _dot(m_ref[0] * csk_ref[0], selk_ref[...])
    cos_q, sin_q = cq_ref[0], sq_ref[0]
    scale = MLA_QK ** -0.5
    for h in range(N_HEADS):
        qh = (jnp.dot(cqb, wq_ref[h], preferred_element_type=F32) * cos_q
              + jnp.dot(cqb, wqr_ref[h], preferred_element_type=F32) * sin_q) * scale
        q_ref[0, h] = qh.astype(BF16)
        k_ref[0, h] = (jnp.dot(ckvb, wk_ref[h], preferred_element_type=F32) + k_rope).astype(BF16)
        v_ref[0, h] = jnp.dot(ckvb, wv_ref[h], preferred_element_type=F32).astype(BF16)


def _mla_prep(u_d, misc, tabs, prm, tm):
    bsz, s, wd = u_d.shape
    full = lambda a: pl.BlockSpec(a.shape, lambda b, c: (0,) * a.ndim)
    tok = lambda n: pl.BlockSpec((1, tm, n), lambda b, c: (b, c, 0))
    head = lambda n: pl.BlockSpec((1, N_HEADS, tm, n), lambda b, c: (b, 0, c, 0))
    return pl.pallas_call(
        _mla_prep_kernel,
        grid=(bsz, s // tm),
        in_specs=[tok(wd), tok(MISC), tok(MLA_QK), tok(MLA_QK), tok(MISC)] + [full(a) for a in prm],
        out_specs=[head(MLA_QK), head(MLA_QK), head(HEAD_DIM)],
        out_shape=[jax.ShapeDtypeStruct((bsz, N_HEADS, s, MLA_QK), BF16),
                   jax.ShapeDtypeStruct((bsz, N_HEADS, s, MLA_QK), BF16),
                   jax.ShapeDtypeStruct((bsz, N_HEADS, s, HEAD_DIM), BF16)],
        compiler_params=_cparams(2),
        name="mla_prep",
    )(u_d, misc, *tabs, *prm)


def _attn_kernel(q_ref, k_ref, v_ref, o_ref):
    tq = q_ref.shape[2]
    qi = pl.program_id(2)
    q = q_ref[0, 0]

    def step(j, carry, diagonal):
        m, l, acc = carry
        sl = pl.ds(pl.multiple_of(j * tq, tq), tq)
        kb, vb = k_ref[0, 0, sl, :], v_ref[0, 0, sl, :]
        s = lax.dot_general(q, kb, (((1,), (1,)), ((), ())), preferred_element_type=F32)
        if diagonal:
            s = jnp.where(_iota((tq, tq), 1) <= _iota((tq, tq), 0), s, -jnp.inf)
        m_new = jnp.maximum(m, jnp.max(s, axis=-1, keepdims=True))
        p = jnp.exp(s - m_new)
        alpha = jnp.exp(m - m_new)
        l = alpha * l + jnp.sum(p, axis=-1, keepdims=True)
        acc = alpha * acc + jnp.dot(p.astype(BF16), vb, preferred_element_type=F32)
        return m_new, l, acc

    init = (jnp.full((tq, 1), -jnp.inf, F32), jnp.zeros((tq, 1), F32), jnp.zeros((tq, HEAD_DIM), F32))
    carry = lax.fori_loop(0, qi, lambda j, c: step(j, c, False), init)
    _, l, acc = step(qi, carry, True)
    o_ref[0, 0] = (acc / l).astype(o_ref.dtype)


def _attention(q, k, v, tq):
    bsz, nh, s, _ = q.shape
    return pl.pallas_call(
        _attn_kernel,
        grid=(bsz, nh, s // tq),
        in_specs=[pl.BlockSpec((1, 1, tq, MLA_QK), lambda b, h, i: (b, h, i, 0)),
                  pl.BlockSpec((1, 1, s, MLA_QK), lambda b, h, i: (b, h, 0, 0)),
                  pl.BlockSpec((1, 1, s, HEAD_DIM), lambda b, h, i: (b, h, 0, 0))],
        out_specs=pl.BlockSpec((1, 1, tq, HEAD_DIM), lambda b, h, i: (b, h, i, 0)),
        out_shape=jax.ShapeDtypeStruct((bsz, nh, s, HEAD_DIM), BF16),
        compiler_params=_cparams(3),
        name="mla_attn",
    )(q, k, v)


def _layer_norm(z, g, b):
    zc = z - jnp.mean(z, axis=-1, keepdims=True)
    return zc * lax.rsqrt(jnp.mean(zc * zc, axis=-1, keepdims=True) + LN_EPS) * g + b


def _outproj_kernel(x_ref, ya_ref, yb_ref, yc_ref, yd_ref, wo_ref, wod_ref, g_ref, b_ref, o_ref):
    gw = GROUP_WIDTH
    mix = jnp.dot(ya_ref[0], wo_ref[0:gw, :], preferred_element_type=F32)
    mix = mix + jnp.dot(yb_ref[0], wo_ref[gw:2 * gw, :], preferred_element_type=F32)
    mix = mix + jnp.dot(yc_ref[0], wo_ref[2 * gw:3 * gw, :], preferred_element_type=F32)
    for h in range(N_HEADS):
        mix = mix + jnp.dot(yd_ref[0, h], wod_ref[h], preferred_element_type=F32)
    o_ref[0] = _layer_norm(DN_ALPHA * x_ref[0] + mix, g_ref[...], b_ref[...])


def _outproj(x, ya, yb, yc, yd, wo, wod, g, b, tm):
    bsz, s, d = x.shape
    gw = GROUP_WIDTH
    tok = lambda n: pl.BlockSpec((1, tm, n), lambda bb, c: (bb, c, 0))
    full = lambda a: pl.BlockSpec(a.shape, lambda bb, c: (0,) * a.ndim)
    return pl.pallas_call(
        _outproj_kernel,
        grid=(bsz, s // tm),
        in_specs=[tok(d), tok(gw), tok(gw), tok(gw),
                  pl.BlockSpec((1, N_HEADS, tm, HEAD_DIM), lambda bb, c: (bb, 0, c, 0)),
                  full(wo), full(wod), full(g), full(b)],
        out_specs=tok(d),
        out_shape=jax.ShapeDtypeStruct((bsz, s, d), F32),
        compiler_params=_cparams(2),
        name="outproj_ln1",
    )(x, ya, yb, yc, yd, wo, wod, g, b)


def _router_kernel(x_ref, w_ref, b_ref, idx_ref, gate_ref):
    x = x_ref[...]
    xh, xl = _split(x)
    wh, wl = _split(w_ref[...])
    dn = (((1,), (1,)), ((), ()))
    logits = (lax.dot_general(wh, xh, dn, preferred_element_type=F32)
              + lax.dot_general(wh, xl, dn, preferred_element_type=F32)
              + lax.dot_general(wl, xh, dn, preferred_element_type=F32)) + b_ref[...]
    tm = x.shape[0]
    ng, ne = N_EXPERT_GROUPS, EXPERTS_PER_GROUP
    gl = [logits[i:i + 1, :] for i in range(ng)]
    gmax = functools.reduce(jnp.maximum, gl)
    gsum = functools.reduce(jnp.add, [jnp.exp(g - gmax) for g in gl])
    group_p = 1.0 / gsum
    gidx = jnp.full((1, tm), ng - 1, jnp.int32)
    for i in range(ng - 2, -1, -1):
        gidx = jnp.where(gl[i] == gmax, i, gidx)
    el = logits[8 + (ng - 1) * ne:8 + ng * ne, :]
    for i in range(ng - 2, -1, -1):
        el = jnp.where(gidx == i, logits[8 + i * ne:8 + (i + 1) * ne, :], el)
    rows = _iota((ne, tm), 0)
    m1 = jnp.max(el, axis=0, keepdims=True)
    i1 = jnp.min(jnp.where(el == m1, rows, ne), axis=0, keepdims=True)
    el2 = jnp.where(rows == i1, -jnp.inf, el)
    m2 = jnp.max(el2, axis=0, keepdims=True)
    i2 = jnp.min(jnp.where(el2 == m2, rows, ne), axis=0, keepdims=True)
    z = jnp.sum(jnp.exp(el - m1), axis=0, keepdims=True)
    p1 = 1.0 / z
    p2 = jnp.exp(m2 - m1) / z
    base = gidx * ne
    idx_ref[...] = jnp.concatenate([base + i1, base + i2], axis=0)
    gate_ref[...] = jnp.concatenate([group_p * p1 / (p1 + p2), group_p * p2 / (p1 + p2)], axis=0)


def _router(x2d, w_t, b_col, tm):
    t_tok, d = x2d.shape
    return pl.pallas_call(
        _router_kernel,
        grid=(t_tok // tm,),
        in_specs=[pl.BlockSpec((tm, d), lambda i: (i, 0)), pl.BlockSpec(w_t.shape, lambda i: (0, 0)),
                  pl.BlockSpec(b_col.shape, lambda i: (0, 0))],
        out_specs=[pl.BlockSpec((TOP_K, tm), lambda i: (0, i)), pl.BlockSpec((TOP_K, tm), lambda i: (0, i))],
        out_shape=[jax.ShapeDtypeStruct((TOP_K, t_tok), jnp.int32), jax.ShapeDtypeStruct((TOP_K, t_tok), F32)],
        compiler_params=_cparams(1),
        name="router",
    )(x2d, w_t, b_col)


def _moe_kernel(be_ref, x_ref, gate_ref, wg_ref, wu_ref, wd_ref, o_ref, wg_s, wu_s, wd_s):
    i = pl.program_id(0)

    @pl.when(jnp.logical_or(i == 0, be_ref[i] != be_ref[jnp.maximum(i - 1, 0)]))
    def _():
        wg_s[...] = wg_ref[0].astype(BF16)
        wu_s[...] = wu_ref[0].astype(BF16)
        wd_s[...] = wd_ref[0].astype(BF16)

    xb = x_ref[...]
    hg = jnp.dot(xb, wg_s[...], preferred_element_type=F32)
    hu = jnp.dot(xb, wu_s[...], preferred_element_type=F32)
    hid = (hg * _sigmoid(hg)) * hu
    y = jnp.dot(hid.astype(BF16), wd_s[...], preferred_element_type=F32) * gate_ref[...]
    o_ref[...] = y.astype(o_ref.dtype)


def _moe(x_rows, block_expert, row_gate, wg, wu, wd, layer):
    n_rows, d = x_rows.shape
    n_blocks = n_rows // MOE_BLOCK
    grid_spec = pltpu.PrefetchScalarGridSpec(
        num_scalar_prefetch=1,
        grid=(n_blocks,),
        in_specs=[pl.BlockSpec((MOE_BLOCK, d), lambda i, be: (i, 0)),
                  pl.BlockSpec((MOE_BLOCK, 1), lambda i, be: (i, 0)),
                  pl.BlockSpec((None, 1, d, D_EXPERT), lambda i, be: (layer, be[i], 0, 0)),
                  pl.BlockSpec((None, 1, d, D_EXPERT), lambda i, be: (layer, be[i], 0, 0)),
                  pl.BlockSpec((None, 1, D_EXPERT, d), lambda i, be: (layer, be[i], 0, 0))],
        out_specs=pl.BlockSpec((MOE_BLOCK, d), lambda i, be: (i, 0)),
        scratch_shapes=[pltpu.VMEM((d, D_EXPERT), BF16), pltpu.VMEM((d, D_EXPERT), BF16),
                        pltpu.VMEM((D_EXPERT, d), BF16)],
    )
    return pl.pallas_call(
        _moe_kernel,
        grid_spec=grid_spec,
        out_shape=jax.ShapeDtypeStruct((n_rows, d), BF16),
        compiler_params=_cparams(1),
        name="moe_experts",
    )(block_expert, x_rows, row_gate.reshape(n_rows, 1), wg, wu, wd)


def _dispatch(idx, gate, t_tok):
    n_asg = t_tok * TOP_K
    flat_e = idx.reshape(n_asg)
    flat_g = gate.reshape(n_asg)
    order = jnp.argsort(flat_e)
    inverse = jnp.argsort(order).astype(jnp.int32)
    starts = jnp.searchsorted(flat_e[order], jnp.arange(N_EXPERTS + 1, dtype=jnp.int32), side='left')
    starts = starts.astype(jnp.int32)
    counts = starts[1:] - starts[:-1]
    starts = starts[:-1]
    padded = (counts + MOE_BLOCK - 1) // MOE_BLOCK * MOE_BLOCK
    pad_ends = jnp.cumsum(padded)
    pad_starts = pad_ends - padded
    n_rows = -(-(n_asg + N_EXPERTS * (MOE_BLOCK - 1)) // MOE_BLOCK) * MOE_BLOCK
    n_blocks = n_rows // MOE_BLOCK
    block_expert = jnp.minimum(jnp.searchsorted(pad_ends, jnp.arange(n_blocks) * MOE_BLOCK, side='right'),
                               N_EXPERTS - 1).astype(jnp.int32)
    pos = pad_starts[flat_e] + inverse - starts[flat_e]
    row_e = jnp.repeat(block_expert, MOE_BLOCK)
    row_rank = jnp.arange(n_rows, dtype=jnp.int32) - pad_starts[row_e]
    valid = row_rank < counts[row_e]
    asg = order[jnp.where(valid, starts[row_e] + row_rank, 0)]
    row_tok = jnp.where(valid, asg % t_tok, 0).astype(jnp.int32)
    row_gate = jnp.where(valid, flat_g[asg], 0.0)
    return block_expert, row_tok, pos.reshape(TOP_K, t_tok), row_gate


def _final_kernel(x_ref, p_ref, f0_ref, f1_ref, wg_ref, bg_ref, wp_ref, g_ref, b_ref, o_ref):
    x = x_ref[...]
    gate = _sigmoid(_dot(x, wg_ref[...]) + bg_ref[...])
    ple = gate * _dot(p_ref[...], wp_ref[...])
    ffn = f0_ref[...].astype(F32) + f1_ref[...].astype(F32)
    o_ref[...] = _layer_norm(DN_ALPHA * x + ffn + ple, g_ref[...], b_ref[...])


def _final(x2d, p2d, ffn0, ffn1, wg, bg, wp, g, b, tm):
    t_tok, d = x2d.shape
    full = lambda a: pl.BlockSpec(a.shape, lambda i: (0,) * a.ndim)
    return pl.pallas_call(
        _final_kernel,
        grid=(t_tok // tm,),
        in_specs=[pl.BlockSpec((tm, d), lambda i: (i, 0)), pl.BlockSpec((tm, p2d.shape[1]), lambda i: (i, 0)),
                  pl.BlockSpec((tm, d), lambda i: (i, 0)), pl.BlockSpec((tm, d), lambda i: (i, 0)),
                  full(wg), full(bg), full(wp), full(g), full(b)],
        out_specs=pl.BlockSpec((tm, d), lambda i: (i, 0)),
        out_shape=jax.ShapeDtypeStruct((t_tok, d), F32),
        compiler_params=_cparams(1),
        name="ple_ln2",
    )(x2d, p2d, ffn0, ffn1, wg, bg, wp, g, b)


def _pad_rows(w, offset, total=MISC):
    return jnp.zeros((total, w.shape[1]), w.dtype).at[offset:offset + w.shape[0]].set(w)


def _rot_cols(w):
    half = w.shape[1] // 2
    return jnp.concatenate([-w[:, half:], w[:, :half]], axis=1)


def _row(v):
    return v.reshape(1, -1)


def _expand_heads(v):
    return jnp.repeat(v, HEAD_DIM).reshape(1, GROUP_WIDTH)


def _head_select(offset):
    sel = np.zeros((MISC, GROUP_WIDTH), np.float32)
    for h in range(N_HEADS):
        sel[offset + h, h * HEAD_DIM:(h + 1) * HEAD_DIM] = 1.0
    return jnp.asarray(sel, BF16)


def _rope_tables(positions):
    inv_freq = ROPE_THETA ** (-jnp.arange(0, MLA_ROPE, 2, dtype=F32) / MLA_ROPE)
    ang = positions.astype(F32)[..., None] * inv_freq
    cos, sin = jnp.cos(ang), jnp.sin(ang)
    cos2 = jnp.concatenate([cos, cos], axis=-1)
    sin2 = jnp.concatenate([sin, sin], axis=-1)
    lead = cos.shape[:-1]
    cos_q = jnp.concatenate([jnp.ones(lead + (MLA_NOPE,), F32), cos2], axis=-1)
    sin_q = jnp.concatenate([jnp.zeros(lead + (MLA_NOPE,), F32), sin2], axis=-1)
    cs_k = jnp.concatenate([jnp.zeros(lead + (MISC_KR,), F32), cos2, sin2,
                            jnp.zeros(lead + (MISC - MISC_KR_ROT - MLA_ROPE,), F32)], axis=-1)
    return cos_q, sin_q, cs_k


def _rope_key_select():
    sel = np.zeros((MISC, MLA_QK), np.float32)
    for c in range(MLA_ROPE):
        sel[MISC_KR + c, MLA_NOPE + c] = 1.0
        sel[MISC_KR_ROT + c, MLA_NOPE + c] = 1.0
    return jnp.asarray(sel, BF16)


def _split_w_in(w):
    a0, b0 = 0, D_IN_A
    c0, d0 = b0 + D_IN_B, b0 + D_IN_B + D_IN_C
    qw, gw = GLA_QK_WIDTH, GROUP_WIDTH
    w_a = w[:, a0:b0]
    gla_ad = w[:, b0 + 2 * qw + gw:b0 + 2 * qw + gw + GLA_GATE_RANK]
    w_b = jnp.concatenate([w[:, b0:b0 + 2 * qw + gw], w[:, b0 + 2 * qw + gw + GLA_GATE_RANK:c0]], axis=1)
    w_c = w[:, c0:c0 + 4 * gw]
    gates = w[:, c0 + 4 * gw:d0]
    w_d = w[:, d0:d0 + MLA_Q_RANK + MLA_KV_RANK]
    kr = w[:, d0 + MLA_Q_RANK + MLA_KV_RANK:]
    z = lambda n: jnp.zeros((w.shape[0], n), w.dtype)
    w_m = jnp.concatenate([gla_ad, gates, z(MISC_KR - MISC_FG - N_HEADS), kr, _rot_cols(kr),
                           z(MISC - MISC_KR_ROT - MLA_ROPE)], axis=1)
    return [t.astype(BF16) for t in (w_a, w_b, w_c, w_d, w_m)]


def _tiles(s):
    return min(512, s), min(512, s)


def kernel(x, p, positions, w_in, rwkv_mu, rwkv_w0, rwkv_w_up, rwkv_a0, rwkv_a_up, rwkv_g_up, rwkv_k_k, rwkv_k_a, rwkv_r_k, rwkv_gn_g, rwkv_gn_b, gla_alpha_up, gla_alpha_b, gla_norm_g, mlstm_conv_w, mlstm_conv_b, mlstm_i_b, mlstm_f_b, mlstm_norm_g, mla_q_norm_g, mla_w_uq, mla_kv_norm_g, mla_w_ukv, w_out, ln1_g, ln1_b, moe_w_rg, moe_b_rg, moe_w_re, moe_b_re, moe_w_gate, moe_w_up, moe_w_down, ple_w_gate, ple_b_gate, ple_w, ln2_g, ln2_b):
    bsz, s, d = x.shape
    t_tok = bsz * s
    depth = w_in.shape[0]
    ct, tq = _tiles(s)
    tm = min(512, t_tok)
    rope_tabs = _rope_tables(positions)
    sel_ig, sel_fg, sel_k = _head_select(MISC_IG), _head_select(MISC_FG), _rope_key_select()
    for i in range(depth):
        u_a, u_b, u_c, u_d, u_m = _project(x.reshape(t_tok, d), _split_w_in(w_in[i]), tm)
        u_a, u_b, u_c, u_d, u_m = (t.reshape(bsz, s, -1) for t in (u_a, u_b, u_c, u_d, u_m))

        rk = RWKV_DECAY_RANK
        ya = _rwkv(u_a, (_row(rwkv_mu[i]), _row(rwkv_w0[i]), _pad_rows(rwkv_w_up[i], 0).astype(BF16),
                         _row(rwkv_a0[i]), _pad_rows(rwkv_a_up[i], rk).astype(BF16),
                         _pad_rows(rwkv_g_up[i], rk + RWKV_ICL_RANK).astype(BF16),
                         _row(rwkv_k_k[i]), _row(rwkv_k_a[i]), _row(rwkv_r_k[i]),
                         _row(rwkv_gn_g[i]), _row(rwkv_gn_b[i])), ct)
        yb = _gla(u_b, u_m, (_pad_rows(gla_alpha_up[i], MISC_GLA_AD).astype(BF16), _row(gla_alpha_b[i]),
                             _row(gla_norm_g[i])), ct)
        yc = _mlstm(u_c, u_m, (mlstm_conv_w[i], _row(mlstm_conv_b[i]), sel_ig, sel_fg,
                               _expand_heads(mlstm_i_b[i]), _expand_heads(mlstm_f_b[i]),
                               _row(mlstm_norm_g[i])), ct)

        wq = mla_w_uq[i].reshape(MLA_Q_RANK, N_HEADS, MLA_QK).transpose(1, 0, 2)
        wq_rot = jnp.concatenate([jnp.zeros((N_HEADS, MLA_Q_RANK, MLA_NOPE), F32),
                                  jax.vmap(_rot_cols)(wq[:, :, MLA_NOPE:])], axis=2)
        wkv = mla_w_ukv[i].reshape(MLA_KV_RANK, N_HEADS, MLA_NOPE + HEAD_DIM).transpose(1, 0, 2)
        wk = jnp.concatenate([wkv[:, :, :MLA_NOPE], jnp.zeros((N_HEADS, MLA_KV_RANK, MLA_ROPE), F32)], axis=2)
        q, k, v = _mla_prep(u_d, u_m, rope_tabs,
                            (_row(mla_q_norm_g[i]), _row(mla_kv_norm_g[i]), wq.astype(BF16), wq_rot.astype(BF16),
                             wk.astype(BF16), wkv[:, :, MLA_NOPE:].astype(BF16), sel_k), ct)
        yd = _attention(q, k, v, tq)

        wo = w_out[i].astype(BF16)
        x = _outproj(x, ya, yb, yc, yd, wo[:3 * GROUP_WIDTH], wo[3 * GROUP_WIDTH:].reshape(N_HEADS, HEAD_DIM, d),
                     _row(ln1_g[i]), _row(ln1_b[i]), ct)

        x2d = x.reshape(t_tok, d)
        w_route = jnp.concatenate([moe_w_rg[i].T, jnp.zeros((8 - N_EXPERT_GROUPS, d), F32), moe_w_re[i].T], axis=0)
        b_route = jnp.concatenate([moe_b_rg[i], jnp.zeros((8 - N_EXPERT_GROUPS,), F32), moe_b_re[i]]).reshape(-1, 1)
        idx, gate = _router(x2d, w_route, b_route, tm)
        block_expert, row_tok, pos, row_gate = _dispatch(idx, gate, t_tok)
        y_rows = _moe(x2d.astype(BF16)[row_tok], block_expert, row_gate, moe_w_gate, moe_w_up, moe_w_down, i)
        x = _final(x2d, p[i].reshape(t_tok, -1), y_rows[pos[0]], y_rows[pos[1]], ple_w_gate[i].astype(BF16),
                   _row(ple_b_gate[i]), ple_w[i].astype(BF16), _row(ln2_g[i]), _row(ln2_b[i]), tm).reshape(bsz, s, d)
    return x
```

```python
import functools

import jax
import jax.numpy as jnp
import numpy as np
from jax import lax
from jax.experimental import pallas as pl
from jax.experimental.pallas import tpu as pltpu

F32 = jnp.float32
BF16 = jnp.bfloat16

D_MODEL = 1024
DEPTH = 4
HEAD_DIM = 64
N_HEADS = 4
GROUP_WIDTH = N_HEADS * HEAD_DIM
D_PLE = 256

RWKV_DECAY_RANK = 32
RWKV_ICL_RANK = 32
RWKV_GATE_RANK = 64
RWKV_GN_EPS = 64e-5
D_IN_A = 3 * GROUP_WIDTH + RWKV_DECAY_RANK + RWKV_ICL_RANK + RWKV_GATE_RANK

GLA_KEY_DIM = 32
GLA_QK_WIDTH = N_HEADS * GLA_KEY_DIM
GLA_GATE_RANK = 16
GLA_TAU = 16.0
D_IN_B = 2 * GLA_QK_WIDTH + 2 * GROUP_WIDTH + GLA_GATE_RANK

MLSTM_CONV = 4
D_IN_C = 4 * GROUP_WIDTH + 2 * N_HEADS

MLA_Q_RANK = 256
MLA_KV_RANK = 128
MLA_NOPE = 64
MLA_ROPE = 32
MLA_QK = MLA_NOPE + MLA_ROPE
D_IN_D = MLA_Q_RANK + MLA_KV_RANK + MLA_ROPE
ROPE_THETA = 10000.0

N_EXPERT_GROUPS = 4
EXPERTS_PER_GROUP = 8
N_EXPERTS = N_EXPERT_GROUPS * EXPERTS_PER_GROUP
TOP_K = 2
D_EXPERT = 512
MOE_BLOCK = 256

DN_ALPHA = (2.0 * DEPTH) ** 0.25
LN_EPS = 1e-5
NORM_EPS = 1e-6

CHUNK = 64
GLA_SUB = 16
MISC = 128
MISC_GLA_AD = 0
MISC_IG = 16
MISC_FG = 20
MISC_KR = 32
MISC_KR_ROT = 64
VMEM_LIMIT_BYTES = 56 * 1024 * 1024


def _cparams(n_axes):
    return pltpu.CompilerParams(dimension_semantics=("arbitrary",) * n_axes,
                                vmem_limit_bytes=VMEM_LIMIT_BYTES)


def _dot(a, b):
    return jnp.dot(a.astype(BF16), b.astype(BF16), preferred_element_type=F32)


def _dot_nt(a, b):
    return lax.dot_general(a.astype(BF16), b.astype(BF16), (((1,), (1,)), ((), ())),
                           preferred_element_type=F32)


def _dot_tn(a, b):
    return lax.dot_general(a.astype(BF16), b.astype(BF16), (((0,), (0,)), ((), ())),
                           preferred_element_type=F32)


def _split(x):
    hi = x.astype(BF16)
    lo = (x - hi.astype(F32)).astype(BF16)
    return hi, lo


def _dot_sel(x, sel):
    hi, lo = _split(x)
    return (jnp.dot(hi, sel, preferred_element_type=F32) + jnp.dot(lo, sel, preferred_element_type=F32))


def _sel_dot(sel, x):
    hi, lo = _split(x)
    return (jnp.dot(sel, hi, preferred_element_type=F32) + jnp.dot(sel, lo, preferred_element_type=F32))


def _dot_tn_sel(x, sel):
    hi, lo = _split(x)
    dn = (((0,), (0,)), ((), ()))
    return (lax.dot_general(hi, sel, dn, preferred_element_type=F32)
            + lax.dot_general(lo, sel, dn, preferred_element_type=F32))


def _iota(shape, dim):
    return lax.broadcasted_iota(jnp.int32, shape, dim)


def _shr(x, n):
    return lax.shift_right_logical(x, jnp.int32(n))


def _head_mask(rows, cols, row_shift, col_shift):
    return _shr(_iota((rows, cols), 0), row_shift) == _shr(_iota((rows, cols), 1), col_shift)


def _head_ones(rows, cols, row_shift, col_shift):
    return jnp.where(_head_mask(rows, cols, row_shift, col_shift), 1.0, 0.0).astype(BF16)


def _tril_ones(n):
    return jnp.where(_iota((n, n), 1) <= _iota((n, n), 0), 1.0, 0.0).astype(BF16)


def _bd(x, row_shift=6, col_shift=6):
    t = jnp.concatenate([x] * N_HEADS, axis=0)
    return jnp.where(_head_mask(t.shape[0], t.shape[1], row_shift, col_shift), t, 0.0)


def _lane_j(shape):
    return jnp.bitwise_and(_iota(shape, 1), CHUNK - 1)


def _sigmoid(x):
    return 1.0 / (1.0 + jnp.exp(-x))


def _softplus(x):
    return jnp.maximum(x, 0.0) + jnp.log(1.0 + jnp.exp(-jnp.abs(x)))


def _shift_rows(x, prev8, s, row):
    if s == 0:
        return x
    rp = pltpu.roll(prev8, s, 0)
    rp_t = jnp.concatenate([rp] * (x.shape[0] // 8), axis=0)
    return jnp.where(row < s, rp_t, pltpu.roll(x, s, 0))


def _proj_kernel(x_ref, *refs):
    n = len(refs) // 2
    xb = x_ref[...].astype(BF16)
    for w_ref, o_ref in zip(refs[:n], refs[n:]):
        o_ref[...] = jnp.dot(xb, w_ref[...], preferred_element_type=F32)


def _project(x2d, ws, tm):
    t_tok, d = x2d.shape
    return pl.pallas_call(
        _proj_kernel,
        grid=(t_tok // tm,),
        in_specs=[pl.BlockSpec((tm, d), lambda i: (i, 0))]
        + [pl.BlockSpec(w.shape, lambda i: (0, 0)) for w in ws],
        out_specs=[pl.BlockSpec((tm, w.shape[1]), lambda i: (i, 0)) for w in ws],
        out_shape=[jax.ShapeDtypeStruct((t_tok, w.shape[1]), F32) for w in ws],
        compiler_params=_cparams(1),
        name="proj",
    )(x2d, *ws)


def _rwkv_kernel(u_ref, mu_ref, w0_ref, wup_ref, a0_ref, aup_ref, gup_ref, kk_ref, ka_ref, rk_ref,
                 gng_ref, gnb_ref, o_ref,
                 st_ref, prev_ref, r_s, lw_s, k_s, v_s, a_s, b_s, g_s, y_s, rg_s, kt_s, p_s, arb_s, fvu_s):
    ct = u_ref.shape[1]
    gw = GROUP_WIDTH

    @pl.when(pl.program_id(1) == 0)
    def _():
        st_ref[...] = jnp.zeros_like(st_ref)
        prev_ref[...] = jnp.zeros_like(prev_ref)

    u = u_ref[0]
    row = _iota(u.shape, 0)
    prev = jnp.where(row == 0, prev_ref[0:1, :], pltpu.roll(u, 1, 0))
    prev_ref[0:1, :] = u_ref[0, ct - 1:ct, :]
    us = u + (prev - u) * mu_ref[...]
    r = us[:, 0:gw]
    k = us[:, gw:2 * gw]
    v = us[:, 2 * gw:3 * gw]
    misc = us[:, 3 * gw:]
    lnl = -_softplus(-(w0_ref[...] + _dot(jnp.tanh(misc), wup_ref[...]))) - 0.5
    a = _sigmoid(a0_ref[...] + _dot(misc, aup_ref[...]))
    ones_bd = _head_ones(gw, gw, 6, 6)
    kkp = k * kk_ref[...]
    kk = kkp / jnp.maximum(jnp.sqrt(_dot_sel(kkp * kkp, ones_bd)), 1e-12)
    k2 = k * (1.0 + (a - 1.0) * ka_ref[...])
    r_s[...] = r
    lw_s[...] = -jnp.exp(lnl)
    k_s[...] = k2
    v_s[...] = v
    a_s[...] = -kk
    b_s[...] = kk * a
    g_s[...] = _dot(_sigmoid(misc), gup_ref[...])

    def prep_body(ci, carry):
        sl = pl.ds(pl.multiple_of(ci * CHUNK, CHUNK), CHUNK)
        r_c, lw_c, k_c, v_c, a_c, b_c = r_s[sl, :], lw_s[sl, :], k_s[sl, :], v_s[sl, :], a_s[sl, :], b_s[sl, :]
        cs = _sel_dot(_tril_ones(CHUNK), lw_c)
        g_incl = jnp.exp(cs)
        g_prev = jnp.exp(cs - lw_c)
        g_inv = jnp.exp(-cs)
        lhs = jnp.concatenate([a_c * g_prev, r_c * g_incl], axis=0)
        sb = _dot_nt(lhs, _bd(b_c * g_inv))
        sk = _dot_nt(lhs, _bd(k_c * g_inv))
        shape = (CHUNK, gw)
        i_idx, j_idx = _iota(shape, 0), _lane_j(shape)
        strict, incl = j_idx < i_idx, j_idx <= i_idx
        a_ab = jnp.where(strict, sb[:CHUNK], 0.0)
        a_rb = jnp.where(incl, sb[CHUNK:], 0.0)
        a_k = jnp.concatenate([jnp.where(strict, sk[:CHUNK], 0.0), jnp.where(incl, sk[CHUNK:], 0.0)], axis=0)
        p = jnp.where(j_idx == i_idx, 1.0, 0.0) + a_ab
        ap = a_ab
        for _ in range(5):
            ap = _dot(ap, _bd(ap))
            p = p + _dot(p, _bd(ap))
        from_v = _dot(a_k, _bd(v_c))
        tail = jnp.exp(cs[CHUNK - 1:CHUNK, :] - cs)
        a_s[sl, :] = lhs[:CHUNK]
        rg_s[sl, :] = lhs[CHUNK:]
        b_s[sl, :] = b_c * tail
        kt_s[sl, :] = k_c * tail
        p_s[sl, :] = p
        arb_s[sl, :] = a_rb
        fvu_s[sl, :] = from_v[:CHUNK]
        y_s[sl, :] = from_v[CHUNK:]
        return carry

    lax.fori_loop(0, ct // CHUNK, prep_body, 0, unroll=2)

    def chunk_body(ci, carry):
        sl = pl.ds(pl.multiple_of(ci * CHUNK, CHUNK), CHUNK)
        st = st_ref[...]
        v_c = v_s[sl, :]
        from_state = _dot(jnp.concatenate([a_s[sl, :], rg_s[sl, :]], axis=0), st)
        uu = _dot(p_s[sl, :], _bd(from_state[:CHUNK] + fvu_s[sl, :]))
        y_s[sl, :] = y_s[sl, :] + from_state[CHUNK:] + _dot(arb_s[sl, :], _bd(uu))
        upd = _dot_tn(jnp.concatenate([b_s[sl, :], kt_s[sl, :]], axis=0), jnp.concatenate([uu, v_c], axis=0))
        g_col = jnp.exp(_dot_tn_sel(lw_s[sl, :], jnp.ones((CHUNK, gw), BF16)))
        st_ref[...] = g_col * st + jnp.where(_head_mask(gw, gw, 6, 6), upd, 0.0)
        return carry

    lax.fori_loop(0, ct // CHUNK, chunk_body, 0)

    y = y_s[...]
    mean = _dot_sel(y, ones_bd) * (1.0 / HEAD_DIM)
    yc = y - mean
    var = _dot_sel(yc * yc, ones_bd) * (1.0 / HEAD_DIM)
    yn = yc * lax.rsqrt(var + RWKV_GN_EPS) * gng_ref[...] + gnb_ref[...]
    bonus = _dot_sel(r_s[...] * k_s[...] * rk_ref[...], ones_bd) * v_s[...]
    o_ref[0] = ((yn + bonus) * g_s[...]).astype(o_ref.dtype)


def _rwkv(u_a, prm, ct):
    bsz, s, _ = u_a.shape
    gw = GROUP_WIDTH
    vec = lambda n: pl.BlockSpec((1, n), lambda b, c: (0, 0))
    mat = lambda m, n: pl.BlockSpec((m, n), lambda b, c: (0, 0))
    tile = lambda n: pltpu.VMEM((ct, n), F32)
    return pl.pallas_call(
        _rwkv_kernel,
        grid=(bsz, s // ct),
        in_specs=[pl.BlockSpec((1, ct, D_IN_A), lambda b, c: (b, c, 0)), vec(D_IN_A), vec(gw), mat(MISC, gw),
                  vec(gw), mat(MISC, gw), mat(MISC, gw), vec(gw), vec(gw), vec(gw), vec(gw), vec(gw)],
        out_specs=pl.BlockSpec((1, ct, gw), lambda b, c: (b, c, 0)),
        out_shape=jax.ShapeDtypeStruct((bsz, s, gw), BF16),
        scratch_shapes=[pltpu.VMEM((gw, gw), F32), pltpu.VMEM((8, D_IN_A), F32)] + [tile(gw)] * 13,
        compiler_params=_cparams(2),
        name="rwkv7",
    )(u_a, *prm)


def _gla_kernel(u_ref, m_ref, aup_ref, ab_ref, ng_ref, o_ref,
                st_ref, q_s, k_s, b_s, v_s, oi_s, ox_s):
    ct = u_ref.shape[1]
    qw, gw = GLA_QK_WIDTH, GROUP_WIDTH

    @pl.when(pl.program_id(1) == 0)
    def _():
        st_ref[...] = jnp.zeros_like(st_ref)

    u = u_ref[0]
    q_s[...] = u[:, 0:qw] * (GLA_KEY_DIM ** -0.5)
    k_s[...] = u[:, qw:2 * qw]
    v_s[...] = u[:, 2 * qw:2 * qw + gw]
    gate = u[:, 2 * qw + gw:]
    log_a = -_softplus(-(_dot(m_ref[0], aup_ref[...]) + ab_ref[...])) * (1.0 / GLA_TAU)
    head_mask = _head_mask(qw, gw, 5, 6)

    b_s[...] = log_a

    def chunk_body(ci, carry):
        sl = pl.ds(pl.multiple_of(ci * CHUNK, CHUNK), CHUNK)
        la_c = b_s[sl, :]
        q_c, k_c, v_c = q_s[sl, :], k_s[sl, :], v_s[sl, :]
        st = st_ref[...]
        b = _sel_dot(_tril_ones(CHUNK), la_c)
        b_s[sl, :] = b
        v_bd = _bd(v_c)
        j_idx = _lane_j((GLA_SUB, gw))
        parts = [jnp.zeros((GLA_SUB, gw), F32)]
        for blk in range(1, CHUNK // GLA_SUB):
            lo = blk * GLA_SUB
            beta = b[lo - 1:lo, :]
            q_blk = q_c[lo:lo + GLA_SUB, :] * jnp.exp(b[lo:lo + GLA_SUB, :] - beta)
            k_sc = k_c * jnp.exp(jnp.minimum(beta - b, 0.0))
            sc = jnp.where(j_idx < lo, _dot_nt(q_blk, _bd(k_sc, 6, 5)), 0.0)
            parts.append(_dot(sc, v_bd))
        ox_s[sl, :] = _dot(q_c * jnp.exp(b), st) + jnp.concatenate(parts, axis=0)
        tail = jnp.exp(b[CHUNK - 1:CHUNK, :] - b)
        g_col = jnp.exp(_dot_tn_sel(la_c, jnp.ones((CHUNK, gw), BF16)))
        st_ref[...] = g_col * st + jnp.where(head_mask, _dot_tn(k_c * tail, v_c), 0.0)
        return carry

    lax.fori_loop(0, ct // CHUNK, chunk_body, 0)

    seg_bcast = jnp.where(head_mask, 1.0, 0.0).astype(BF16)

    def sub_body(sb, carry):
        c0 = pl.multiple_of(sb * GLA_SUB, GLA_SUB)
        sl = pl.ds(c0, GLA_SUB)
        q_t = jnp.concatenate([q_s[sl, :]] * GLA_SUB, axis=0)
        b_t = jnp.concatenate([b_s[sl, :]] * GLA_SUB, axis=0)
        rep = lambda ref, w: jnp.concatenate(
            [jnp.broadcast_to(ref[pl.ds(c0 + j, 1), :], (GLA_SUB, w)) for j in range(GLA_SUB)], axis=0)
        k_r, b_r = rep(k_s, qw), rep(b_s, qw)
        rows = _iota((GLA_SUB * GLA_SUB, qw), 0)
        valid = jnp.bitwise_and(rows, GLA_SUB - 1) >= _shr(rows, 4)
        p = jnp.where(valid, q_t * k_r * jnp.exp(jnp.minimum(b_t - b_r, 0.0)), 0.0)
        sx = jnp.dot(p.astype(BF16), seg_bcast, preferred_element_type=F32)
        o = jnp.zeros((GLA_SUB, gw), F32)
        for j in range(GLA_SUB):
            o = o + sx[j * GLA_SUB:(j + 1) * GLA_SUB, :] * v_s[pl.ds(c0 + j, 1), :]
        oi_s[sl, :] = o
        return carry

    lax.fori_loop(0, ct // GLA_SUB, sub_body, 0, unroll=2)

    o = oi_s[...] + ox_s[...]
    ms = _dot_sel(o * o, _head_ones(gw, gw, 6, 6)) * (1.0 / HEAD_DIM)
    on = o * lax.rsqrt(ms + NORM_EPS) * ng_ref[...]
    o_ref[0] = (on * (gate * _sigmoid(gate))).astype(o_ref.dtype)


def _gla(u_b, misc, prm, ct):
    bsz, s, wb = u_b.shape
    qw, gw = GLA_QK_WIDTH, GROUP_WIDTH
    return pl.pallas_call(
        _gla_kernel,
        grid=(bsz, s // ct),
        in_specs=[pl.BlockSpec((1, ct, wb), lambda b, c: (b, c, 0)),
                  pl.BlockSpec((1, ct, MISC), lambda b, c: (b, c, 0)),
                  pl.BlockSpec((MISC, qw), lambda b, c: (0, 0)),
                  pl.BlockSpec((1, qw), lambda b, c: (0, 0)),
                  pl.BlockSpec((1, gw), lambda b, c: (0, 0))],
        out_specs=pl.BlockSpec((1, ct, gw), lambda b, c: (b, c, 0)),
        out_shape=jax.ShapeDtypeStruct((bsz, s, gw), BF16),
        scratch_shapes=[pltpu.VMEM((qw, gw), F32), pltpu.VMEM((ct, qw), F32), pltpu.VMEM((ct, qw), F32),
                        pltpu.VMEM((ct, qw), F32), pltpu.VMEM((ct, gw), F32), pltpu.VMEM((ct, gw), F32),
                        pltpu.VMEM((ct, gw), F32)],
        compiler_params=_cparams(2),
        name="gla",
    )(u_b, misc, *prm)


def _mlstm_kernel(u_ref, m_ref, cw_ref, cb_ref, eig_ref, efg_ref, ib_ref, fb_ref, ng_ref, o_ref,
                  mem_ref, nb_ref, mst_ref, prev_ref, q_s, k_s, v_s, li_s, lf_s, h_s):
    ct = u_ref.shape[1]
    gw = GROUP_WIDTH

    @pl.when(pl.program_id(1) == 0)
    def _():
        mem_ref[...] = jnp.zeros_like(mem_ref)
        nb_ref[...] = jnp.zeros_like(nb_ref)
        mst_ref[...] = jnp.zeros_like(mst_ref)
        prev_ref[...] = jnp.zeros_like(prev_ref)

    u = u_ref[0]
    qk = u[:, 0:2 * gw]
    row = _iota(qk.shape, 0)
    prev8 = prev_ref[...]
    conv = cb_ref[...] + jnp.zeros_like(qk)
    for j in range(MLSTM_CONV):
        conv = conv + cw_ref[j:j + 1, :] * _shift_rows(qk, prev8, MLSTM_CONV - 1 - j, row)
    prev_ref[...] = u_ref[0, ct - 8:ct, 0:2 * gw]
    qk_act = conv * _sigmoid(conv)
    q_s[...] = qk_act[:, 0:gw]
    k_s[...] = qk_act[:, gw:] * (HEAD_DIM ** -0.5)
    v_s[...] = u[:, 2 * gw:3 * gw]
    o_gate = u[:, 3 * gw:]
    misc = m_ref[0]
    li_s[...] = _dot_sel(misc, eig_ref[...]) + ib_ref[...]
    lf_s[...] = -_softplus(-(_dot_sel(misc, efg_ref[...]) + fb_ref[...]))
    neg_inf = -jnp.inf

    def chunk_body(ci, carry):
        sl = pl.ds(pl.multiple_of(ci * CHUNK, CHUNK), CHUNK)
        q_c, k_c, v_c, li_c, lf_c = q_s[sl, :], k_s[sl, :], v_s[sl, :], li_s[sl, :], lf_s[sl, :]
        mem, nb, m_e = mem_ref[...], nb_ref[...], mst_ref[0:1, :]
        shape = (CHUNK, gw)
        i_idx, j_idx = _iota(shape, 0), _lane_j(shape)
        incl = j_idx <= i_idx
        head_mask = _head_mask(gw, gw, 6, 6)
        fc = _sel_dot(_tril_ones(CHUNK), lf_c)
        g_row = li_c - fc
        ones_avg = jnp.full(shape, 1.0 / HEAD_DIM, BF16)
        g_hi, g_lo = _split(_bd(g_row))
        dn = (((1,), (1,)), ((), ()))
        g_t = (lax.dot_general(ones_avg, g_hi, dn, preferred_element_type=F32)
               + lax.dot_general(ones_avg, g_lo, dn, preferred_element_type=F32))
        log_w = jnp.where(incl, fc + g_t, neg_inf)
        cm = g_row
        for sh in (1, 2, 4, 8, 16, 32):
            cm = jnp.maximum(cm, jnp.where(i_idx < sh, neg_inf, pltpu.roll(cm, sh, 0)))
        log_carry = fc + m_e
        m_row = jnp.maximum(fc + cm, log_carry)
        sc = _dot_nt(q_c, _bd(k_c)) * jnp.exp(log_w - m_row)
        w_c = jnp.exp(log_carry - m_row)
        scb = sc.astype(BF16)
        num = jnp.dot(scb, _bd(v_c).astype(BF16), preferred_element_type=F32) + w_c * _dot(q_c, mem)
        den = (jnp.dot(scb, _head_ones(gw, gw, 6, 6), preferred_element_type=F32) + w_c * _dot(q_c, nb))
        h_s[sl, :] = num / jnp.maximum(jnp.abs(den), jnp.exp(-m_row))
        f_end = fc[CHUNK - 1:CHUNK, :]
        log_kv = f_end - fc + li_c
        m_new = jnp.maximum(f_end + m_e, jnp.max(log_kv, axis=0, keepdims=True))
        kwk = jnp.exp(log_kv - m_new) * k_c
        cd = jnp.exp(f_end + m_e - m_new)
        cd_rows = jnp.where(_iota((16, gw), 0) == 0, cd, 0.0)
        cd_col = _dot_tn_sel(cd_rows, jnp.ones((16, gw), BF16))
        upd = _dot_tn(kwk, jnp.concatenate([v_c, jnp.ones(shape, F32)], axis=1))
        mem_ref[...] = cd_col * mem + jnp.where(head_mask, upd[:, :gw], 0.0)
        nb_ref[...] = cd_col * nb + jnp.where(head_mask, upd[:, gw:], 0.0)
        mst_ref[0:1, :] = m_new
        return carry

    lax.fori_loop(0, ct // CHUNK, chunk_body, 0)

    h = h_s[...]
    ones_bd = _head_ones(gw, gw, 6, 6)
    mean = _dot_sel(h, ones_bd) * (1.0 / HEAD_DIM)
    hc = h - mean
    var = _dot_sel(hc * hc, ones_bd) * (1.0 / HEAD_DIM)
    hn = hc * lax.rsqrt(var + LN_EPS) * ng_ref[...]
    o_ref[0] = (hn * _sigmoid(o_gate)).astype(o_ref.dtype)


def _mlstm(u_c, misc, prm, ct):
    bsz, s, wc = u_c.shape
    gw = GROUP_WIDTH
    vec = lambda n: pl.BlockSpec((1, n), lambda b, c: (0, 0))
    mat = lambda m, n: pl.BlockSpec((m, n), lambda b, c: (0, 0))
    tile = pltpu.VMEM((ct, gw), F32)
    return pl.pallas_call(
        _mlstm_kernel,
        grid=(bsz, s // ct),
        in_specs=[pl.BlockSpec((1, ct, wc), lambda b, c: (b, c, 0)),
                  pl.BlockSpec((1, ct, MISC), lambda b, c: (b, c, 0)),
                  mat(MLSTM_CONV, 2 * gw), vec(2 * gw), mat(MISC, gw), mat(MISC, gw), vec(gw), vec(gw), vec(gw)],
        out_specs=pl.BlockSpec((1, ct, gw), lambda b, c: (b, c, 0)),
        out_shape=jax.ShapeDtypeStruct((bsz, s, gw), BF16),
        scratch_shapes=[pltpu.VMEM((gw, gw), F32), pltpu.VMEM((gw, gw), F32), pltpu.VMEM((8, gw), F32),
                        pltpu.VMEM((8, 2 * gw), F32)] + [tile] * 6,
        compiler_params=_cparams(2),
        name="mlstm",
    )(u_c, misc, *prm)


def _mla_prep_kernel(u_ref, m_ref, cq_ref, sq_ref, csk_ref, qg_ref, kvg_ref, wq_ref, wqr_ref, wk_ref, wv_ref,
                     selk_ref, q_ref, k_ref, v_ref):
    u = u_ref[0]
    cq = u[:, 0:MLA_Q_RANK]
    ckv = u[:, MLA_Q_RANK:]
    cqn = cq * lax.rsqrt(jnp.mean(cq * cq, axis=-1, keepdims=True) + NORM_EPS) * qg_ref[...]
    ckvn = ckv * lax.rsqrt(jnp.mean(ckv * ckv, axis=-1, keepdims=True) + NORM_EPS) * kvg_ref[...]
    cqb, ckvb = cqn.astype(BF16), ckvn.astype(BF16)
    k_rope = _dot(m_ref[0] * csk_ref[0], selk_ref[...])
    cos_q, sin_q = cq_ref[0], sq_ref[0]
    scale = MLA_QK ** -0.5
    for h in range(N_HEADS):
        qh = (jnp.dot(cqb, wq_ref[h], preferred_element_type=F32) * cos_q
              + jnp.dot(cqb, wqr_ref[h], preferred_element_type=F32) * sin_q) * scale
        q_ref[0, h] = qh.astype(BF16)
        k_ref[0, h] = (jnp.dot(ckvb, wk_ref[h], preferred_element_type=F32) + k_rope).astype(BF16)
        v_ref[0, h] = jnp.dot(ckvb, wv_ref[h], preferred_element_type=F32).astype(BF16)


def _mla_prep(u_d, misc, tabs, prm, tm):
    bsz, s, wd = u_d.shape
    full = lambda a: pl.BlockSpec(a.shape, lambda b, c: (0,) * a.ndim)
    tok = lambda n: pl.BlockSpec((1, tm, n), lambda b, c: (b, c, 0))
    head = lambda n: pl.BlockSpec((1, N_HEADS, tm, n), lambda b, c: (b, 0, c, 0))
    return pl.pallas_call(
        _mla_prep_kernel,
        grid=(bsz, s // tm),
        in_specs=[tok(wd), tok(MISC), tok(MLA_QK), tok(MLA_QK), tok(MISC)] + [full(a) for a in prm],
        out_specs=[head(MLA_QK), head(MLA_QK), head(HEAD_DIM)],
        out_shape=[jax.ShapeDtypeStruct((bsz, N_HEADS, s, MLA_QK), BF16),
                   jax.ShapeDtypeStruct((bsz, N_HEADS, s, MLA_QK), BF16),
                   jax.ShapeDtypeStruct((bsz, N_HEADS, s, HEAD_DIM), BF16)],
        compiler_params=_cparams(2),
        name="mla_prep",
    )(u_d, misc, *tabs, *prm)


def _attn_kernel(q_ref, k_ref, v_ref, o_ref):
    tq = q_ref.shape[2]
    qi = pl.program_id(2)
    q = q_ref[0, 0]

    def step(j, carry, diagonal):
        m, l, acc = carry
        sl = pl.ds(pl.multiple_of(j * tq, tq), tq)
        kb, vb = k_ref[0, 0, sl, :], v_ref[0, 0, sl, :]
        s = lax.dot_general(q, kb, (((1,), (1,)), ((), ())), preferred_element_type=F32)
        if diagonal:
            s = jnp.where(_iota((tq, tq), 1) <= _iota((tq, tq), 0), s, -jnp.inf)
        m_new = jnp.maximum(m, jnp.max(s, axis=-1, keepdims=True))
        p = jnp.exp(s - m_new)
        alpha = jnp.exp(m - m_new)
        l = alpha * l + jnp.sum(p, axis=-1, keepdims=True)
        acc = alpha * acc + jnp.dot(p.astype(BF16), vb, preferred_element_type=F32)
        return m_new, l, acc

    init = (jnp.full((tq, 1), -jnp.inf, F32), jnp.zeros((tq, 1), F32), jnp.zeros((tq, HEAD_DIM), F32))
    carry = lax.fori_loop(0, qi, lambda j, c: step(j, c, False), init)
    _, l, acc = step(qi, carry, True)
    o_ref[0, 0] = (acc / l).astype(o_ref.dtype)


def _attention(q, k, v, tq):
    bsz, nh, s, _ = q.shape
    return pl.pallas_call(
        _attn_kernel,
        grid=(bsz, nh, s // tq),
        in_specs=[pl.BlockSpec((1, 1, tq, MLA_QK), lambda b, h, i: (b, h, i, 0)),
                  pl.BlockSpec((1, 1, s, MLA_QK), lambda b, h, i: (b, h, 0, 0)),
                  pl.BlockSpec((1, 1, s, HEAD_DIM), lambda b, h, i: (b, h, 0, 0))],
        out_specs=pl.BlockSpec((1, 1, tq, HEAD_DIM), lambda b, h, i: (b, h, i, 0)),
        out_shape=jax.ShapeDtypeStruct((bsz, nh, s, HEAD_DIM), BF16),
        compiler_params=_cparams(3),
        name="mla_attn",
    )(q, k, v)


def _layer_norm(z, g, b):
    zc = z - jnp.mean(z, axis=-1, keepdims=True)
    return zc * lax.rsqrt(jnp.mean(zc * zc, axis=-1, keepdims=True) + LN_EPS) * g + b


def _outproj_kernel(x_ref, ya_ref, yb_ref, yc_ref, yd_ref, wo_ref, wod_ref, g_ref, b_ref, o_ref):
    gw = GROUP_WIDTH
    mix = jnp.dot(ya_ref[0], wo_ref[0:gw, :], preferred_element_type=F32)
    mix = mix + jnp.dot(yb_ref[0], wo_ref[gw:2 * gw, :], preferred_element_type=F32)
    mix = mix + jnp.dot(yc_ref[0], wo_ref[2 * gw:3 * gw, :], preferred_element_type=F32)
    for h in range(N_HEADS):
        mix = mix + jnp.dot(yd_ref[0, h], wod_ref[h], preferred_element_type=F32)
    o_ref[0] = _layer_norm(DN_ALPHA * x_ref[0] + mix, g_ref[...], b_ref[...])


def _outproj(x, ya, yb, yc, yd, wo, wod, g, b, tm):
    bsz, s, d = x.shape
    gw = GROUP_WIDTH
    tok = lambda n: pl.BlockSpec((1, tm, n), lambda bb, c: (bb, c, 0))
    full = lambda a: pl.BlockSpec(a.shape, lambda bb, c: (0,) * a.ndim)
    return pl.pallas_call(
        _outproj_kernel,
        grid=(bsz, s // tm),
        in_specs=[tok(d), tok(gw), tok(gw), tok(gw),
                  pl.BlockSpec((1, N_HEADS, tm, HEAD_DIM), lambda bb, c: (bb, 0, c, 0)),
                  full(wo), full(wod), full(g), full(b)],
        out_specs=tok(d),
        out_shape=jax.ShapeDtypeStruct((bsz, s, d), F32),
        compiler_params=_cparams(2),
        name="outproj_ln1",
    )(x, ya, yb, yc, yd, wo, wod, g, b)


def _router_kernel(x_ref, w_ref, b_ref, idx_ref, gate_ref):
    x = x_ref[...]
    xh, xl = _split(x)
    wh, wl = _split(w_ref[...])
    dn = (((1,), (1,)), ((), ()))
    logits = (lax.dot_general(wh, xh, dn, preferred_element_type=F32)
              + lax.dot_general(wh, xl, dn, preferred_element_type=F32)
              + lax.dot_general(wl, xh, dn, preferred_element_type=F32)) + b_ref[...]
    tm = x.shape[0]
    ng, ne = N_EXPERT_GROUPS, EXPERTS_PER_GROUP
    gl = [logits[i:i + 1, :] for i in range(ng)]
    gmax = functools.reduce(jnp.maximum, gl)
    gsum = functools.reduce(jnp.add, [jnp.exp(g - gmax) for g in gl])
    group_p = 1.0 / gsum
    gidx = jnp.full((1, tm), ng - 1, jnp.int32)
    for i in range(ng - 2, -1, -1):
        gidx = jnp.where(gl[i] == gmax, i, gidx)
    el = logits[8 + (ng - 1) * ne:8 + ng * ne, :]
    for i in range(ng - 2, -1, -1):
        el = jnp.where(gidx == i, logits[8 + i * ne:8 + (i + 1) * ne, :], el)
    rows = _iota((ne, tm), 0)
    m1 = jnp.max(el, axis=0, keepdims=True)
    i1 = jnp.min(jnp.where(el == m1, rows, ne), axis=0, keepdims=True)
    el2 = jnp.where(rows == i1, -jnp.inf, el)
    m2 = jnp.max(el2, axis=0, keepdims=True)
    i2 = jnp.min(jnp.where(el2 == m2, rows, ne), axis=0, keepdims=True)
    z = jnp.sum(jnp.exp(el - m1), axis=0, keepdims=True)
    p1 = 1.0 / z
    p2 = jnp.exp(m2 - m1) / z
    base = gidx * ne
    idx_ref[...] = jnp.concatenate([base + i1, base + i2], axis=0)
    gate_ref[...] = jnp.concatenate([group_p * p1 / (p1 + p2), group_p * p2 / (p1 + p2)], axis=0)


def _router(x2d, w_t, b_col, tm):
    t_tok, d = x2d.shape
    return pl.pallas_call(
        _router_kernel,
        grid=(t_tok // tm,),
        in_specs=[pl.BlockSpec((tm, d), lambda i: (i, 0)), pl.BlockSpec(w_t.shape, lambda i: (0, 0)),
                  pl.BlockSpec(b_col.shape, lambda i: (0, 0))],
        out_specs=[pl.BlockSpec((TOP_K, tm), lambda i: (0, i)), pl.BlockSpec((TOP_K, tm), lambda i: (0, i))],
        out_shape=[jax.ShapeDtypeStruct((TOP_K, t_tok), jnp.int32), jax.ShapeDtypeStruct((TOP_K, t_tok), F32)],
        compiler_params=_cparams(1),
        name="router",
    )(x2d, w_t, b_col)


def _moe_kernel(be_ref, x_ref, gate_ref, wg_ref, wu_ref, wd_ref, o_ref, wg_s, wu_s, wd_s):
    i = pl.program_id(0)

    @pl.when(jnp.logical_or(i == 0, be_ref[i] != be_ref[jnp.maximum(i - 1, 0)]))
    def _():
        wg_s[...] = wg_ref[0].astype(BF16)
        wu_s[...] = wu_ref[0].astype(BF16)
        wd_s[...] = wd_ref[0].astype(BF16)

    xb = x_ref[...]
    hg = jnp.dot(xb, wg_s[...], preferred_element_type=F32)
    hu = jnp.dot(xb, wu_s[...], preferred_element_type=F32)
    hid = (hg * _sigmoid(hg)) * hu
    y = jnp.dot(hid.astype(BF16), wd_s[...], preferred_element_type=F32) * gate_ref[...]
    o_ref[...] = y.astype(o_ref.dtype)


def _moe(x_rows, block_expert, row_gate, wg, wu, wd, layer):
    n_rows, d = x_rows.shape
    n_blocks = n_rows // MOE_BLOCK
    grid_spec = pltpu.PrefetchScalarGridSpec(
        num_scalar_prefetch=1,
        grid=(n_blocks,),
        in_specs=[pl.BlockSpec((MOE_BLOCK, d), lambda i, be: (i, 0)),
                  pl.BlockSpec((MOE_BLOCK, 1), lambda i, be: (i, 0)),
                  pl.BlockSpec((None, 1, d, D_EXPERT), lambda i, be: (layer, be[i], 0, 0)),
                  pl.BlockSpec((None, 1, d, D_EXPERT), lambda i, be: (layer, be[i], 0, 0)),
                  pl.BlockSpec((None, 1, D_EXPERT, d), lambda i, be: (layer, be[i], 0, 0))],
        out_specs=pl.BlockSpec((MOE_BLOCK, d), lambda i, be: (i, 0)),
        scratch_shapes=[pltpu.VMEM((d, D_EXPERT), BF16), pltpu.VMEM((d, D_EXPERT), BF16),
                        pltpu.VMEM((D_EXPERT, d), BF16)],
    )
    return pl.pallas_call(
        _moe_kernel,
        grid_spec=grid_spec,
        out_shape=jax.ShapeDtypeStruct((n_rows, d), BF16),
        compiler_params=_cparams(1),
        name="moe_experts",
    )(block_expert, x_rows, row_gate.reshape(n_rows, 1), wg, wu, wd)


def _dispatch(idx, gate, t_tok):
    n_asg = t_tok * TOP_K
    flat_e = idx.reshape(n_asg)
    flat_g = gate.reshape(n_asg)
    order = jnp.argsort(flat_e)
    inverse = jnp.argsort(order).astype(jnp.int32)
    starts = jnp.searchsorted(flat_e[order], jnp.arange(N_EXPERTS + 1, dtype=jnp.int32), side='left')
    starts = starts.astype(jnp.int32)
    counts = starts[1:] - starts[:-1]
    starts = starts[:-1]
    padded = (counts + MOE_BLOCK - 1) // MOE_BLOCK * MOE_BLOCK
    pad_ends = jnp.cumsum(padded)
    pad_starts = pad_ends - padded
    n_rows = -(-(n_asg + N_EXPERTS * (MOE_BLOCK - 1)) // MOE_BLOCK) * MOE_BLOCK
    n_blocks = n_rows // MOE_BLOCK
    block_expert = jnp.minimum(jnp.searchsorted(pad_ends, jnp.arange(n_blocks) * MOE_BLOCK, side='right'),
                               N_EXPERTS - 1).astype(jnp.int32)
    pos = pad_starts[flat_e] + inverse - starts[flat_e]
    row_e = jnp.repeat(block_expert, MOE_BLOCK)
    row_rank = jnp.arange(n_rows, dtype=jnp.int32) - pad_starts[row_e]
    valid = row_rank < counts[row_e]
    asg = order[jnp.where(valid, starts[row_e] + row_rank, 0)]
    row_tok = jnp.where(valid, asg % t_tok, 0).astype(jnp.int32)
    row_gate = jnp.where(valid, flat_g[asg], 0.0)
    return block_expert, row_tok, pos.reshape(TOP_K, t_tok), row_gate


def _final_kernel(x_ref, p_ref, f0_ref, f1_ref, wg_ref, bg_ref, wp_ref, g_ref, b_ref, o_ref):
    x = x_ref[...]
    gate = _sigmoid(_dot(x, wg_ref[...]) + bg_ref[...])
    ple = gate * _dot(p_ref[...], wp_ref[...])
    ffn = f0_ref[...].astype(F32) + f1_ref[...].astype(F32)
    o_ref[...] = _layer_norm(DN_ALPHA * x + ffn + ple, g_ref[...], b_ref[...])


def _final(x2d, p2d, ffn0, ffn1, wg, bg, wp, g, b, tm):
    t_tok, d = x2d.shape
    full = lambda a: pl.BlockSpec(a.shape, lambda i: (0,) * a.ndim)
    return pl.pallas_call(
        _final_kernel,
        grid=(t_tok // tm,),
        in_specs=[pl.BlockSpec((tm, d), lambda i: (i, 0)), pl.BlockSpec((tm, p2d.shape[1]), lambda i: (i, 0)),
                  pl.BlockSpec((tm, d), lambda i: (i, 0)), pl.BlockSpec((tm, d), lambda i: (i, 0)),
                  full(wg), full(bg), full(wp), full(g), full(b)],
        out_specs=pl.BlockSpec((tm, d), lambda i: (i, 0)),
        out_shape=jax.ShapeDtypeStruct((t_tok, d), F32),
        compiler_params=_cparams(1),
        name="ple_ln2",
    )(x2d, p2d, ffn0, ffn1, wg, bg, wp, g, b)


def _pad_rows(w, offset, total=MISC):
    return jnp.zeros((total, w.shape[1]), w.dtype).at[offset:offset + w.shape[0]].set(w)


def _rot_cols(w):
    half = w.shape[1] // 2
    return jnp.concatenate([-w[:, half:], w[:, :half]], axis=1)


def _row(v):
    return v.reshape(1, -1)


def _expand_heads(v):
    return jnp.repeat(v, HEAD_DIM).reshape(1, GROUP_WIDTH)


def _head_select(offset):
    sel = np.zeros((MISC, GROUP_WIDTH), np.float32)
    for h in range(N_HEADS):
        sel[offset + h, h * HEAD_DIM:(h + 1) * HEAD_DIM] = 1.0
    return jnp.asarray(sel, BF16)


def _rope_tables(positions):
    inv_freq = ROPE_THETA ** (-jnp.arange(0, MLA_ROPE, 2, dtype=F32) / MLA_ROPE)
    ang = positions.astype(F32)[..., None] * inv_freq
    cos, sin = jnp.cos(ang), jnp.sin(ang)
    cos2 = jnp.concatenate([cos, cos], axis=-1)
    sin2 = jnp.concatenate([sin, sin], axis=-1)
    lead = cos.shape[:-1]
    cos_q = jnp.concatenate([jnp.ones(lead + (MLA_NOPE,), F32), cos2], axis=-1)
    sin_q = jnp.concatenate([jnp.zeros(lead + (MLA_NOPE,), F32), sin2], axis=-1)
    cs_k = jnp.concatenate([jnp.zeros(lead + (MISC_KR,), F32), cos2, sin2,
                            jnp.zeros(lead + (MISC - MISC_KR_ROT - MLA_ROPE,), F32)], axis=-1)
    return cos_q, sin_q, cs_k


def _rope_key_select():
    sel = np.zeros((MISC, MLA_QK), np.float32)
    for c in range(MLA_ROPE):
        sel[MISC_KR + c, MLA_NOPE + c] = 1.0
        sel[MISC_KR_ROT + c, MLA_NOPE + c] = 1.0
    return jnp.asarray(sel, BF16)


def _split_w_in(w):
    a0, b0 = 0, D_IN_A
    c0, d0 = b0 + D_IN_B, b0 + D_IN_B + D_IN_C
    qw, gw = GLA_QK_WIDTH, GROUP_WIDTH
    w_a = w[:, a0:b0]
    gla_ad = w[:, b0 + 2 * qw + gw:b0 + 2 * qw + gw + GLA_GATE_RANK]
    w_b = jnp.concatenate([w[:, b0:b0 + 2 * qw + gw], w[:, b0 + 2 * qw + gw + GLA_GATE_RANK:c0]], axis=1)
    w_c = w[:, c0:c0 + 4 * gw]
    gates = w[:, c0 + 4 * gw:d0]
    w_d = w[:, d0:d0 + MLA_Q_RANK + MLA_KV_RANK]
    kr = w[:, d0 + MLA_Q_RANK + MLA_KV_RANK:]
    z = lambda n: jnp.zeros((w.shape[0], n), w.dtype)
    w_m = jnp.concatenate([gla_ad, gates, z(MISC_KR - MISC_FG - N_HEADS), kr, _rot_cols(kr),
                           z(MISC - MISC_KR_ROT - MLA_ROPE)], axis=1)
    return [t.astype(BF16) for t in (w_a, w_b, w_c, w_d, w_m)]


def _tiles(s):
    return min(512, s), min(512, s)


def kernel(x, p, positions, w_in, rwkv_mu, rwkv_w0, rwkv_w_up, rwkv_a0, rwkv_a_up, rwkv_g_up, rwkv_k_k, rwkv_k_a, rwkv_r_k, rwkv_gn_g, rwkv_gn_b, gla_alpha_up, gla_alpha_b, gla_norm_g, mlstm_conv_w, mlstm_conv_b, mlstm_i_b, mlstm_f_b, mlstm_norm_g, mla_q_norm_g, mla_w_uq, mla_kv_norm_g, mla_w_ukv, w_out, ln1_g, ln1_b, moe_w_rg, moe_b_rg, moe_w_re, moe_b_re, moe_w_gate, moe_w_up, moe_w_down, ple_w_gate, ple_b_gate, ple_w, ln2_g, ln2_b):
    bsz, s, d = x.shape
    t_tok = bsz * s
    depth = w_in.shape[0]
    ct, tq = _tiles(s)
    tm = min(512, t_tok)
    rope_tabs = _rope_tables(positions)
    sel_ig, sel_fg, sel_k = _head_select(MISC_IG), _head_select(MISC_FG), _rope_key_select()
    for i in range(depth):
        u_a, u_b, u_c, u_d, u_m = _project(x.reshape(t_tok, d), _split_w_in(w_in[i]), tm)
        u_a, u_b, u_c, u_d, u_m = (t.reshape(bsz, s, -1) for t in (u_a, u_b, u_c, u_d, u_m))

        rk = RWKV_DECAY_RANK
        ya = _rwkv(u_a, (_row(rwkv_mu[i]), _row(rwkv_w0[i]), _pad_rows(rwkv_w_up[i], 0).astype(BF16),
                         _row(rwkv_a0[i]), _pad_rows(rwkv_a_up[i], rk).astype(BF16),
                         _pad_rows(rwkv_g_up[i], rk + RWKV_ICL_RANK).astype(BF16),
                         _row(rwkv_k_k[i]), _row(rwkv_k_a[i]), _row(rwkv_r_k[i]),
                         _row(rwkv_gn_g[i]), _row(rwkv_gn_b[i])), ct)
        yb = _gla(u_b, u_m, (_pad_rows(gla_alpha_up[i], MISC_GLA_AD).astype(BF16), _row(gla_alpha_b[i]),
                             _row(gla_norm_g[i])), ct)
        yc = _mlstm(u_c, u_m, (mlstm_conv_w[i], _row(mlstm_conv_b[i]), sel_ig, sel_fg,
                               _expand_heads(mlstm_i_b[i]), _expand_heads(mlstm_f_b[i]),
                               _row(mlstm_norm_g[i])), ct)

        wq = mla_w_uq[i].reshape(MLA_Q_RANK, N_HEADS, MLA_QK).transpose(1, 0, 2)
        wq_rot = jnp.concatenate([jnp.zeros((N_HEADS, MLA_Q_RANK, MLA_NOPE), F32),
                                  jax.vmap(_rot_cols)(wq[:, :, MLA_NOPE:])], axis=2)
        wkv = mla_w_ukv[i].reshape(MLA_KV_RANK, N_HEADS, MLA_NOPE + HEAD_DIM).transpose(1, 0, 2)
        wk = jnp.concatenate([wkv[:, :, :MLA_NOPE], jnp.zeros((N_HEADS, MLA_KV_RANK, MLA_ROPE), F32)], axis=2)
        q, k, v = _mla_prep(u_d, u_m, rope_tabs,
                            (_row(mla_q_norm_g[i]), _row(mla_kv_norm_g[i]), wq.astype(BF16), wq_rot.astype(BF16),
                             wk.astype(BF16), wkv[:, :, MLA_NOPE:].astype(BF16), sel_k), ct)
        yd = _attention(q, k, v, tq)

        wo = w_out[i].astype(BF16)
        x = _outproj(x, ya, yb, yc, yd, wo[:3 * GROUP_WIDTH], wo[3 * GROUP_WIDTH:].reshape(N_HEADS, HEAD_DIM, d),
                     _row(ln1_g[i]), _row(ln1_b[i]), ct)

        x2d = x.reshape(t_tok, d)
        w_route = jnp.concatenate([moe_w_rg[i].T, jnp.zeros((8 - N_EXPERT_GROUPS, d), F32), moe_w_re[i].T], axis=0)
        b_route = jnp.concatenate([moe_b_rg[i], jnp.zeros((8 - N_EXPERT_GROUPS,), F32), moe_b_re[i]]).reshape(-1, 1)
        idx, gate = _router(x2d, w_route, b_route, tm)
        block_expert, row_tok, pos, row_gate = _dispatch(idx, gate, t_tok)
        y_rows = _moe(x2d.astype(BF16)[row_tok], block_expert, row_gate, moe_w_gate, moe_w_up, moe_w_down, i)
        x = _final(x2d, p[i].reshape(t_tok, -1), y_rows[pos[0]], y_rows[pos[1]], ple_w_gate[i].astype(BF16),
                   _row(ple_b_gate[i]), ple_w[i].astype(BF16), _row(ln2_g[i]), _row(ln2_b[i]), tm).reshape(bsz, s, d)
    return x
```

```python
import functools

import jax
import jax.numpy as jnp
import numpy as np
from jax import lax
from jax.experimental import pallas as pl
from jax.experimental.pallas import tpu as pltpu

F32 = jnp.float32
BF16 = jnp.bfloat16

D_MODEL = 1024
DEPTH = 4
HEAD_DIM = 64
N_HEADS = 4
GROUP_WIDTH = N_HEADS * HEAD_DIM
D_PLE = 256

RWKV_DECAY_RANK = 32
RWKV_ICL_RANK = 32
RWKV_GATE_RANK = 64
RWKV_GN_EPS = 64e-5
D_IN_A = 3 * GROUP_WIDTH + RWKV_DECAY_RANK + RWKV_ICL_RANK + RWKV_GATE_RANK

GLA_KEY_DIM = 32
GLA_QK_WIDTH = N_HEADS * GLA_KEY_DIM
GLA_GATE_RANK = 16
GLA_TAU = 16.0
D_IN_B = 2 * GLA_QK_WIDTH + 2 * GROUP_WIDTH + GLA_GATE_RANK

MLSTM_CONV = 4
D_IN_C = 4 * GROUP_WIDTH + 2 * N_HEADS

MLA_Q_RANK = 256
MLA_KV_RANK = 128
MLA_NOPE = 64
MLA_ROPE = 32
MLA_QK = MLA_NOPE + MLA_ROPE
D_IN_D = MLA_Q_RANK + MLA_KV_RANK + MLA_ROPE
ROPE_THETA = 10000.0

N_EXPERT_GROUPS = 4
EXPERTS_PER_GROUP = 8
N_EXPERTS = N_EXPERT_GROUPS * EXPERTS_PER_GROUP
TOP_K = 2
D_EXPERT = 512
MOE_BLOCK = 256

DN_ALPHA = (2.0 * DEPTH) ** 0.25
LN_EPS = 1e-5
NORM_EPS = 1e-6

CHUNK = 64
GLA_SUB = 16
MISC = 128
MISC_GLA_AD = 0
MISC_IG = 16
MISC_FG = 20
MISC_KR = 32
MISC_KR_ROT = 64
VMEM_LIMIT_BYTES = 56 * 1024 * 1024


def _cparams(n_axes):
    return pltpu.CompilerParams(dimension_semantics=("arbitrary",) * n_axes,
                                vmem_limit_bytes=VMEM_LIMIT_BYTES)


def _dot(a, b):
    return jnp.dot(a.astype(BF16), b.astype(BF16), preferred_element_type=F32)


def _dot_nt(a, b):
    return lax.dot_general(a.astype(BF16), b.astype(BF16), (((1,), (1,)), ((), ())),
                           preferred_element_type=F32)


def _dot_tn(a, b):
    return lax.dot_general(a.astype(BF16), b.astype(BF16), (((0,), (0,)), ((), ())),
                           preferred_element_type=F32)


def _split(x):
    hi = x.astype(BF16)
    lo = (x - hi.astype(F32)).astype(BF16)
    return hi, lo


def _dot_sel(x, sel):
    hi, lo = _split(x)
    return (jnp.dot(hi, sel, preferred_element_type=F32) + jnp.dot(lo, sel, preferred_element_type=F32))


def _sel_dot(sel, x):
    hi, lo = _split(x)
    return (jnp.dot(sel, hi, preferred_element_type=F32) + jnp.dot(sel, lo, preferred_element_type=F32))


def _dot_tn_sel(x, sel):
    hi, lo = _split(x)
    dn = (((0,), (0,)), ((), ()))
    return (lax.dot_general(hi, sel, dn, preferred_element_type=F32)
            + lax.dot_general(lo, sel, dn, preferred_element_type=F32))


def _iota(shape, dim):
    return lax.broadcasted_iota(jnp.int32, shape, dim)


def _shr(x, n):
    return lax.shift_right_logical(x, jnp.int32(n))


def _head_mask(rows, cols, row_shift, col_shift):
    return _shr(_iota((rows, cols), 0), row_shift) == _shr(_iota((rows, cols), 1), col_shift)


def _head_ones(rows, cols, row_shift, col_shift):
    return jnp.where(_head_mask(rows, cols, row_shift, col_shift), 1.0, 0.0).astype(BF16)


def _tril_ones(n):
    return jnp.where(_iota((n, n), 1) <= _iota((n, n), 0), 1.0, 0.0).astype(BF16)


def _bd(x, row_shift=6, col_shift=6):
    t = jnp.concatenate([x] * N_HEADS, axis=0)
    return jnp.where(_head_mask(t.shape[0], t.shape[1], row_shift, col_shift), t, 0.0)


def _lane_j(shape):
    return jnp.bitwise_and(_iota(shape, 1), CHUNK - 1)


def _sigmoid(x):
    return 1.0 / (1.0 + jnp.exp(-x))


def _softplus(x):
    return jnp.maximum(x, 0.0) + jnp.log(1.0 + jnp.exp(-jnp.abs(x)))


def _shift_rows(x, prev8, s, row):
    if s == 0:
        return x
    rp = pltpu.roll(prev8, s, 0)
    rp_t = jnp.concatenate([rp] * (x.shape[0] // 8), axis=0)
    return jnp.where(row < s, rp_t, pltpu.roll(x, s, 0))


def _proj_kernel(x_ref, *refs):
    n = len(refs) // 2
    xb = x_ref[...].astype(BF16)
    for w_ref, o_ref in zip(refs[:n], refs[n:]):
        o_ref[...] = jnp.dot(xb, w_ref[...], preferred_element_type=F32)


def _project(x2d, ws, tm):
    t_tok, d = x2d.shape
    return pl.pallas_call(
        _proj_kernel,
        grid=(t_tok // tm,),
        in_specs=[pl.BlockSpec((tm, d), lambda i: (i, 0))]
        + [pl.BlockSpec(w.shape, lambda i: (0, 0)) for w in ws],
        out_specs=[pl.BlockSpec((tm, w.shape[1]), lambda i: (i, 0)) for w in ws],
        out_shape=[jax.ShapeDtypeStruct((t_tok, w.shape[1]), F32) for w in ws],
        compiler_params=_cparams(1),
        name="proj",
    )(x2d, *ws)


def _rwkv_kernel(u_ref, mu_ref, w0_ref, wup_ref, a0_ref, aup_ref, gup_ref, kk_ref, ka_ref, rk_ref,
                 gng_ref, gnb_ref, o_ref,
                 st_ref, prev_ref, r_s, lw_s, k_s, v_s, a_s, b_s, g_s, y_s):
    ct = u_ref.shape[1]
    gw = GROUP_WIDTH

    @pl.when(pl.program_id(1) == 0)
    def _():
        st_ref[...] = jnp.zeros_like(st_ref)
        prev_ref[...] = jnp.zeros_like(prev_ref)

    u = u_ref[0]
    row = _iota(u.shape, 0)
    prev = jnp.where(row == 0, prev_ref[0:1, :], pltpu.roll(u, 1, 0))
    prev_ref[0:1, :] = u_ref[0, ct - 1:ct, :]
    us = u + (prev - u) * mu_ref[...]
    r = us[:, 0:gw]
    k = us[:, gw:2 * gw]
    v = us[:, 2 * gw:3 * gw]
    misc = us[:, 3 * gw:]
    lnl = -_softplus(-(w0_ref[...] + _dot(jnp.tanh(misc), wup_ref[...]))) - 0.5
    a = _sigmoid(a0_ref[...] + _dot(misc, aup_ref[...]))
    ones_bd = _head_ones(gw, gw, 6, 6)
    kkp = k * kk_ref[...]
    kk = kkp / jnp.maximum(jnp.sqrt(_dot_sel(kkp * kkp, ones_bd)), 1e-12)
    k2 = k * (1.0 + (a - 1.0) * ka_ref[...])
    r_s[...] = r
    lw_s[...] = -jnp.exp(lnl)
    k_s[...] = k2
    v_s[...] = v
    a_s[...] = -kk
    b_s[...] = kk * a
    g_s[...] = _dot(_sigmoid(misc), gup_ref[...])

    def chunk_body(ci, carry):
        sl = pl.ds(pl.multiple_of(ci * CHUNK, CHUNK), CHUNK)
        r_c, lw_c, k_c, v_c, a_c, b_c = r_s[sl, :], lw_s[sl, :], k_s[sl, :], v_s[sl, :], a_s[sl, :], b_s[sl, :]
        st = st_ref[...]
        cs = _sel_dot(_tril_ones(CHUNK), lw_c)
        g_incl = jnp.exp(cs)
        g_prev = jnp.exp(cs - lw_c)
        g_inv = jnp.exp(-cs)
        lhs = jnp.concatenate([a_c * g_prev, r_c * g_incl], axis=0)
        sb = _dot_nt(lhs, _bd(b_c * g_inv))
        sk = _dot_nt(lhs, _bd(k_c * g_inv))
        shape = (CHUNK, gw)
        i_idx, j_idx = _iota(shape, 0), _lane_j(shape)
        strict, incl = j_idx < i_idx, j_idx <= i_idx
        a_ab = jnp.where(strict, sb[:CHUNK], 0.0)
        a_rb = jnp.where(incl, sb[CHUNK:], 0.0)
        a_k = jnp.concatenate([jnp.where(strict, sk[:CHUNK], 0.0), jnp.where(incl, sk[CHUNK:], 0.0)], axis=0)
        p = jnp.where(j_idx == i_idx, 1.0, 0.0) + a_ab
        ap = a_ab
        for _ in range(5):
            ap = _dot(ap, _bd(ap))
            p = p + _dot(p, _bd(ap))
        from_state = _dot(lhs, st)
        from_v = _dot(a_k, _bd(v_c))
        uu = _dot(p, _bd(from_state[:CHUNK] + from_v[:CHUNK]))
        y_s[sl, :] = from_state[CHUNK:] + from_v[CHUNK:] + _dot(a_rb, _bd(uu))
        tail = jnp.exp(cs[CHUNK - 1:CHUNK, :] - cs)
        upd = _dot_tn(jnp.concatenate([b_c * tail, k_c * tail], axis=0), jnp.concatenate([uu, v_c], axis=0))
        g_col = jnp.exp(_dot_tn_sel(lw_c, jnp.ones((CHUNK, gw), BF16)))
        st_ref[...] = g_col * st + jnp.where(_head_mask(gw, gw, 6, 6), upd, 0.0)
        return carry

    lax.fori_loop(0, ct // CHUNK, chunk_body, 0)

    y = y_s[...]
    mean = _dot_sel(y, ones_bd) * (1.0 / HEAD_DIM)
    yc = y - mean
    var = _dot_sel(yc * yc, ones_bd) * (1.0 / HEAD_DIM)
    yn = yc * lax.rsqrt(var + RWKV_GN_EPS) * gng_ref[...] + gnb_ref[...]
    bonus = _dot_sel(r_s[...] * k_s[...] * rk_ref[...], ones_bd) * v_s[...]
    o_ref[0] = ((yn + bonus) * g_s[...]).astype(o_ref.dtype)


def _rwkv(u_a, prm, ct):
    bsz, s, _ = u_a.shape
    gw = GROUP_WIDTH
    vec = lambda n: pl.BlockSpec((1, n), lambda b, c: (0, 0))
    mat = lambda m, n: pl.BlockSpec((m, n), lambda b, c: (0, 0))
    tile = lambda n: pltpu.VMEM((ct, n), F32)
    return pl.pallas_call(
        _rwkv_kernel,
        grid=(bsz, s // ct),
        in_specs=[pl.BlockSpec((1, ct, D_IN_A), lambda b, c: (b, c, 0)), vec(D_IN_A), vec(gw), mat(MISC, gw),
                  vec(gw), mat(MISC, gw), mat(MISC, gw), vec(gw), vec(gw), vec(gw), vec(gw), vec(gw)],
        out_specs=pl.BlockSpec((1, ct, gw), lambda b, c: (b, c, 0)),
        out_shape=jax.ShapeDtypeStruct((bsz, s, gw), BF16),
        scratch_shapes=[pltpu.VMEM((gw, gw), F32), pltpu.VMEM((8, D_IN_A), F32)] + [tile(gw)] * 8,
        compiler_params=_cparams(2),
        name="rwkv7",
    )(u_a, *prm)


def _gla_kernel(u_ref, m_ref, aup_ref, ab_ref, ng_ref, o_ref,
                st_ref, q_s, k_s, b_s, v_s, oi_s, ox_s):
    ct = u_ref.shape[1]
    qw, gw = GLA_QK_WIDTH, GROUP_WIDTH

    @pl.when(pl.program_id(1) == 0)
    def _():
        st_ref[...] = jnp.zeros_like(st_ref)

    u = u_ref[0]
    q_s[...] = u[:, 0:qw] * (GLA_KEY_DIM ** -0.5)
    k_s[...] = u[:, qw:2 * qw]
    v_s[...] = u[:, 2 * qw:2 * qw + gw]
    gate = u[:, 2 * qw + gw:]
    log_a = -_softplus(-(_dot(m_ref[0], aup_ref[...]) + ab_ref[...])) * (1.0 / GLA_TAU)
    head_mask = _head_mask(qw, gw, 5, 6)

    b_s[...] = log_a

    def chunk_body(ci, carry):
        sl = pl.ds(pl.multiple_of(ci * CHUNK, CHUNK), CHUNK)
        la_c = b_s[sl, :]
        q_c, k_c, v_c = q_s[sl, :], k_s[sl, :], v_s[sl, :]
        st = st_ref[...]
        b = _sel_dot(_tril_ones(CHUNK), la_c)
        b_s[sl, :] = b
        v_bd = _bd(v_c)
        j_idx = _lane_j((GLA_SUB, gw))
        parts = [jnp.zeros((GLA_SUB, gw), F32)]
        for blk in range(1, CHUNK // GLA_SUB):
            lo = blk * GLA_SUB
            beta = b[lo - 1:lo, :]
            q_blk = q_c[lo:lo + GLA_SUB, :] * jnp.exp(b[lo:lo + GLA_SUB, :] - beta)
            k_sc = k_c * jnp.exp(jnp.minimum(beta - b, 0.0))
            sc = jnp.where(j_idx < lo, _dot_nt(q_blk, _bd(k_sc, 6, 5)), 0.0)
            parts.append(_dot(sc, v_bd))
        ox_s[sl, :] = _dot(q_c * jnp.exp(b), st) + jnp.concatenate(parts, axis=0)
        tail = jnp.exp(b[CHUNK - 1:CHUNK, :] - b)
        g_col = jnp.exp(_dot_tn_sel(la_c, jnp.ones((CHUNK, gw), BF16)))
        st_ref[...] = g_col * st + jnp.where(head_mask, _dot_tn(k_c * tail, v_c), 0.0)
        return carry

    lax.fori_loop(0, ct // CHUNK, chunk_body, 0)

    seg_bcast = jnp.where(head_mask, 1.0, 0.0).astype(BF16)

    def sub_body(sb, carry):
        c0 = pl.multiple_of(sb * GLA_SUB, GLA_SUB)
        sl = pl.ds(c0, GLA_SUB)
        q_t = jnp.concatenate([q_s[sl, :]] * GLA_SUB, axis=0)
        b_t = jnp.concatenate([b_s[sl, :]] * GLA_SUB, axis=0)
        rep = lambda ref, w: jnp.concatenate(
            [jnp.broadcast_to(ref[pl.ds(c0 + j, 1), :], (GLA_SUB, w)) for j in range(GLA_SUB)], axis=0)
        k_r, b_r = rep(k_s, qw), rep(b_s, qw)
        rows = _iota((GLA_SUB * GLA_SUB, qw), 0)
        valid = jnp.bitwise_and(rows, GLA_SUB - 1) >= _shr(rows, 4)
        p = jnp.where(valid, q_t * k_r * jnp.exp(jnp.minimum(b_t - b_r, 0.0)), 0.0)
        sx = jnp.dot(p.astype(BF16), seg_bcast, preferred_element_type=F32)
        o = jnp.zeros((GLA_SUB, gw), F32)
        for j in range(GLA_SUB):
            o = o + sx[j * GLA_SUB:(j + 1) * GLA_SUB, :] * v_s[pl.ds(c0 + j, 1), :]
        oi_s[sl, :] = o
        return carry

    lax.fori_loop(0, ct // GLA_SUB, sub_body, 0, unroll=2)

    o = oi_s[...] + ox_s[...]
    ms = _dot_sel(o * o, _head_ones(gw, gw, 6, 6)) * (1.0 / HEAD_DIM)
    on = o * lax.rsqrt(ms + NORM_EPS) * ng_ref[...]
    o_ref[0] = (on * (gate * _sigmoid(gate))).astype(o_ref.dtype)


def _gla(u_b, misc, prm, ct):
    bsz, s, wb = u_b.shape
    qw, gw = GLA_QK_WIDTH, GROUP_WIDTH
    return pl.pallas_call(
        _gla_kernel,
        grid=(bsz, s // ct),
        in_specs=[pl.BlockSpec((1, ct, wb), lambda b, c: (b, c, 0)),
                  pl.BlockSpec((1, ct, MISC), lambda b, c: (b, c, 0)),
                  pl.BlockSpec((MISC, qw), lambda b, c: (0, 0)),
                  pl.BlockSpec((1, qw), lambda b, c: (0, 0)),
                  pl.BlockSpec((1, gw), lambda b, c: (0, 0))],
        out_specs=pl.BlockSpec((1, ct, gw), lambda b, c: (b, c, 0)),
        out_shape=jax.ShapeDtypeStruct((bsz, s, gw), BF16),
        scratch_shapes=[pltpu.VMEM((qw, gw), F32), pltpu.VMEM((ct, qw), F32), pltpu.VMEM((ct, qw), F32),
                        pltpu.VMEM((ct, qw), F32), pltpu.VMEM((ct, gw), F32), pltpu.VMEM((ct, gw), F32),
                        pltpu.VMEM((ct, gw), F32)],
        compiler_params=_cparams(2),
        name="gla",
    )(u_b, misc, *prm)


def _mlstm_kernel(u_ref, m_ref, cw_ref, cb_ref, eig_ref, efg_ref, ib_ref, fb_ref, ng_ref, o_ref,
                  mem_ref, nb_ref, mst_ref, prev_ref, q_s, k_s, v_s, li_s, lf_s, h_s):
    ct = u_ref.shape[1]
    gw = GROUP_WIDTH

    @pl.when(pl.program_id(1) == 0)
    def _():
        mem_ref[...] = jnp.zeros_like(mem_ref)
        nb_ref[...] = jnp.zeros_like(nb_ref)
        mst_ref[...] = jnp.zeros_like(mst_ref)
        prev_ref[...] = jnp.zeros_like(prev_ref)

    u = u_ref[0]
    qk = u[:, 0:2 * gw]
    row = _iota(qk.shape, 0)
    prev8 = prev_ref[...]
    conv = cb_ref[...] + jnp.zeros_like(qk)
    for j in range(MLSTM_CONV):
        conv = conv + cw_ref[j:j + 1, :] * _shift_rows(qk, prev8, MLSTM_CONV - 1 - j, row)
    prev_ref[...] = u_ref[0, ct - 8:ct, 0:2 * gw]
    qk_act = conv * _sigmoid(conv)
    q_s[...] = qk_act[:, 0:gw]
    k_s[...] = qk_act[:, gw:] * (HEAD_DIM ** -0.5)
    v_s[...] = u[:, 2 * gw:3 * gw]
    o_gate = u[:, 3 * gw:]
    misc = m_ref[0]
    li_s[...] = _dot_sel(misc, eig_ref[...]) + ib_ref[...]
    lf_s[...] = -_softplus(-(_dot_sel(misc, efg_ref[...]) + fb_ref[...]))
    neg_inf = -jnp.inf

    def chunk_body(ci, carry):
        sl = pl.ds(pl.multiple_of(ci * CHUNK, CHUNK), CHUNK)
        q_c, k_c, v_c, li_c, lf_c = q_s[sl, :], k_s[sl, :], v_s[sl, :], li_s[sl, :], lf_s[sl, :]
        mem, nb, m_e = mem_ref[...], nb_ref[...], mst_ref[0:1, :]
        shape = (CHUNK, gw)
        i_idx, j_idx = _iota(shape, 0), _lane_j(shape)
        incl = j_idx <= i_idx
        head_mask = _head_mask(gw, gw, 6, 6)
        fc = _sel_dot(_tril_ones(CHUNK), lf_c)
        g_row = li_c - fc
        ones_avg = jnp.full(shape, 1.0 / HEAD_DIM, BF16)
        g_hi, g_lo = _split(_bd(g_row))
        dn = (((1,), (1,)), ((), ()))
        g_t = (lax.dot_general(ones_avg, g_hi, dn, preferred_element_type=F32)
               + lax.dot_general(ones_avg, g_lo, dn, preferred_element_type=F32))
        log_w = jnp.where(incl, fc + g_t, neg_inf)
        cm = g_row
        for sh in (1, 2, 4, 8, 16, 32):
            cm = jnp.maximum(cm, jnp.where(i_idx < sh, neg_inf, pltpu.roll(cm, sh, 0)))
        log_carry = fc + m_e
        m_row = jnp.maximum(fc + cm, log_carry)
        sc = _dot_nt(q_c, _bd(k_c)) * jnp.exp(log_w - m_row)
        w_c = jnp.exp(log_carry - m_row)
        scb = sc.astype(BF16)
        num = jnp.dot(scb, _bd(v_c).astype(BF16), preferred_element_type=F32) + w_c * _dot(q_c, mem)
        den = (jnp.dot(scb, _head_ones(gw, gw, 6, 6), preferred_element_type=F32) + w_c * _dot(q_c, nb))
        h_s[sl, :] = num / jnp.maximum(jnp.abs(den), jnp.exp(-m_row))
        f_end = fc[CHUNK - 1:CHUNK, :]
        log_kv = f_end - fc + li_c
        m_new = jnp.maximum(f_end + m_e, jnp.max(log_kv, axis=0, keepdims=True))
        kwk = jnp.exp(log_kv - m_new) * k_c
        cd = jnp.exp(f_end + m_e - m_new)
        cd_rows = jnp.where(_iota((16, gw), 0) == 0, cd, 0.0)
        cd_col = _dot_tn_sel(cd_rows, jnp.ones((16, gw), BF16))
        upd = _dot_tn(kwk, jnp.concatenate([v_c, jnp.ones(shape, F32)], axis=1))
        mem_ref[...] = cd_col * mem + jnp.where(head_mask, upd[:, :gw], 0.0)
        nb_ref[...] = cd_col * nb + jnp.where(head_mask, upd[:, gw:], 0.0)
        mst_ref[0:1, :] = m_new
        return carry

    lax.fori_loop(0, ct // CHUNK, chunk_body, 0)

    h = h_s[...]
    ones_bd = _head_ones(gw, gw, 6, 6)
    mean = _dot_sel(h, ones_bd) * (1.0 / HEAD_DIM)
    hc = h - mean
    var = _dot_sel(hc * hc, ones_bd) * (1.0 / HEAD_DIM)
    hn = hc * lax.rsqrt(var + LN_EPS) * ng_ref[...]
    o_ref[0] = (hn * _sigmoid(o_gate)).astype(o_ref.dtype)


def _mlstm(u_c, misc, prm, ct):
    bsz, s, wc = u_c.shape
    gw = GROUP_WIDTH
    vec = lambda n: pl.BlockSpec((1, n), lambda b, c: (0, 0))
    mat = lambda m, n: pl.BlockSpec((m, n), lambda b, c: (0, 0))
    tile = pltpu.VMEM((ct, gw), F32)
    return pl.pallas_call(
        _mlstm_kernel,
        grid=(bsz, s // ct),
        in_specs=[pl.BlockSpec((1, ct, wc), lambda b, c: (b, c, 0)),
                  pl.BlockSpec((1, ct, MISC), lambda b, c: (b, c, 0)),
                  mat(MLSTM_CONV, 2 * gw), vec(2 * gw), mat(MISC, gw), mat(MISC, gw), vec(gw), vec(gw), vec(gw)],
        out_specs=pl.BlockSpec((1, ct, gw), lambda b, c: (b, c, 0)),
        out_shape=jax.ShapeDtypeStruct((bsz, s, gw), BF16),
        scratch_shapes=[pltpu.VMEM((gw, gw), F32), pltpu.VMEM((gw, gw), F32), pltpu.VMEM((8, gw), F32),
                        pltpu.VMEM((8, 2 * gw), F32)] + [tile] * 6,
        compiler_params=_cparams(2),
        name="mlstm",
    )(u_c, misc, *prm)


def _mla_prep_kernel(u_ref, m_ref, cq_ref, sq_ref, csk_ref, qg_ref, kvg_ref, wq_ref, wqr_ref, wk_ref, wv_ref,
                     selk_ref, q_ref, k_ref, v_ref):
    u = u_ref[0]
    cq = u[:, 0:MLA_Q_RANK]
    ckv = u[:, MLA_Q_RANK:]
    cqn = cq * lax.rsqrt(jnp.mean(cq * cq, axis=-1, keepdims=True) + NORM_EPS) * qg_ref[...]
    ckvn = ckv * lax.rsqrt(jnp.mean(ckv * ckv, axis=-1, keepdims=True) + NORM_EPS) * kvg_ref[...]
    cqb, ckvb = cqn.astype(BF16), ckvn.astype(BF16)
    k_rope = _dot(m_ref[0] * csk_ref[0], selk_ref[...])
    cos_q, sin_q = cq_ref[0], sq_ref[0]
    scale = MLA_QK ** -0.5
    for h in range(N_HEADS):
        qh = (jnp.dot(cqb, wq_ref[h], preferred_element_type=F32) * cos_q
              + jnp.dot(cqb, wqr_ref[h], preferred_element_type=F32) * sin_q) * scale
        q_ref[0, h] = qh.astype(BF16)
        k_ref[0, h] = (jnp.dot(ckvb, wk_ref[h], preferred_element_type=F32) + k_rope).astype(BF16)
        v_ref[0, h] = jnp.dot(ckvb, wv_ref[h], preferred_element_type=F32).astype(BF16)


def _mla_prep(u_d, misc, tabs, prm, tm):
    bsz, s, wd = u_d.shape
    full = lambda a: pl.BlockSpec(a.shape, lambda b, c: (0,) * a.ndim)
    tok = lambda n: pl.BlockSpec((1, tm, n), lambda b, c: (b, c, 0))
    head = lambda n: pl.BlockSpec((1, N_HEADS, tm, n), lambda b, c: (b, 0, c, 0))
    return pl.pallas_call(
        _mla_prep_kernel,
        grid=(bsz, s // tm),
        in_specs=[tok(wd), tok(MISC), tok(MLA_QK), tok(MLA_QK), tok(MISC)] + [full(a) for a in prm],
        out_specs=[head(MLA_QK), head(MLA_QK), head(HEAD_DIM)],
        out_shape=[jax.ShapeDtypeStruct((bsz, N_HEADS, s, MLA_QK), BF16),
                   jax.ShapeDtypeStruct((bsz, N_HEADS, s, MLA_QK), BF16),
                   jax.ShapeDtypeStruct((bsz, N_HEADS, s, HEAD_DIM), BF16)],
        compiler_params=_cparams(2),
        name="mla_prep",
    )(u_d, misc, *tabs, *prm)


def _attn_kernel(q_ref, k_ref, v_ref, o_ref):
    tq = q_ref.shape[2]
    qi = pl.program_id(2)
    q = q_ref[0, 0]

    def step(j, carry, diagonal):
        m, l, acc = carry
        sl = pl.ds(pl.multiple_of(j * tq, tq), tq)
        kb, vb = k_ref[0, 0, sl, :], v_ref[0, 0, sl, :]
        s = lax.dot_general(q, kb, (((1,), (1,)), ((), ())), preferred_element_type=F32)
        if diagonal:
            s = jnp.where(_iota((tq, tq), 1) <= _iota((tq, tq), 0), s, -jnp.inf)
        m_new = jnp.maximum(m, jnp.max(s, axis=-1, keepdims=True))
        p = jnp.exp(s - m_new)
        alpha = jnp.exp(m - m_new)
        l = alpha * l + jnp.sum(p, axis=-1, keepdims=True)
        acc = alpha * acc + jnp.dot(p.astype(BF16), vb, preferred_element_type=F32)
        return m_new, l, acc

    init = (jnp.full((tq, 1), -jnp.inf, F32), jnp.zeros((tq, 1), F32), jnp.zeros((tq, HEAD_DIM), F32))
    carry = lax.fori_loop(0, qi, lambda j, c: step(j, c, False), init)
    _, l, acc = step(qi, carry, True)
    o_ref[0, 0] = (acc / l).astype(o_ref.dtype)


def _attention(q, k, v, tq):
    bsz, nh, s, _ = q.shape
    return pl.pallas_call(
        _attn_kernel,
        grid=(bsz, nh, s // tq),
        in_specs=[pl.BlockSpec((1, 1, tq, MLA_QK), lambda b, h, i: (b, h, i, 0)),
                  pl.BlockSpec((1, 1, s, MLA_QK), lambda b, h, i: (b, h, 0, 0)),
                  pl.BlockSpec((1, 1, s, HEAD_DIM), lambda b, h, i: (b, h, 0, 0))],
        out_specs=pl.BlockSpec((1, 1, tq, HEAD_DIM), lambda b, h, i: (b, h, i, 0)),
        out_shape=jax.ShapeDtypeStruct((bsz, nh, s, HEAD_DIM), BF16),
        compiler_params=_cparams(3),
        name="mla_attn",
    )(q, k, v)


def _layer_norm(z, g, b):
    zc = z - jnp.mean(z, axis=-1, keepdims=True)
    return zc * lax.rsqrt(jnp.mean(zc * zc, axis=-1, keepdims=True) + LN_EPS) * g + b


def _outproj_kernel(x_ref, ya_ref, yb_ref, yc_ref, yd_ref, wo_ref, wod_ref, g_ref, b_ref, o_ref):
    gw = GROUP_WIDTH
    mix = jnp.dot(ya_ref[0], wo_ref[0:gw, :], preferred_element_type=F32)
    mix = mix + jnp.dot(yb_ref[0], wo_ref[gw:2 * gw, :], preferred_element_type=F32)
    mix = mix + jnp.dot(yc_ref[0], wo_ref[2 * gw:3 * gw, :], preferred_element_type=F32)
    for h in range(N_HEADS):
        mix = mix + jnp.dot(yd_ref[0, h], wod_ref[h], preferred_element_type=F32)
    o_ref[0] = _layer_norm(DN_ALPHA * x_ref[0] + mix, g_ref[...], b_ref[...])


def _outproj(x, ya, yb, yc, yd, wo, wod, g, b, tm):
    bsz, s, d = x.shape
    gw = GROUP_WIDTH
    tok = lambda n: pl.BlockSpec((1, tm, n), lambda bb, c: (bb, c, 0))
    full = lambda a: pl.BlockSpec(a.shape, lambda bb, c: (0,) * a.ndim)
    return pl.pallas_call(
        _outproj_kernel,
        grid=(bsz, s // tm),
        in_specs=[tok(d), tok(gw), tok(gw), tok(gw),
                  pl.BlockSpec((1, N_HEADS, tm, HEAD_DIM), lambda bb, c: (bb, 0, c, 0)),
                  full(wo), full(wod), full(g), full(b)],
        out_specs=tok(d),
        out_shape=jax.ShapeDtypeStruct((bsz, s, d), F32),
        compiler_params=_cparams(2),
        name="outproj_ln1",
    )(x, ya, yb, yc, yd, wo, wod, g, b)


def _router_kernel(x_ref, w_ref, b_ref, idx_ref, gate_ref):
    x = x_ref[...]
    xh, xl = _split(x)
    wh, wl = _split(w_ref[...])
    dn = (((1,), (1,)), ((), ()))
    logits = (lax.dot_general(wh, xh, dn, preferred_element_type=F32)
              + lax.dot_general(wh, xl, dn, preferred_element_type=F32)
              + lax.dot_general(wl, xh, dn, preferred_element_type=F32)) + b_ref[...]
    tm = x.shape[0]
    ng, ne = N_EXPERT_GROUPS, EXPERTS_PER_GROUP
    gl = [logits[i:i + 1, :] for i in range(ng)]
    gmax = functools.reduce(jnp.maximum, gl)
    gsum = functools.reduce(jnp.add, [jnp.exp(g - gmax) for g in gl])
    group_p = 1.0 / gsum
    gidx = jnp.full((1, tm), ng - 1, jnp.int32)
    for i in range(ng - 2, -1, -1):
        gidx = jnp.where(gl[i] == gmax, i, gidx)
    el = logits[8 + (ng - 1) * ne:8 + ng * ne, :]
    for i in range(ng - 2, -1, -1):
        el = jnp.where(gidx == i, logits[8 + i * ne:8 + (i + 1) * ne, :], el)
    rows = _iota((ne, tm), 0)
    m1 = jnp.max(el, axis=0, keepdims=True)
    i1 = jnp.min(jnp.where(el == m1, rows, ne), axis=0, keepdims=True)
    el2 = jnp.where(rows == i1, -jnp.inf, el)
    m2 = jnp.max(el2, axis=0, keepdims=True)
    i2 = jnp.min(jnp.where(el2 == m2, rows, ne), axis=0, keepdims=True)
    z = jnp.sum(jnp.exp(el - m1), axis=0, keepdims=True)
    p1 = 1.0 / z
    p2 = jnp.exp(m2 - m1) / z
    base = gidx * ne
    idx_ref[...] = jnp.concatenate([base + i1, base + i2], axis=0)
    gate_ref[...] = jnp.concatenate([group_p * p1 / (p1 + p2), group_p * p2 / (p1 + p2)], axis=0)


def _router(x2d, w_t, b_col, tm):
    t_tok, d = x2d.shape
    return pl.pallas_call(
        _router_kernel,
        grid=(t_tok // tm,),
        in_specs=[pl.BlockSpec((tm, d), lambda i: (i, 0)), pl.BlockSpec(w_t.shape, lambda i: (0, 0)),
                  pl.BlockSpec(b_col.shape, lambda i: (0, 0))],
        out_specs=[pl.BlockSpec((TOP_K, tm), lambda i: (0, i)), pl.BlockSpec((TOP_K, tm), lambda i: (0, i))],
        out_shape=[jax.ShapeDtypeStruct((TOP_K, t_tok), jnp.int32), jax.ShapeDtypeStruct((TOP_K, t_tok), F32)],
        compiler_params=_cparams(1),
        name="router",
    )(x2d, w_t, b_col)


def _moe_kernel(be_ref, x_ref, gate_ref, wg_ref, wu_ref, wd_ref, o_ref, wg_s, wu_s, wd_s):
    i = pl.program_id(0)

    @pl.when(jnp.logical_or(i == 0, be_ref[i] != be_ref[jnp.maximum(i - 1, 0)]))
    def _():
        wg_s[...] = wg_ref[0].astype(BF16)
        wu_s[...] = wu_ref[0].astype(BF16)
        wd_s[...] = wd_ref[0].astype(BF16)

    xb = x_ref[...]
    hg = jnp.dot(xb, wg_s[...], preferred_element_type=F32)
    hu = jnp.dot(xb, wu_s[...], preferred_element_type=F32)
    hid = (hg * _sigmoid(hg)) * hu
    y = jnp.dot(hid.astype(BF16), wd_s[...], preferred_element_type=F32) * gate_ref[...]
    o_ref[...] = y.astype(o_ref.dtype)


def _moe(x_rows, block_expert, row_gate, wg, wu, wd, layer):
    n_rows, d = x_rows.shape
    n_blocks = n_rows // MOE_BLOCK
    grid_spec = pltpu.PrefetchScalarGridSpec(
        num_scalar_prefetch=1,
        grid=(n_blocks,),
        in_specs=[pl.BlockSpec((MOE_BLOCK, d), lambda i, be: (i, 0)),
                  pl.BlockSpec((MOE_BLOCK, 1), lambda i, be: (i, 0)),
                  pl.BlockSpec((None, 1, d, D_EXPERT), lambda i, be: (layer, be[i], 0, 0)),
                  pl.BlockSpec((None, 1, d, D_EXPERT), lambda i, be: (layer, be[i], 0, 0)),
                  pl.BlockSpec((None, 1, D_EXPERT, d), lambda i, be: (layer, be[i], 0, 0))],
        out_specs=pl.BlockSpec((MOE_BLOCK, d), lambda i, be: (i, 0)),
        scratch_shapes=[pltpu.VMEM((d, D_EXPERT), BF16), pltpu.VMEM((d, D_EXPERT), BF16),
                        pltpu.VMEM((D_EXPERT, d), BF16)],
    )
    return pl.pallas_call(
        _moe_kernel,
        grid_spec=grid_spec,
        out_shape=jax.ShapeDtypeStruct((n_rows, d), BF16),
        compiler_params=_cparams(1),
        name="moe_experts",
    )(block_expert, x_rows, row_gate.reshape(n_rows, 1), wg, wu, wd)


def _dispatch(idx, gate, t_tok):
    n_asg = t_tok * TOP_K
    flat_e = idx.reshape(n_asg)
    flat_g = gate.reshape(n_asg)
    order = jnp.argsort(flat_e)
    inverse = jnp.argsort(order).astype(jnp.int32)
    starts = jnp.searchsorted(flat_e[order], jnp.arange(N_EXPERTS + 1, dtype=jnp.int32), side='left')
    starts = starts.astype(jnp.int32)
    counts = starts[1:] - starts[:-1]
    starts = starts[:-1]
    padded = (counts + MOE_BLOCK - 1) // MOE_BLOCK * MOE_BLOCK
    pad_ends = jnp.cumsum(padded)
    pad_starts = pad_ends - padded
    n_rows = -(-(n_asg + N_EXPERTS * (MOE_BLOCK - 1)) // MOE_BLOCK) * MOE_BLOCK
    n_blocks = n_rows // MOE_BLOCK
    block_expert = jnp.minimum(jnp.searchsorted(pad_ends, jnp.arange(n_blocks) * MOE_BLOCK, side='right'),
                               N_EXPERTS - 1).astype(jnp.int32)
    pos = pad_starts[flat_e] + inverse - starts[flat_e]
    row_e = jnp.repeat(block_expert, MOE_BLOCK)
    row_rank = jnp.arange(n_rows, dtype=jnp.int32) - pad_starts[row_e]
    valid = row_rank < counts[row_e]
    asg = order[jnp.where(valid, starts[row_e] + row_rank, 0)]
    row_tok = jnp.where(valid, asg % t_tok, jnp.arange(n_rows, dtype=jnp.int32) % t_tok).astype(jnp.int32)
    row_gate = jnp.where(valid, flat_g[asg], 0.0)
    return block_expert, row_tok, pos.reshape(TOP_K, t_tok), row_gate


def _final_kernel(x_ref, p_ref, f0_ref, f1_ref, wg_ref, bg_ref, wp_ref, g_ref, b_ref, o_ref):
    x = x_ref[...]
    gate = _sigmoid(_dot(x, wg_ref[...]) + bg_ref[...])
    ple = gate * _dot(p_ref[...], wp_ref[...])
    ffn = f0_ref[...].astype(F32) + f1_ref[...].astype(F32)
    o_ref[...] = _layer_norm(DN_ALPHA * x + ffn + ple, g_ref[...], b_ref[...])


def _final(x2d, p2d, ffn0, ffn1, wg, bg, wp, g, b, tm):
    t_tok, d = x2d.shape
    full = lambda a: pl.BlockSpec(a.shape, lambda i: (0,) * a.ndim)
    return pl.pallas_call(
        _final_kernel,
        grid=(t_tok // tm,),
        in_specs=[pl.BlockSpec((tm, d), lambda i: (i, 0)), pl.BlockSpec((tm, p2d.shape[1]), lambda i: (i, 0)),
                  pl.BlockSpec((tm, d), lambda i: (i, 0)), pl.BlockSpec((tm, d), lambda i: (i, 0)),
                  full(wg), full(bg), full(wp), full(g), full(b)],
        out_specs=pl.BlockSpec((tm, d), lambda i: (i, 0)),
        out_shape=jax.ShapeDtypeStruct((t_tok, d), F32),
        compiler_params=_cparams(1),
        name="ple_ln2",
    )(x2d, p2d, ffn0, ffn1, wg, bg, wp, g, b)


def _pad_rows(w, offset, total=MISC):
    return jnp.zeros((total, w.shape[1]), w.dtype).at[offset:offset + w.shape[0]].set(w)


def _rot_cols(w):
    half = w.shape[1] // 2
    return jnp.concatenate([-w[:, half:], w[:, :half]], axis=1)


def _row(v):
    return v.reshape(1, -1)


def _expand_heads(v):
    return jnp.repeat(v, HEAD_DIM).reshape(1, GROUP_WIDTH)


def _head_select(offset):
    sel = np.zeros((MISC, GROUP_WIDTH), np.float32)
    for h in range(N_HEADS):
        sel[offset + h, h * HEAD_DIM:(h + 1) * HEAD_DIM] = 1.0
    return jnp.asarray(sel, BF16)


def _rope_tables(positions):
    inv_freq = ROPE_THETA ** (-jnp.arange(0, MLA_ROPE, 2, dtype=F32) / MLA_ROPE)
    ang = positions.astype(F32)[..., None] * inv_freq
    cos, sin = jnp.cos(ang), jnp.sin(ang)
    cos2 = jnp.concatenate([cos, cos], axis=-1)
    sin2 = jnp.concatenate([sin, sin], axis=-1)
    lead = cos.shape[:-1]
    cos_q = jnp.concatenate([jnp.ones(lead + (MLA_NOPE,), F32), cos2], axis=-1)
    sin_q = jnp.concatenate([jnp.zeros(lead + (MLA_NOPE,), F32), sin2], axis=-1)
    cs_k = jnp.concatenate([jnp.zeros(lead + (MISC_KR,), F32), cos2, sin2,
                            jnp.zeros(lead + (MISC - MISC_KR_ROT - MLA_ROPE,), F32)], axis=-1)
    return cos_q, sin_q, cs_k


def _rope_key_select():
    sel = np.zeros((MISC, MLA_QK), np.float32)
    for c in range(MLA_ROPE):
        sel[MISC_KR + c, MLA_NOPE + c] = 1.0
        sel[MISC_KR_ROT + c, MLA_NOPE + c] = 1.0
    return jnp.asarray(sel, BF16)


def _split_w_in(w):
    a0, b0 = 0, D_IN_A
    c0, d0 = b0 + D_IN_B, b0 + D_IN_B + D_IN_C
    qw, gw = GLA_QK_WIDTH, GROUP_WIDTH
    w_a = w[:, a0:b0]
    gla_ad = w[:, b0 + 2 * qw + gw:b0 + 2 * qw + gw + GLA_GATE_RANK]
    w_b = jnp.concatenate([w[:, b0:b0 + 2 * qw + gw], w[:, b0 + 2 * qw + gw + GLA_GATE_RANK:c0]], axis=1)
    w_c = w[:, c0:c0 + 4 * gw]
    gates = w[:, c0 + 4 * gw:d0]
    w_d = w[:, d0:d0 + MLA_Q_RANK + MLA_KV_RANK]
    kr = w[:, d0 + MLA_Q_RANK + MLA_KV_RANK:]
    z = lambda n: jnp.zeros((w.shape[0], n), w.dtype)
    w_m = jnp.concatenate([gla_ad, gates, z(MISC_KR - MISC_FG - N_HEADS), kr, _rot_cols(kr),
                           z(MISC - MISC_KR_ROT - MLA_ROPE)], axis=1)
    return [t.astype(BF16) for t in (w_a, w_b, w_c, w_d, w_m)]


def _tiles(s):
    return min(512, s), min(512, s)


def kernel(x, p, positions, w_in, rwkv_mu, rwkv_w0, rwkv_w_up, rwkv_a0, rwkv_a_up, rwkv_g_up, rwkv_k_k, rwkv_k_a, rwkv_r_k, rwkv_gn_g, rwkv_gn_b, gla_alpha_up, gla_alpha_b, gla_norm_g, mlstm_conv_w, mlstm_conv_b, mlstm_i_b, mlstm_f_b, mlstm_norm_g, mla_q_norm_g, mla_w_uq, mla_kv_norm_g, mla_w_ukv, w_out, ln1_g, ln1_b, moe_w_rg, moe_b_rg, moe_w_re, moe_b_re, moe_w_gate, moe_w_up, moe_w_down, ple_w_gate, ple_b_gate, ple_w, ln2_g, ln2_b):
    bsz, s, d = x.shape
    t_tok = bsz * s
    depth = w_in.shape[0]
    ct, tq = _tiles(s)
    tm = min(512, t_tok)
    rope_tabs = _rope_tables(positions)
    sel_ig, sel_fg, sel_k = _head_select(MISC_IG), _head_select(MISC_FG), _rope_key_select()
    for i in range(depth):
        u_a, u_b, u_c, u_d, u_m = _project(x.reshape(t_tok, d), _split_w_in(w_in[i]), tm)
        u_a, u_b, u_c, u_d, u_m = (t.reshape(bsz, s, -1) for t in (u_a, u_b, u_c, u_d, u_m))

        rk = RWKV_DECAY_RANK
        ya = _rwkv(u_a, (_row(rwkv_mu[i]), _row(rwkv_w0[i]), _pad_rows(rwkv_w_up[i], 0).astype(BF16),
                         _row(rwkv_a0[i]), _pad_rows(rwkv_a_up[i], rk).astype(BF16),
                         _pad_rows(rwkv_g_up[i], rk + RWKV_ICL_RANK).astype(BF16),
                         _row(rwkv_k_k[i]), _row(rwkv_k_a[i]), _row(rwkv_r_k[i]),
                         _row(rwkv_gn_g[i]), _row(rwkv_gn_b[i])), ct)
        yb = _gla(u_b, u_m, (_pad_rows(gla_alpha_up[i], MISC_GLA_AD).astype(BF16), _row(gla_alpha_b[i]),
                             _row(gla_norm_g[i])), ct)
        yc = _mlstm(u_c, u_m, (mlstm_conv_w[i], _row(mlstm_conv_b[i]), sel_ig, sel_fg,
                               _expand_heads(mlstm_i_b[i]), _expand_heads(mlstm_f_b[i]),
                               _row(mlstm_norm_g[i])), ct)

        wq = mla_w_uq[i].reshape(MLA_Q_RANK, N_HEADS, MLA_QK).transpose(1, 0, 2)
        wq_rot = jnp.concatenate([jnp.zeros((N_HEADS, MLA_Q_RANK, MLA_NOPE), F32),
                                  jax.vmap(_rot_cols)(wq[:, :, MLA_NOPE:])], axis=2)
        wkv = mla_w_ukv[i].reshape(MLA_KV_RANK, N_HEADS, MLA_NOPE + HEAD_DIM).transpose(1, 0, 2)
        wk = jnp.concatenate([wkv[:, :, :MLA_NOPE], jnp.zeros((N_HEADS, MLA_KV_RANK, MLA_ROPE), F32)], axis=2)
        q, k, v = _mla_prep(u_d, u_m, rope_tabs,
                            (_row(mla_q_norm_g[i]), _row(mla_kv_norm_g[i]), wq.astype(BF16), wq_rot.astype(BF16),
                             wk.astype(BF16), wkv[:, :, MLA_NOPE:].astype(BF16), sel_k), ct)
        yd = _attention(q, k, v, tq)

        wo = w_out[i].astype(BF16)
        x = _outproj(x, ya, yb, yc, yd, wo[:3 * GROUP_WIDTH], wo[3 * GROUP_WIDTH:].reshape(N_HEADS, HEAD_DIM, d),
                     _row(ln1_g[i]), _row(ln1_b[i]), ct)

        x2d = x.reshape(t_tok, d)
        w_route = jnp.concatenate([moe_w_rg[i].T, jnp.zeros((8 - N_EXPERT_GROUPS, d), F32), moe_w_re[i].T], axis=0)
        b_route = jnp.concatenate([moe_b_rg[i], jnp.zeros((8 - N_EXPERT_GROUPS,), F32), moe_b_re[i]]).reshape(-1, 1)
        idx, gate = _router(x2d, w_route, b_route, tm)
        block_expert, row_tok, pos, row_gate = _dispatch(idx, gate, t_tok)
        y_rows = _moe(x2d.astype(BF16)[row_tok], block_expert, row_gate, moe_w_gate, moe_w_up, moe_w_down, i)
        x = _final(x2d, p[i].reshape(t_tok, -1), y_rows[pos[0]], y_rows[pos[1]], ple_w_gate[i].astype(BF16),
                   _row(ple_b_gate[i]), ple_w[i].astype(BF16), _row(ln2_g[i]), _row(ln2_b[i]), tm).reshape(bsz, s, d)
    return x
```

```python
import functools

import jax
import jax.numpy as jnp
import numpy as np
from jax import lax
from jax.experimental import pallas as pl
from jax.experimental.pallas import tpu as pltpu

F32 = jnp.float32
BF16 = jnp.bfloat16

D_MODEL = 1024
DEPTH = 4
HEAD_DIM = 64
N_HEADS = 4
GROUP_WIDTH = N_HEADS * HEAD_DIM
D_PLE = 256

RWKV_DECAY_RANK = 32
RWKV_ICL_RANK = 32
RWKV_GATE_RANK = 64
RWKV_GN_EPS = 64e-5
D_IN_A = 3 * GROUP_WIDTH + RWKV_DECAY_RANK + RWKV_ICL_RANK + RWKV_GATE_RANK

GLA_KEY_DIM = 32
GLA_QK_WIDTH = N_HEADS * GLA_KEY_DIM
GLA_GATE_RANK = 16
GLA_TAU = 16.0
D_IN_B = 2 * GLA_QK_WIDTH + 2 * GROUP_WIDTH + GLA_GATE_RANK

MLSTM_CONV = 4
D_IN_C = 4 * GROUP_WIDTH + 2 * N_HEADS

MLA_Q_RANK = 256
MLA_KV_RANK = 128
MLA_NOPE = 64
MLA_ROPE = 32
MLA_QK = MLA_NOPE + MLA_ROPE
D_IN_D = MLA_Q_RANK + MLA_KV_RANK + MLA_ROPE
ROPE_THETA = 10000.0

N_EXPERT_GROUPS = 4
EXPERTS_PER_GROUP = 8
N_EXPERTS = N_EXPERT_GROUPS * EXPERTS_PER_GROUP
TOP_K = 2
D_EXPERT = 512
MOE_BLOCK = 256

DN_ALPHA = (2.0 * DEPTH) ** 0.25
LN_EPS = 1e-5
NORM_EPS = 1e-6

CHUNK = 64
GLA_SUB = 16
MISC = 128
MISC_GLA_AD = 0
MISC_IG = 16
MISC_FG = 20
MISC_KR = 32
MISC_KR_ROT = 64
VMEM_LIMIT_BYTES = 56 * 1024 * 1024


def _cparams(n_axes):
    return pltpu.CompilerParams(dimension_semantics=("arbitrary",) * n_axes,
                                vmem_limit_bytes=VMEM_LIMIT_BYTES)


def _dot(a, b):
    return jnp.dot(a.astype(BF16), b.astype(BF16), preferred_element_type=F32)


def _dot_nt(a, b):
    return lax.dot_general(a.astype(BF16), b.astype(BF16), (((1,), (1,)), ((), ())),
                           preferred_element_type=F32)


def _dot_tn(a, b):
    return lax.dot_general(a.astype(BF16), b.astype(BF16), (((0,), (0,)), ((), ())),
                           preferred_element_type=F32)


def _split(x):
    hi = x.astype(BF16)
    lo = (x - hi.astype(F32)).astype(BF16)
    return hi, lo


def _dot_sel(x, sel):
    hi, lo = _split(x)
    return (jnp.dot(hi, sel, preferred_element_type=F32) + jnp.dot(lo, sel, preferred_element_type=F32))


def _sel_dot(sel, x):
    hi, lo = _split(x)
    return (jnp.dot(sel, hi, preferred_element_type=F32) + jnp.dot(sel, lo, preferred_element_type=F32))


def _dot_tn_sel(x, sel):
    hi, lo = _split(x)
    dn = (((0,), (0,)), ((), ()))
    return (lax.dot_general(hi, sel, dn, preferred_element_type=F32)
            + lax.dot_general(lo, sel, dn, preferred_element_type=F32))


def _iota(shape, dim):
    return lax.broadcasted_iota(jnp.int32, shape, dim)


def _shr(x, n):
    return lax.shift_right_logical(x, jnp.int32(n))


def _head_mask(rows, cols, row_shift, col_shift):
    return _shr(_iota((rows, cols), 0), row_shift) == _shr(_iota((rows, cols), 1), col_shift)


def _head_ones(rows, cols, row_shift, col_shift):
    return jnp.where(_head_mask(rows, cols, row_shift, col_shift), 1.0, 0.0).astype(BF16)


def _tril_ones(n):
    return jnp.where(_iota((n, n), 1) <= _iota((n, n), 0), 1.0, 0.0).astype(BF16)


def _bd(x, row_shift=6, col_shift=6):
    t = jnp.concatenate([x] * N_HEADS, axis=0)
    return jnp.where(_head_mask(t.shape[0], t.shape[1], row_shift, col_shift), t, 0.0)


def _lane_j(shape):
    return jnp.bitwise_and(_iota(shape, 1), CHUNK - 1)


def _sigmoid(x):
    return 1.0 / (1.0 + jnp.exp(-x))


def _softplus(x):
    return jnp.maximum(x, 0.0) + jnp.log(1.0 + jnp.exp(-jnp.abs(x)))


def _shift_rows(x, prev8, s, row):
    if s == 0:
        return x
    rp = pltpu.roll(prev8, s, 0)
    rp_t = jnp.concatenate([rp] * (x.shape[0] // 8), axis=0)
    return jnp.where(row < s, rp_t, pltpu.roll(x, s, 0))


def _proj_kernel(x_ref, *refs):
    n = len(refs) // 2
    xb = x_ref[...].astype(BF16)
    for w_ref, o_ref in zip(refs[:n], refs[n:]):
        o_ref[...] = jnp.dot(xb, w_ref[...], preferred_element_type=F32)


def _project(x2d, ws, tm):
    t_tok, d = x2d.shape
    return pl.pallas_call(
        _proj_kernel,
        grid=(t_tok // tm,),
        in_specs=[pl.BlockSpec((tm, d), lambda i: (i, 0))]
        + [pl.BlockSpec(w.shape, lambda i: (0, 0)) for w in ws],
        out_specs=[pl.BlockSpec((tm, w.shape[1]), lambda i: (i, 0)) for w in ws],
        out_shape=[jax.ShapeDtypeStruct((t_tok, w.shape[1]), F32) for w in ws],
        compiler_params=_cparams(1),
        name="proj",
    )(x2d, *ws)


def _rwkv_kernel(u_ref, mu_ref, w0_ref, wup_ref, a0_ref, aup_ref, gup_ref, kk_ref, ka_ref, rk_ref,
                 gng_ref, gnb_ref, o_ref,
                 st_ref, prev_ref, r_s, lw_s, k_s, v_s, a_s, b_s, g_s, y_s):
    ct = u_ref.shape[1]
    gw = GROUP_WIDTH

    @pl.when(pl.program_id(1) == 0)
    def _():
        st_ref[...] = jnp.zeros_like(st_ref)
        prev_ref[...] = jnp.zeros_like(prev_ref)

    u = u_ref[0]
    row = _iota(u.shape, 0)
    prev = jnp.where(row == 0, prev_ref[0:1, :], pltpu.roll(u, 1, 0))
    prev_ref[0:1, :] = u_ref[0, ct - 1:ct, :]
    us = u + (prev - u) * mu_ref[...]
    r = us[:, 0:gw]
    k = us[:, gw:2 * gw]
    v = us[:, 2 * gw:3 * gw]
    misc = us[:, 3 * gw:]
    lnl = -_softplus(-(w0_ref[...] + _dot(jnp.tanh(misc), wup_ref[...]))) - 0.5
    a = _sigmoid(a0_ref[...] + _dot(misc, aup_ref[...]))
    ones_bd = _head_ones(gw, gw, 6, 6)
    kkp = k * kk_ref[...]
    kk = kkp / jnp.maximum(jnp.sqrt(_dot_sel(kkp * kkp, ones_bd)), 1e-12)
    k2 = k * (1.0 + (a - 1.0) * ka_ref[...])
    r_s[...] = r
    lw_s[...] = -jnp.exp(lnl)
    k_s[...] = k2
    v_s[...] = v
    a_s[...] = -kk
    b_s[...] = kk * a
    g_s[...] = _dot(_sigmoid(misc), gup_ref[...])

    def chunk_body(ci, carry):
        sl = pl.ds(pl.multiple_of(ci * CHUNK, CHUNK), CHUNK)
        r_c, lw_c, k_c, v_c, a_c, b_c = r_s[sl, :], lw_s[sl, :], k_s[sl, :], v_s[sl, :], a_s[sl, :], b_s[sl, :]
        st = st_ref[...]
        cs = _sel_dot(_tril_ones(CHUNK), lw_c)
        g_incl = jnp.exp(cs)
        g_prev = jnp.exp(cs - lw_c)
        g_inv = jnp.exp(-cs)
        lhs = jnp.concatenate([a_c * g_prev, r_c * g_incl], axis=0)
        sb = _dot_nt(lhs, _bd(b_c * g_inv))
        sk = _dot_nt(lhs, _bd(k_c * g_inv))
        shape = (CHUNK, gw)
        i_idx, j_idx = _iota(shape, 0), _lane_j(shape)
        strict, incl = j_idx < i_idx, j_idx <= i_idx
        a_ab = jnp.where(strict, sb[:CHUNK], 0.0)
        a_rb = jnp.where(incl, sb[CHUNK:], 0.0)
        a_k = jnp.concatenate([jnp.where(strict, sk[:CHUNK], 0.0), jnp.where(incl, sk[CHUNK:], 0.0)], axis=0)
        p = jnp.where(j_idx == i_idx, 1.0, 0.0) + a_ab
        ap = a_ab
        for _ in range(5):
            ap = _dot(ap, _bd(ap))
            p = p + _dot(p, _bd(ap))
        from_state = _dot(lhs, st)
        from_v = _dot(a_k, _bd(v_c))
        uu = _dot(p, _bd(from_state[:CHUNK] + from_v[:CHUNK]))
        y_s[sl, :] = from_state[CHUNK:] + from_v[CHUNK:] + _dot(a_rb, _bd(uu))
        tail = jnp.exp(cs[CHUNK - 1:CHUNK, :] - cs)
        upd = _dot_tn(jnp.concatenate([b_c * tail, k_c * tail], axis=0), jnp.concatenate([uu, v_c], axis=0))
        g_col = jnp.exp(_dot_tn_sel(lw_c, jnp.ones((CHUNK, gw), BF16)))
        st_ref[...] = g_col * st + jnp.where(_head_mask(gw, gw, 6, 6), upd, 0.0)
        return carry

    lax.fori_loop(0, ct // CHUNK, chunk_body, 0)

    y = y_s[...]
    mean = _dot_sel(y, ones_bd) * (1.0 / HEAD_DIM)
    yc = y - mean
    var = _dot_sel(yc * yc, ones_bd) * (1.0 / HEAD_DIM)
    yn = yc * lax.rsqrt(var + RWKV_GN_EPS) * gng_ref[...] + gnb_ref[...]
    bonus = _dot_sel(r_s[...] * k_s[...] * rk_ref[...], ones_bd) * v_s[...]
    o_ref[0] = ((yn + bonus) * g_s[...]).astype(o_ref.dtype)


def _rwkv(u_a, prm, ct):
    bsz, s, _ = u_a.shape
    gw = GROUP_WIDTH
    vec = lambda n: pl.BlockSpec((1, n), lambda b, c: (0, 0))
    mat = lambda m, n: pl.BlockSpec((m, n), lambda b, c: (0, 0))
    tile = lambda n: pltpu.VMEM((ct, n), F32)
    return pl.pallas_call(
        _rwkv_kernel,
        grid=(bsz, s // ct),
        in_specs=[pl.BlockSpec((1, ct, D_IN_A), lambda b, c: (b, c, 0)), vec(D_IN_A), vec(gw), mat(MISC, gw),
                  vec(gw), mat(MISC, gw), mat(MISC, gw), vec(gw), vec(gw), vec(gw), vec(gw), vec(gw)],
        out_specs=pl.BlockSpec((1, ct, gw), lambda b, c: (b, c, 0)),
        out_shape=jax.ShapeDtypeStruct((bsz, s, gw), BF16),
        scratch_shapes=[pltpu.VMEM((gw, gw), F32), pltpu.VMEM((8, D_IN_A), F32)] + [tile(gw)] * 8,
        compiler_params=_cparams(2),
        name="rwkv7",
    )(u_a, *prm)


def _gla_kernel(u_ref, m_ref, aup_ref, ab_ref, ng_ref, o_ref,
                st_ref, q_s, k_s, b_s, v_s, oi_s, ox_s):
    ct = u_ref.shape[1]
    qw, gw = GLA_QK_WIDTH, GROUP_WIDTH

    @pl.when(pl.program_id(1) == 0)
    def _():
        st_ref[...] = jnp.zeros_like(st_ref)

    u = u_ref[0]
    q_s[...] = u[:, 0:qw] * (GLA_KEY_DIM ** -0.5)
    k_s[...] = u[:, qw:2 * qw]
    v_s[...] = u[:, 2 * qw:2 * qw + gw]
    gate = u[:, 2 * qw + gw:]
    log_a = -_softplus(-(_dot(m_ref[0], aup_ref[...]) + ab_ref[...])) * (1.0 / GLA_TAU)
    head_mask = _head_mask(qw, gw, 5, 6)

    b_s[...] = log_a

    def chunk_body(ci, carry):
        sl = pl.ds(pl.multiple_of(ci * CHUNK, CHUNK), CHUNK)
        la_c = b_s[sl, :]
        q_c, k_c, v_c = q_s[sl, :], k_s[sl, :], v_s[sl, :]
        st = st_ref[...]
        b = _sel_dot(_tril_ones(CHUNK), la_c)
        b_s[sl, :] = b
        v_bd = _bd(v_c)
        j_idx = _lane_j((GLA_SUB, gw))
        parts = [jnp.zeros((GLA_SUB, gw), F32)]
        for blk in range(1, CHUNK // GLA_SUB):
            lo = blk * GLA_SUB
            beta = b[lo - 1:lo, :]
            q_blk = q_c[lo:lo + GLA_SUB, :] * jnp.exp(b[lo:lo + GLA_SUB, :] - beta)
            k_sc = k_c * jnp.exp(jnp.minimum(beta - b, 0.0))
            sc = jnp.where(j_idx < lo, _dot_nt(q_blk, _bd(k_sc, 6, 5)), 0.0)
            parts.append(_dot(sc, v_bd))
        ox_s[sl, :] = _dot(q_c * jnp.exp(b), st) + jnp.concatenate(parts, axis=0)
        tail = jnp.exp(b[CHUNK - 1:CHUNK, :] - b)
        g_col = jnp.exp(_dot_tn_sel(la_c, jnp.ones((CHUNK, gw), BF16)))
        st_ref[...] = g_col * st + jnp.where(head_mask, _dot_tn(k_c * tail, v_c), 0.0)
        return carry

    lax.fori_loop(0, ct // CHUNK, chunk_body, 0)

    seg_bcast = jnp.where(head_mask, 1.0, 0.0).astype(BF16)

    def sub_body(sb, carry):
        c0 = pl.multiple_of(sb * GLA_SUB, GLA_SUB)
        sl = pl.ds(c0, GLA_SUB)
        q_t = jnp.concatenate([q_s[sl, :]] * GLA_SUB, axis=0)
        b_t = jnp.concatenate([b_s[sl, :]] * GLA_SUB, axis=0)
        rep = lambda ref, w: jnp.concatenate(
            [jnp.broadcast_to(ref[pl.ds(c0 + j, 1), :], (GLA_SUB, w)) for j in range(GLA_SUB)], axis=0)
        k_r, b_r = rep(k_s, qw), rep(b_s, qw)
        rows = _iota((GLA_SUB * GLA_SUB, qw), 0)
        valid = jnp.bitwise_and(rows, GLA_SUB - 1) >= _shr(rows, 4)
        p = jnp.where(valid, q_t * k_r * jnp.exp(jnp.minimum(b_t - b_r, 0.0)), 0.0)
        sx = jnp.dot(p.astype(BF16), seg_bcast, preferred_element_type=F32)
        o = jnp.zeros((GLA_SUB, gw), F32)
        for j in range(GLA_SUB):
            o = o + sx[j * GLA_SUB:(j + 1) * GLA_SUB, :] * v_s[pl.ds(c0 + j, 1), :]
        oi_s[sl, :] = o
        return carry

    lax.fori_loop(0, ct // GLA_SUB, sub_body, 0, unroll=2)

    o = oi_s[...] + ox_s[...]
    ms = _dot_sel(o * o, _head_ones(gw, gw, 6, 6)) * (1.0 / HEAD_DIM)
    on = o * lax.rsqrt(ms + NORM_EPS) * ng_ref[...]
    o_ref[0] = (on * (gate * _sigmoid(gate))).astype(o_ref.dtype)


def _gla(u_b, misc, prm, ct):
    bsz, s, wb = u_b.shape
    qw, gw = GLA_QK_WIDTH, GROUP_WIDTH
    return pl.pallas_call(
        _gla_kernel,
        grid=(bsz, s // ct),
        in_specs=[pl.BlockSpec((1, ct, wb), lambda b, c: (b, c, 0)),
                  pl.BlockSpec((1, ct, MISC), lambda b, c: (b, c, 0)),
                  pl.BlockSpec((MISC, qw), lambda b, c: (0, 0)),
                  pl.BlockSpec((1, qw), lambda b, c: (0, 0)),
                  pl.BlockSpec((1, gw), lambda b, c: (0, 0))],
        out_specs=pl.BlockSpec((1, ct, gw), lambda b, c: (b, c, 0)),
        out_shape=jax.ShapeDtypeStruct((bsz, s, gw), BF16),
        scratch_shapes=[pltpu.VMEM((qw, gw), F32), pltpu.VMEM((ct, qw), F32), pltpu.VMEM((ct, qw), F32),
                        pltpu.VMEM((ct, qw), F32), pltpu.VMEM((ct, gw), F32), pltpu.VMEM((ct, gw), F32),
                        pltpu.VMEM((ct, gw), F32)],
        compiler_params=_cparams(2),
        name="gla",
    )(u_b, misc, *prm)


def _mlstm_kernel(u_ref, m_ref, cw_ref, cb_ref, eig_ref, efg_ref, ib_ref, fb_ref, ng_ref, o_ref,
                  mem_ref, nb_ref, mst_ref, prev_ref, q_s, k_s, v_s, li_s, lf_s, h_s):
    ct = u_ref.shape[1]
    gw = GROUP_WIDTH

    @pl.when(pl.program_id(1) == 0)
    def _():
        mem_ref[...] = jnp.zeros_like(mem_ref)
        nb_ref[...] = jnp.zeros_like(nb_ref)
        mst_ref[...] = jnp.zeros_like(mst_ref)
        prev_ref[...] = jnp.zeros_like(prev_ref)

    u = u_ref[0]
    qk = u[:, 0:2 * gw]
    row = _iota(qk.shape, 0)
    prev8 = prev_ref[...]
    conv = cb_ref[...] + jnp.zeros_like(qk)
    for j in range(MLSTM_CONV):
        conv = conv + cw_ref[j:j + 1, :] * _shift_rows(qk, prev8, MLSTM_CONV - 1 - j, row)
    prev_ref[...] = u_ref[0, ct - 8:ct, 0:2 * gw]
    qk_act = conv * _sigmoid(conv)
    q_s[...] = qk_act[:, 0:gw]
    k_s[...] = qk_act[:, gw:] * (HEAD_DIM ** -0.5)
    v_s[...] = u[:, 2 * gw:3 * gw]
    o_gate = u[:, 3 * gw:]
    misc = m_ref[0]
    li_s[...] = _dot_sel(misc, eig_ref[...]) + ib_ref[...]
    lf_s[...] = -_softplus(-(_dot_sel(misc, efg_ref[...]) + fb_ref[...]))
    neg_inf = -jnp.inf

    def chunk_body(ci, carry):
        sl = pl.ds(pl.multiple_of(ci * CHUNK, CHUNK), CHUNK)
        q_c, k_c, v_c, li_c, lf_c = q_s[sl, :], k_s[sl, :], v_s[sl, :], li_s[sl, :], lf_s[sl, :]
        mem, nb, m_e = mem_ref[...], nb_ref[...], mst_ref[0:1, :]
        shape = (CHUNK, gw)
        i_idx, j_idx = _iota(shape, 0), _lane_j(shape)
        incl = j_idx <= i_idx
        head_mask = _head_mask(gw, gw, 6, 6)
        fc = _sel_dot(_tril_ones(CHUNK), lf_c)
        g_row = li_c - fc
        ones_avg = jnp.full(shape, 1.0 / HEAD_DIM, BF16)
        g_hi, g_lo = _split(_bd(g_row))
        dn = (((1,), (1,)), ((), ()))
        g_t = (lax.dot_general(ones_avg, g_hi, dn, preferred_element_type=F32)
               + lax.dot_general(ones_avg, g_lo, dn, preferred_element_type=F32))
        log_w = jnp.where(incl, fc + g_t, neg_inf)
        cm = g_row
        for sh in (1, 2, 4, 8, 16, 32):
            cm = jnp.maximum(cm, jnp.where(i_idx < sh, neg_inf, pltpu.roll(cm, sh, 0)))
        log_carry = fc + m_e
        m_row = jnp.maximum(fc + cm, log_carry)
        sc = _dot_nt(q_c, _bd(k_c)) * jnp.exp(log_w - m_row)
        w_c = jnp.exp(log_carry - m_row)
        scb = sc.astype(BF16)
        num = jnp.dot(scb, _bd(v_c).astype(BF16), preferred_element_type=F32) + w_c * _dot(q_c, mem)
        den = (jnp.dot(scb, _head_ones(gw, gw, 6, 6), preferred_element_type=F32) + w_c * _dot(q_c, nb))
        h_s[sl, :] = num / jnp.maximum(jnp.abs(den), jnp.exp(-m_row))
        f_end = fc[CHUNK - 1:CHUNK, :]
        log_kv = f_end - fc + li_c
        m_new = jnp.maximum(f_end + m_e, jnp.max(log_kv, axis=0, keepdims=True))
        kwk = jnp.exp(log_kv - m_new) * k_c
        cd = jnp.exp(f_end + m_e - m_new)
        cd_rows = jnp.where(_iota((16, gw), 0) == 0, cd, 0.0)
        cd_col = _dot_tn_sel(cd_rows, jnp.ones((16, gw), BF16))
        upd = _dot_tn(kwk, jnp.concatenate([v_c, jnp.ones(shape, F32)], axis=1))
        mem_ref[...] = cd_col * mem + jnp.where(head_mask, upd[:, :gw], 0.0)
        nb_ref[...] = cd_col * nb + jnp.where(head_mask, upd[:, gw:], 0.0)
        mst_ref[0:1, :] = m_new
        return carry

    lax.fori_loop(0, ct // CHUNK, chunk_body, 0)

    h = h_s[...]
    ones_bd = _head_ones(gw, gw, 6, 6)
    mean = _dot_sel(h, ones_bd) * (1.0 / HEAD_DIM)
    hc = h - mean
    var = _dot_sel(hc * hc, ones_bd) * (1.0 / HEAD_DIM)
    hn = hc * lax.rsqrt(var + LN_EPS) * ng_ref[...]
    o_ref[0] = (hn * _sigmoid(o_gate)).astype(o_ref.dtype)


def _mlstm(u_c, misc, prm, ct):
    bsz, s, wc = u_c.shape
    gw = GROUP_WIDTH
    vec = lambda n: pl.BlockSpec((1, n), lambda b, c: (0, 0))
    mat = lambda m, n: pl.BlockSpec((m, n), lambda b, c: (0, 0))
    tile = pltpu.VMEM((ct, gw), F32)
    return pl.pallas_call(
        _mlstm_kernel,
        grid=(bsz, s // ct),
        in_specs=[pl.BlockSpec((1, ct, wc), lambda b, c: (b, c, 0)),
                  pl.BlockSpec((1, ct, MISC), lambda b, c: (b, c, 0)),
                  mat(MLSTM_CONV, 2 * gw), vec(2 * gw), mat(MISC, gw), mat(MISC, gw), vec(gw), vec(gw), vec(gw)],
        out_specs=pl.BlockSpec((1, ct, gw), lambda b, c: (b, c, 0)),
        out_shape=jax.ShapeDtypeStruct((bsz, s, gw), BF16),
        scratch_shapes=[pltpu.VMEM((gw, gw), F32), pltpu.VMEM((gw, gw), F32), pltpu.VMEM((8, gw), F32),
                        pltpu.VMEM((8, 2 * gw), F32)] + [tile] * 6,
        compiler_params=_cparams(2),
        name="mlstm",
    )(u_c, misc, *prm)


def _mla_prep_kernel(u_ref, m_ref, cq_ref, sq_ref, csk_ref, qg_ref, kvg_ref, wq_ref, wqr_ref, wk_ref, wv_ref,
                     selk_ref, q_ref, k_ref, v_ref):
    u = u_ref[0]
    cq = u[:, 0:MLA_Q_RANK]
    ckv = u[:, MLA_Q_RANK:]
    cqn = cq * lax.rsqrt(jnp.mean(cq * cq, axis=-1, keepdims=True) + NORM_EPS) * qg_ref[...]
    ckvn = ckv * lax.rsqrt(jnp.mean(ckv * ckv, axis=-1, keepdims=True) + NORM_EPS) * kvg_ref[...]
    cqb, ckvb = cqn.astype(BF16), ckvn.astype(BF16)
    k_rope = _dot(m_ref[0] * csk_ref[0], selk_ref[...])
    cos_q, sin_q = cq_ref[0], sq_ref[0]
    scale = MLA_QK ** -0.5
    for h in range(N_HEADS):
        qh = (jnp.dot(cqb, wq_ref[h], preferred_element_type=F32) * cos_q
              + jnp.dot(cqb, wqr_ref[h], preferred_element_type=F32) * sin_q) * scale
        q_ref[0, h] = qh.astype(BF16)
        k_ref[0, h] = (jnp.dot(ckvb, wk_ref[h], preferred_element_type=F32) + k_rope).astype(BF16)
        v_ref[0, h] = jnp.dot(ckvb, wv_ref[h], preferred_element_type=F32).astype(BF16)


def _mla_prep(u_d, misc, tabs, prm, tm):
    bsz, s, wd = u_d.shape
    full = lambda a: pl.BlockSpec(a.shape, lambda b, c: (0,) * a.ndim)
    tok = lambda n: pl.BlockSpec((1, tm, n), lambda b, c: (b, c, 0))
    head = lambda n: pl.BlockSpec((1, N_HEADS, tm, n), lambda b, c: (b, 0, c, 0))
    return pl.pallas_call(
        _mla_prep_kernel,
        grid=(bsz, s // tm),
        in_specs=[tok(wd), tok(MISC), tok(MLA_QK), tok(MLA_QK), tok(MISC)] + [full(a) for a in prm],
        out_specs=[head(MLA_QK), head(MLA_QK), head(HEAD_DIM)],
        out_shape=[jax.ShapeDtypeStruct((bsz, N_HEADS, s, MLA_QK), BF16),
                   jax.ShapeDtypeStruct((bsz, N_HEADS, s, MLA_QK), BF16),
                   jax.ShapeDtypeStruct((bsz, N_HEADS, s, HEAD_DIM), BF16)],
        compiler_params=_cparams(2),
        name="mla_prep",
    )(u_d, misc, *tabs, *prm)


def _attn_kernel(q_ref, k_ref, v_ref, o_ref):
    tq = q_ref.shape[2]
    qi = pl.program_id(2)
    q = q_ref[0, 0]

    def step(j, carry, diagonal):
        m, l, acc = carry
        sl = pl.ds(pl.multiple_of(j * tq, tq), tq)
        kb, vb = k_ref[0, 0, sl, :], v_ref[0, 0, sl, :]
        s = lax.dot_general(q, kb, (((1,), (1,)), ((), ())), preferred_element_type=F32)
        if diagonal:
            s = jnp.where(_iota((tq, tq), 1) <= _iota((tq, tq), 0), s, -jnp.inf)
        m_new = jnp.maximum(m, jnp.max(s, axis=-1, keepdims=True))
        p = jnp.exp(s - m_new)
        alpha = jnp.exp(m - m_new)
        l = alpha * l + jnp.sum(p, axis=-1, keepdims=True)
        acc = alpha * acc + jnp.dot(p.astype(BF16), vb, preferred_element_type=F32)
        return m_new, l, acc

    init = (jnp.full((tq, 1), -jnp.inf, F32), jnp.zeros((tq, 1), F32), jnp.zeros((tq, HEAD_DIM), F32))
    carry = lax.fori_loop(0, qi, lambda j, c: step(j, c, False), init)
    _, l, acc = step(qi, carry, True)
    o_ref[0, 0] = (acc / l).astype(o_ref.dtype)


def _attention(q, k, v, tq):
    bsz, nh, s, _ = q.shape
    return pl.pallas_call(
        _attn_kernel,
        grid=(bsz, nh, s // tq),
        in_specs=[pl.BlockSpec((1, 1, tq, MLA_QK), lambda b, h, i: (b, h, i, 0)),
                  pl.BlockSpec((1, 1, s, MLA_QK), lambda b, h, i: (b, h, 0, 0)),
                  pl.BlockSpec((1, 1, s, HEAD_DIM), lambda b, h, i: (b, h, 0, 0))],
        out_specs=pl.BlockSpec((1, 1, tq, HEAD_DIM), lambda b, h, i: (b, h, i, 0)),
        out_shape=jax.ShapeDtypeStruct((bsz, nh, s, HEAD_DIM), BF16),
        compiler_params=_cparams(3),
        name="mla_attn",
    )(q, k, v)


def _layer_norm(z, g, b):
    zc = z - jnp.mean(z, axis=-1, keepdims=True)
    return zc * lax.rsqrt(jnp.mean(zc * zc, axis=-1, keepdims=True) + LN_EPS) * g + b


def _outproj_kernel(x_ref, ya_ref, yb_ref, yc_ref, yd_ref, wo_ref, wod_ref, g_ref, b_ref, o_ref):
    gw = GROUP_WIDTH
    mix = jnp.dot(ya_ref[0], wo_ref[0:gw, :], preferred_element_type=F32)
    mix = mix + jnp.dot(yb_ref[0], wo_ref[gw:2 * gw, :], preferred_element_type=F32)
    mix = mix + jnp.dot(yc_ref[0], wo_ref[2 * gw:3 * gw, :], preferred_element_type=F32)
    for h in range(N_HEADS):
        mix = mix + jnp.dot(yd_ref[0, h], wod_ref[h], preferred_element_type=F32)
    o_ref[0] = _layer_norm(DN_ALPHA * x_ref[0] + mix, g_ref[...], b_ref[...])


def _outproj(x, ya, yb, yc, yd, wo, wod, g, b, tm):
    bsz, s, d = x.shape
    gw = GROUP_WIDTH
    tok = lambda n: pl.BlockSpec((1, tm, n), lambda bb, c: (bb, c, 0))
    full = lambda a: pl.BlockSpec(a.shape, lambda bb, c: (0,) * a.ndim)
    return pl.pallas_call(
        _outproj_kernel,
        grid=(bsz, s // tm),
        in_specs=[tok(d), tok(gw), tok(gw), tok(gw),
                  pl.BlockSpec((1, N_HEADS, tm, HEAD_DIM), lambda bb, c: (bb, 0, c, 0)),
                  full(wo), full(wod), full(g), full(b)],
        out_specs=tok(d),
        out_shape=jax.ShapeDtypeStruct((bsz, s, d), F32),
        compiler_params=_cparams(2),
        name="outproj_ln1",
    )(x, ya, yb, yc, yd, wo, wod, g, b)


def _router_kernel(x_ref, w_ref, b_ref, idx_ref, gate_ref):
    x = x_ref[...]
    xh, xl = _split(x)
    wh, wl = _split(w_ref[...])
    dn = (((1,), (1,)), ((), ()))
    logits = (lax.dot_general(wh, xh, dn, preferred_element_type=F32)
              + lax.dot_general(wh, xl, dn, preferred_element_type=F32)
              + lax.dot_general(wl, xh, dn, preferred_element_type=F32)) + b_ref[...]
    tm = x.shape[0]
    ng, ne = N_EXPERT_GROUPS, EXPERTS_PER_GROUP
    gl = [logits[i:i + 1, :] for i in range(ng)]
    gmax = functools.reduce(jnp.maximum, gl)
    gsum = functools.reduce(jnp.add, [jnp.exp(g - gmax) for g in gl])
    group_p = 1.0 / gsum
    gidx = jnp.full((1, tm), ng - 1, jnp.int32)
    for i in range(ng - 2, -1, -1):
        gidx = jnp.where(gl[i] == gmax, i, gidx)
    el = logits[8 + (ng - 1) * ne:8 + ng * ne, :]
    for i in range(ng - 2, -1, -1):
        el = jnp.where(gidx == i, logits[8 + i * ne:8 + (i + 1) * ne, :], el)
    rows = _iota((ne, tm), 0)
    m1 = jnp.max(el, axis=0, keepdims=True)
    i1 = jnp.min(jnp.where(el == m1, rows, ne), axis=0, keepdims=True)
    el2 = jnp.where(rows == i1, -jnp.inf, el)
    m2 = jnp.max(el2, axis=0, keepdims=True)
    i2 = jnp.min(jnp.where(el2 == m2, rows, ne), axis=0, keepdims=True)
    z = jnp.sum(jnp.exp(el - m1), axis=0, keepdims=True)
    p1 = 1.0 / z
    p2 = jnp.exp(m2 - m1) / z
    base = gidx * ne
    idx_ref[...] = jnp.concatenate([base + i1, base + i2], axis=0)
    gate_ref[...] = jnp.concatenate([group_p * p1 / (p1 + p2), group_p * p2 / (p1 + p2)], axis=0)


def _router(x2d, w_t, b_col, tm):
    t_tok, d = x2d.shape
    return pl.pallas_call(
        _router_kernel,
        grid=(t_tok // tm,),
        in_specs=[pl.BlockSpec((tm, d), lambda i: (i, 0)), pl.BlockSpec(w_t.shape, lambda i: (0, 0)),
                  pl.BlockSpec(b_col.shape, lambda i: (0, 0))],
        out_specs=[pl.BlockSpec((TOP_K, tm), lambda i: (0, i)), pl.BlockSpec((TOP_K, tm), lambda i: (0, i))],
        out_shape=[jax.ShapeDtypeStruct((TOP_K, t_tok), jnp.int32), jax.ShapeDtypeStruct((TOP_K, t_tok), F32)],
        compiler_params=_cparams(1),
        name="router",
    )(x2d, w_t, b_col)


def _moe_kernel(be_ref, x_ref, gate_ref, wg_ref, wu_ref, wd_ref, o_ref, wg_s, wu_s, wd_s):
    i = pl.program_id(0)

    @pl.when(jnp.logical_or(i == 0, be_ref[i] != be_ref[jnp.maximum(i - 1, 0)]))
    def _():
        wg_s[...] = wg_ref[0].astype(BF16)
        wu_s[...] = wu_ref[0].astype(BF16)
        wd_s[...] = wd_ref[0].astype(BF16)

    xb = x_ref[...]
    hg = jnp.dot(xb, wg_s[...], preferred_element_type=F32)
    hu = jnp.dot(xb, wu_s[...], preferred_element_type=F32)
    hid = (hg * _sigmoid(hg)) * hu
    y = jnp.dot(hid.astype(BF16), wd_s[...], preferred_element_type=F32) * gate_ref[...]
    o_ref[...] = y.astype(o_ref.dtype)


def _moe(x_rows, block_expert, row_gate, wg, wu, wd, layer):
    n_rows, d = x_rows.shape
    n_blocks = n_rows // MOE_BLOCK
    grid_spec = pltpu.PrefetchScalarGridSpec(
        num_scalar_prefetch=1,
        grid=(n_blocks,),
        in_specs=[pl.BlockSpec((MOE_BLOCK, d), lambda i, be: (i, 0)),
                  pl.BlockSpec((MOE_BLOCK, 1), lambda i, be: (i, 0)),
                  pl.BlockSpec((None, 1, d, D_EXPERT), lambda i, be: (layer, be[i], 0, 0)),
                  pl.BlockSpec((None, 1, d, D_EXPERT), lambda i, be: (layer, be[i], 0, 0)),
                  pl.BlockSpec((None, 1, D_EXPERT, d), lambda i, be: (layer, be[i], 0, 0))],
        out_specs=pl.BlockSpec((MOE_BLOCK, d), lambda i, be: (i, 0)),
        scratch_shapes=[pltpu.VMEM((d, D_EXPERT), BF16), pltpu.VMEM((d, D_EXPERT), BF16),
                        pltpu.VMEM((D_EXPERT, d), BF16)],
    )
    return pl.pallas_call(
        _moe_kernel,
        grid_spec=grid_spec,
        out_shape=jax.ShapeDtypeStruct((n_rows, d), BF16),
        compiler_params=_cparams(1),
        name="moe_experts",
    )(block_expert, x_rows, row_gate.reshape(n_rows, 1), wg, wu, wd)


def _dispatch(idx, gate, t_tok):
    n_asg = t_tok * TOP_K
    flat_e = idx.reshape(n_asg)
    flat_g = gate.reshape(n_asg)
    order = jnp.argsort(flat_e)
    inverse = jnp.argsort(order).astype(jnp.int32)
    starts = jnp.searchsorted(flat_e[order], jnp.arange(N_EXPERTS + 1, dtype=jnp.int32), side='left')
    starts = starts.astype(jnp.int32)
    counts = starts[1:] - starts[:-1]
    starts = starts[:-1]
    padded = (counts + MOE_BLOCK - 1) // MOE_BLOCK * MOE_BLOCK
    pad_ends = jnp.cumsum(padded)
    pad_starts = pad_ends - padded
    n_rows = -(-(n_asg + N_EXPERTS * (MOE_BLOCK - 1)) // MOE_BLOCK) * MOE_BLOCK
    n_blocks = n_rows // MOE_BLOCK
    block_expert = jnp.minimum(jnp.searchsorted(pad_ends, jnp.arange(n_blocks) * MOE_BLOCK, side='right'),
                               N_EXPERTS - 1).astype(jnp.int32)
    pos = pad_starts[flat_e] + inverse - starts[flat_e]
    rows = jnp.arange(n_rows, dtype=jnp.int32)
    valid = rows < jnp.repeat((pad_starts + counts)[block_expert], MOE_BLOCK)
    asg = order[jnp.where(valid, rows + jnp.repeat((starts - pad_starts)[block_expert], MOE_BLOCK), 0)]
    row_tok = jnp.where(valid, asg % t_tok, jnp.arange(n_rows, dtype=jnp.int32) % t_tok).astype(jnp.int32)
    row_gate = jnp.where(valid, flat_g[asg], 0.0)
    return block_expert, row_tok, pos.reshape(TOP_K, t_tok), row_gate


def _final_kernel(x_ref, p_ref, f0_ref, f1_ref, wg_ref, bg_ref, wp_ref, g_ref, b_ref, o_ref):
    x = x_ref[...]
    gate = _sigmoid(_dot(x, wg_ref[...]) + bg_ref[...])
    ple = gate * _dot(p_ref[...], wp_ref[...])
    ffn = f0_ref[...].astype(F32) + f1_ref[...].astype(F32)
    o_ref[...] = _layer_norm(DN_ALPHA * x + ffn + ple, g_ref[...], b_ref[...])


def _final(x2d, p2d, ffn0, ffn1, wg, bg, wp, g, b, tm):
    t_tok, d = x2d.shape
    full = lambda a: pl.BlockSpec(a.shape, lambda i: (0,) * a.ndim)
    return pl.pallas_call(
        _final_kernel,
        grid=(t_tok // tm,),
        in_specs=[pl.BlockSpec((tm, d), lambda i: (i, 0)), pl.BlockSpec((tm, p2d.shape[1]), lambda i: (i, 0)),
                  pl.BlockSpec((tm, d), lambda i: (i, 0)), pl.BlockSpec((tm, d), lambda i: (i, 0)),
                  full(wg), full(bg), full(wp), full(g), full(b)],
        out_specs=pl.BlockSpec((tm, d), lambda i: (i, 0)),
        out_shape=jax.ShapeDtypeStruct((t_tok, d), F32),
        compiler_params=_cparams(1),
        name="ple_ln2",
    )(x2d, p2d, ffn0, ffn1, wg, bg, wp, g, b)


def _pad_rows(w, offset, total=MISC):
    return jnp.zeros((total, w.shape[1]), w.dtype).at[offset:offset + w.shape[0]].set(w)


def _rot_cols(w):
    half = w.shape[1] // 2
    return jnp.concatenate([-w[:, half:], w[:, :half]], axis=1)


def _row(v):
    return v.reshape(1, -1)


def _expand_heads(v):
    return jnp.repeat(v, HEAD_DIM).reshape(1, GROUP_WIDTH)


def _head_select(offset):
    sel = np.zeros((MISC, GROUP_WIDTH), np.float32)
    for h in range(N_HEADS):
        sel[offset + h, h * HEAD_DIM:(h + 1) * HEAD_DIM] = 1.0
    return jnp.asarray(sel, BF16)


def _rope_tables(positions):
    inv_freq = ROPE_THETA ** (-jnp.arange(0, MLA_ROPE, 2, dtype=F32) / MLA_ROPE)
    ang = positions.astype(F32)[..., None] * inv_freq
    cos, sin = jnp.cos(ang), jnp.sin(ang)
    cos2 = jnp.concatenate([cos, cos], axis=-1)
    sin2 = jnp.concatenate([sin, sin], axis=-1)
    lead = cos.shape[:-1]
    cos_q = jnp.concatenate([jnp.ones(lead + (MLA_NOPE,), F32), cos2], axis=-1)
    sin_q = jnp.concatenate([jnp.zeros(lead + (MLA_NOPE,), F32), sin2], axis=-1)
    cs_k = jnp.concatenate([jnp.zeros(lead + (MISC_KR,), F32), cos2, sin2,
                            jnp.zeros(lead + (MISC - MISC_KR_ROT - MLA_ROPE,), F32)], axis=-1)
    return cos_q, sin_q, cs_k


def _rope_key_select():
    sel = np.zeros((MISC, MLA_QK), np.float32)
    for c in range(MLA_ROPE):
        sel[MISC_KR + c, MLA_NOPE + c] = 1.0
        sel[MISC_KR_ROT + c, MLA_NOPE + c] = 1.0
    return jnp.asarray(sel, BF16)


def _split_w_in(w):
    a0, b0 = 0, D_IN_A
    c0, d0 = b0 + D_IN_B, b0 + D_IN_B + D_IN_C
    qw, gw = GLA_QK_WIDTH, GROUP_WIDTH
    w_a = w[:, a0:b0]
    gla_ad = w[:, b0 + 2 * qw + gw:b0 + 2 * qw + gw + GLA_GATE_RANK]
    w_b = jnp.concatenate([w[:, b0:b0 + 2 * qw + gw], w[:, b0 + 2 * qw + gw + GLA_GATE_RANK:c0]], axis=1)
    w_c = w[:, c0:c0 + 4 * gw]
    gates = w[:, c0 + 4 * gw:d0]
    w_d = w[:, d0:d0 + MLA_Q_RANK + MLA_KV_RANK]
    kr = w[:, d0 + MLA_Q_RANK + MLA_KV_RANK:]
    z = lambda n: jnp.zeros((w.shape[0], n), w.dtype)
    w_m = jnp.concatenate([gla_ad, gates, z(MISC_KR - MISC_FG - N_HEADS), kr, _rot_cols(kr),
                           z(MISC - MISC_KR_ROT - MLA_ROPE)], axis=1)
    return [t.astype(BF16) for t in (w_a, w_b, w_c, w_d, w_m)]


def _tiles(s):
    return min(512, s), min(512, s)


def kernel(x, p, positions, w_in, rwkv_mu, rwkv_w0, rwkv_w_up, rwkv_a0, rwkv_a_up, rwkv_g_up, rwkv_k_k, rwkv_k_a, rwkv_r_k, rwkv_gn_g, rwkv_gn_b, gla_alpha_up, gla_alpha_b, gla_norm_g, mlstm_conv_w, mlstm_conv_b, mlstm_i_b, mlstm_f_b, mlstm_norm_g, mla_q_norm_g, mla_w_uq, mla_kv_norm_g, mla_w_ukv, w_out, ln1_g, ln1_b, moe_w_rg, moe_b_rg, moe_w_re, moe_b_re, moe_w_gate, moe_w_up, moe_w_down, ple_w_gate, ple_b_gate, ple_w, ln2_g, ln2_b):
    bsz, s, d = x.shape
    t_tok = bsz * s
    depth = w_in.shape[0]
    ct, tq = _tiles(s)
    tm = min(512, t_tok)
    rope_tabs = _rope_tables(positions)
    sel_ig, sel_fg, sel_k = _head_select(MISC_IG), _head_select(MISC_FG), _rope_key_select()
    for i in range(depth):
        u_a, u_b, u_c, u_d, u_m = _project(x.reshape(t_tok, d), _split_w_in(w_in[i]), tm)
        u_a, u_b, u_c, u_d, u_m = (t.reshape(bsz, s, -1) for t in (u_a, u_b, u_c, u_d, u_m))

        rk = RWKV_DECAY_RANK
        ya = _rwkv(u_a, (_row(rwkv_mu[i]), _row(rwkv_w0[i]), _pad_rows(rwkv_w_up[i], 0).astype(BF16),
                         _row(rwkv_a0[i]), _pad_rows(rwkv_a_up[i], rk).astype(BF16),
                         _pad_rows(rwkv_g_up[i], rk + RWKV_ICL_RANK).astype(BF16),
                         _row(rwkv_k_k[i]), _row(rwkv_k_a[i]), _row(rwkv_r_k[i]),
                         _row(rwkv_gn_g[i]), _row(rwkv_gn_b[i])), ct)
        yb = _gla(u_b, u_m, (_pad_rows(gla_alpha_up[i], MISC_GLA_AD).astype(BF16), _row(gla_alpha_b[i]),
                             _row(gla_norm_g[i])), ct)
        yc = _mlstm(u_c, u_m, (mlstm_conv_w[i], _row(mlstm_conv_b[i]), sel_ig, sel_fg,
                               _expand_heads(mlstm_i_b[i]), _expand_heads(mlstm_f_b[i]),
                               _row(mlstm_norm_g[i])), ct)

        wq = mla_w_uq[i].reshape(MLA_Q_RANK, N_HEADS, MLA_QK).transpose(1, 0, 2)
        wq_rot = jnp.concatenate([jnp.zeros((N_HEADS, MLA_Q_RANK, MLA_NOPE), F32),
                                  jax.vmap(_rot_cols)(wq[:, :, MLA_NOPE:])], axis=2)
        wkv = mla_w_ukv[i].reshape(MLA_KV_RANK, N_HEADS, MLA_NOPE + HEAD_DIM).transpose(1, 0, 2)
        wk = jnp.concatenate([wkv[:, :, :MLA_NOPE], jnp.zeros((N_HEADS, MLA_KV_RANK, MLA_ROPE), F32)], axis=2)
        q, k, v = _mla_prep(u_d, u_m, rope_tabs,
                            (_row(mla_q_norm_g[i]), _row(mla_kv_norm_g[i]), wq.astype(BF16), wq_rot.astype(BF16),
                             wk.astype(BF16), wkv[:, :, MLA_NOPE:].astype(BF16), sel_k), ct)
        yd = _attention(q, k, v, tq)

        wo = w_out[i].astype(BF16)
        x = _outproj(x, ya, yb, yc, yd, wo[:3 * GROUP_WIDTH], wo[3 * GROUP_WIDTH:].reshape(N_HEADS, HEAD_DIM, d),
                     _row(ln1_g[i]), _row(ln1_b[i]), ct)

        x2d = x.reshape(t_tok, d)
        w_route = jnp.concatenate([moe_w_rg[i].T, jnp.zeros((8 - N_EXPERT_GROUPS, d), F32), moe_w_re[i].T], axis=0)
        b_route = jnp.concatenate([moe_b_rg[i], jnp.zeros((8 - N_EXPERT_GROUPS,), F32), moe_b_re[i]]).reshape(-1, 1)
        idx, gate = _router(x2d, w_route, b_route, tm)
        block_expert, row_tok, pos, row_gate = _dispatch(idx, gate, t_tok)
        y_rows = _moe(x2d.astype(BF16)[row_tok], block_expert, row_gate, moe_w_gate, moe_w_up, moe_w_down, i)
        x = _final(x2d, p[i].reshape(t_tok, -1), y_rows[pos[0]], y_rows[pos[1]], ple_w_gate[i].astype(BF16),
                   _row(ple_b_gate[i]), ple_w[i].astype(BF16), _row(ln2_g[i]), _row(ln2_b[i]), tm).reshape(bsz, s, d)
    return x
```
